```python
import math
import jax, jax.numpy as jnp
from jax import lax
import numpy as np

D_MODEL = 1024
BATCH = 8
SEQ = 2048
DEPTH = 4
DEC_BATCH = 2
DEC_SEQ = 16384
PAST_LEN = 128

N_META = 16
BLOCK = 128
PAD = BLOCK - N_META
RET_HEADS = 4
RET_HEAD_DIM = 128
RET_WIDTH = RET_HEADS * RET_HEAD_DIM
CONV_WIDTH = 512
CONV_TAPS = 3
ATT_Q_HEADS = 8
ATT_KV_HEADS = 2
ATT_HEAD_DIM = 64
ATT_GROUPS = ATT_Q_HEADS // ATT_KV_HEADS
ATT_WIDTH = ATT_Q_HEADS * ATT_HEAD_DIM
ATT_KV_WIDTH = ATT_KV_HEADS * ATT_HEAD_DIM
WINDOW = 128
N_BRANCH = 3
D_FF = 2816
ROPE_THETA = 10000.0
EPS = 1e-6
NEG_INF = -1e30
SPLITS = (RET_WIDTH, RET_WIDTH, RET_WIDTH, RET_WIDTH,
          CONV_WIDTH, CONV_WIDTH, CONV_WIDTH,
          ATT_WIDTH, ATT_KV_WIDTH, ATT_KV_WIDTH,
          N_BRANCH * D_MODEL)
IN_WIDTH = sum(SPLITS)

kernel_name = "hybrid_retention_conv_swa_encoder"


def rms_norm(x, g):
    xf = x.astype(jnp.float32)
    y = xf * lax.rsqrt(jnp.mean(xf * xf, axis=-1, keepdims=True) + EPS)
    return (y * g.astype(jnp.float32)).astype(x.dtype)


def swiglu(x, wg, wu, wd):
    return (jax.nn.silu(x @ wg) * (x @ wu)) @ wd


def rotary(x, pos):
    d = x.shape[-1]
    inv = ROPE_THETA ** (-jnp.arange(0, d, 2, dtype=jnp.float32) / d)
    ang = pos[:, None] * inv[None, :]
    cos = jnp.cos(ang)[None, :, None, :].astype(x.dtype)
    sin = jnp.sin(ang)[None, :, None, :].astype(x.dtype)
    x1, x2 = x[..., : d // 2], x[..., d // 2:]
    return jnp.concatenate([x1 * cos - x2 * sin, x2 * cos + x1 * sin], axis=-1)


def pad_front(t):
    return jnp.pad(t, [(0, 0), (PAD, 0)] + [(0, 0)] * (t.ndim - 2))


def retention_scan(q, k, v, log_gamma, strict):
    b, p, h, _ = q.shape
    dv = v.shape[-1]
    n = p // BLOCK
    idx = jnp.arange(BLOCK, dtype=jnp.float32)
    diff = idx[:, None] - idx[None, :]
    keep = (diff > 0) if strict else (diff >= 0)
    dmat = jnp.where(keep[None], jnp.exp(log_gamma[:, None, None] * jnp.maximum(diff, 0.0)[None]), 0.0)
    q_dec = jnp.exp(log_gamma[:, None] * (idx + 1.0)[None])
    k_dec = jnp.exp(log_gamma[:, None] * (BLOCK - 1.0 - idx)[None])
    c_dec = jnp.exp(log_gamma * BLOCK)

    def chunks(t):
        return t.reshape(b, n, BLOCK, h, t.shape[-1]).swapaxes(0, 1)

    def step(state, inp):
        qc, kc, vc = inp
        att = jnp.einsum('bihd,bjhd->bhij', qc, kc) * dmat[None]
        o = (jnp.einsum('bhij,bjhe->bihe', att, vc)
             + jnp.einsum('bihd,hi,bhde->bihe', qc, q_dec, state))
        state = (c_dec[None, :, None, None] * state
                 + jnp.einsum('bjhd,hj,bjhe->bhde', kc, k_dec, vc))
        return state, o

    s0 = jnp.zeros((b, h, q.shape[-1], dv), jnp.float32)
    _, o = lax.scan(step, s0, (chunks(q), chunks(k), chunks(v)))
    return o.swapaxes(0, 1).reshape(b, p, h, dv)


def retention(q, k, v, g, log_decay, gn_gain, pos):
    b, l, _ = q.shape
    dt = q.dtype
    heads = lambda t: t.reshape(b, l, RET_HEADS, RET_HEAD_DIM)
    qh = pad_front(rotary(heads(q), pos).astype(jnp.float32))
    kh = pad_front((rotary(heads(k), pos) * RET_HEAD_DIM ** -0.5).astype(jnp.float32))
    vh = pad_front(heads(v).astype(jnp.float32))
    log_gamma = jnp.log1p(-jnp.exp(log_decay.astype(jnp.float32)))
    fwd = retention_scan(qh, kh, vh, log_gamma[0], False)
    rev = lambda t: jnp.flip(t, axis=1)
    bwd = rev(retention_scan(rev(qh), rev(kh), rev(vh), log_gamma[1], True))
    o = (fwd + bwd)[:, PAD:]
    mu = jnp.mean(o, axis=-1, keepdims=True)
    var = jnp.mean(jnp.square(o - mu), axis=-1, keepdims=True)
    o = ((o - mu) * lax.rsqrt(var + EPS)).reshape(b, l, RET_WIDTH) * gn_gain.astype(jnp.float32)
    return (jax.nn.silu(g.astype(jnp.float32)) * o).astype(dt)


def short_conv(bg, cg, xc, w):
    u = cg * xc
    up = jnp.pad(u, ((0, 0), (1, 1), (0, 0)))
    y = up[:, :-2] * w[0] + up[:, 1:-1] * w[1] + up[:, 2:] * w[2]
    return bg * y


def window_attention(q, k, v, sink):
    b, l = q.shape[:2]
    q, k, v = pad_front(q), pad_front(k), pad_front(v)
    p = l + PAD
    nb = p // BLOCK
    qb = q.reshape(b, nb, BLOCK, ATT_KV_HEADS, ATT_GROUPS, ATT_HEAD_DIM)

    def neighbours(t):
        t = t.reshape(b, nb, BLOCK, ATT_KV_HEADS, ATT_HEAD_DIM)
        t = jnp.pad(t, ((0, 0), (1, 1), (0, 0), (0, 0), (0, 0)))
        return jnp.concatenate([t[:, :-2], t[:, 1:-1], t[:, 2:]], axis=2)

    kb, vb = neighbours(k), neighbours(v)
    s = jnp.einsum('bnqkgd,bnskd->bnkgqs', qb, kb).astype(jnp.float32) * (ATT_HEAD_DIM ** -0.5)
    start = jnp.arange(nb)[:, None] * BLOCK
    qpos = start + jnp.arange(BLOCK)[None, :]
    kpos = start - BLOCK + jnp.arange(3 * BLOCK)[None, :]
    kp = kpos[:, None, :]
    ok = (jnp.abs(qpos[:, :, None] - kp) <= WINDOW) & (kp >= PAD) & (kp < p)
    s = jnp.where(ok[None, :, None, None], s, NEG_INF)
    sink_col = jnp.broadcast_to(
        sink.astype(jnp.float32).reshape(1, 1, ATT_KV_HEADS, ATT_GROUPS, 1, 1), s.shape[:-1] + (1,))
    probs = jax.nn.softmax(jnp.concatenate([s, sink_col], axis=-1), axis=-1)[..., :-1]
    o = jnp.einsum('bnkgqs,bnskd->bnqkgd', probs.astype(v.dtype), vb)
    return o.reshape(b, p, ATT_WIDTH)[:, PAD:]


def mixer(h, w_in, ret_decay, ret_gn_gain, conv_w, attn_sink, w_ret_out, w_conv_out, w_attn_out, w_o, pos):
    b, l, _ = h.shape
    offsets = []
    acc = 0
    for s in SPLITS[:-1]:
        acc += s
        offsets.append(acc)
    rq, rk, rv, rg, cb, cc, cx, aq, ak, av, gates = jnp.split(h @ w_in, offsets, axis=-1)
    y_ret = retention(rq, rk, rv, rg, ret_decay, ret_gn_gain, pos) @ w_ret_out
    y_conv = short_conv(cb, cc, cx, conv_w) @ w_conv_out
    aq = rotary(aq.reshape(b, l, ATT_Q_HEADS, ATT_HEAD_DIM), pos)
    ak = rotary(ak.reshape(b, l, ATT_KV_HEADS, ATT_HEAD_DIM), pos)
    av = av.reshape(b, l, ATT_KV_HEADS, ATT_HEAD_DIM)
    y_att = window_attention(aq, ak, av, attn_sink) @ w_attn_out
    g = jax.nn.sigmoid(gates.reshape(b, l, N_BRANCH, D_MODEL))
    merged = g[:, :, 0] * y_ret + g[:, :, 1] * y_conv + g[:, :, 2] * y_att
    return merged @ w_o


def trunk(x, meta_tokens, norm_ffn1, w_ffn1_gate, w_ffn1_up, w_ffn1_down, norm_mix, w_in,
          ret_decay, ret_gn_gain, conv_w, attn_sink, w_ret_out, w_conv_out, w_attn_out, w_o,
          norm_ffn2, w_ffn2_gate, w_ffn2_up, w_ffn2_down, final_norm):
    b = x.shape[0]
    meta = jnp.broadcast_to(meta_tokens.astype(x.dtype)[None], (b, N_META, D_MODEL))
    h = jnp.concatenate([meta, x], axis=1)
    pos = jnp.arange(h.shape[1], dtype=jnp.float32)
    for l in range(DEPTH):
        h = h + 0.5 * swiglu(rms_norm(h, norm_ffn1[l]), w_ffn1_gate[l], w_ffn1_up[l], w_ffn1_down[l])
        h = h + mixer(rms_norm(h, norm_mix[l]), w_in[l], ret_decay[l], ret_gn_gain[l], conv_w[l],
                      attn_sink[l], w_ret_out[l], w_conv_out[l], w_attn_out[l], w_o[l], pos)
        h = h + 0.5 * swiglu(rms_norm(h, norm_ffn2[l]), w_ffn2_gate[l], w_ffn2_up[l], w_ffn2_down[l])
    return rms_norm(h, final_norm)[:, N_META:]


def setup_inputs(seed: int = 0) -> dict:
    key = jax.random.key(seed)
    ks = jax.random.split(key, 24)
    f32 = jnp.float32

    def dense(k, shape):
        return jax.random.normal(k, shape, f32) * (shape[-2] ** -0.5)

    def gain(k, shape):
        return 1.0 + 0.01 * jax.random.normal(k, shape, f32)

    base_decay = -(5.0 + jnp.arange(RET_HEADS, dtype=f32)) * math.log(2.0)
    return {
        "x_prompt": jax.random.normal(ks[0], (BATCH, SEQ, D_MODEL), f32),
        "x_sample": jax.random.normal(ks[1], (DEC_BATCH, DEC_SEQ, D_MODEL), f32),
        "meta_tokens": jax.random.normal(ks[2], (N_META, D_MODEL), f32),
        "norm_ffn1": gain(ks[3], (DEPTH, D_MODEL)),
        "w_ffn1_gate": dense(ks[4], (DEPTH, D_MODEL, D_FF)),
        "w_ffn1_up": dense(ks[5], (DEPTH, D_MODEL, D_FF)),
        "w_ffn1_down": dense(ks[6], (DEPTH, D_FF, D_MODEL)),
        "norm_mix": gain(ks[7], (DEPTH, D_MODEL)),
        "w_in": dense(ks[8], (DEPTH, D_MODEL, IN_WIDTH)),
        "ret_decay": base_decay[None, None, :] + 0.05 * jax.random.normal(ks[9], (DEPTH, 2, RET_HEADS), f32),
        "ret_gn_gain": gain(ks[10], (DEPTH, RET_WIDTH)),
        "conv_w": jax.random.normal(ks[11], (DEPTH, CONV_TAPS, CONV_WIDTH), f32) * (CONV_TAPS ** -0.5),
        "attn_sink": 0.5 * jax.random.normal(ks[12], (DEPTH, ATT_Q_HEADS), f32),
        "w_ret_out": dense(ks[13], (DEPTH, RET_WIDTH, D_MODEL)),
        "w_conv_out": dense(ks[14], (DEPTH, CONV_WIDTH, D_MODEL)),
        "w_attn_out": dense(ks[15], (DEPTH, ATT_WIDTH, D_MODEL)),
        "w_o": dense(ks[16], (DEPTH, D_MODEL, D_MODEL)),
        "norm_ffn2": gain(ks[17], (DEPTH, D_MODEL)),
        "w_ffn2_gate": dense(ks[18], (DEPTH, D_MODEL, D_FF)),
        "w_ffn2_up": dense(ks[19], (DEPTH, D_MODEL, D_FF)),
        "w_ffn2_down": dense(ks[20], (DEPTH, D_FF, D_MODEL)),
        "final_norm": gain(ks[21], (D_MODEL,)),
    }


def reference(x_prompt, x_sample, meta_tokens, norm_ffn1, w_ffn1_gate, w_ffn1_up, w_ffn1_down, norm_mix,
              w_in, ret_decay, ret_gn_gain, conv_w, attn_sink, w_ret_out, w_conv_out, w_attn_out, w_o,
              norm_ffn2, w_ffn2_gate, w_ffn2_up, w_ffn2_down, final_norm):
    y_prompt = trunk(x_prompt, meta_tokens, norm_ffn1, w_ffn1_gate, w_ffn1_up, w_ffn1_down, norm_mix, w_in,
                     ret_decay, ret_gn_gain, conv_w, attn_sink, w_ret_out, w_conv_out, w_attn_out, w_o,
                     norm_ffn2, w_ffn2_gate, w_ffn2_up, w_ffn2_down, final_norm)
    y_sample = trunk(x_sample, meta_tokens, norm_ffn1, w_ffn1_gate, w_ffn1_up, w_ffn1_down, norm_mix, w_in,
                     ret_decay, ret_gn_gain, conv_w, attn_sink, w_ret_out, w_conv_out, w_attn_out, w_o,
                     norm_ffn2, w_ffn2_gate, w_ffn2_up, w_ffn2_down, final_norm)
    return (y_prompt, y_sample)
```

```python
import functools
import math

import jax
import jax.numpy as jnp
from jax import lax
from jax.experimental import pallas as pl
from jax.experimental.pallas import tpu as pltpu

F32 = jnp.float32
BF16 = jnp.bfloat16

D_MODEL = 1024
D_FF = 2816
N_META = 16
CHUNK = 128
PAD = CHUNK - N_META
RET_HEADS = 4
RET_DIM = 128
RET_WIDTH = RET_HEADS * RET_DIM
CONV_WIDTH = 512
ATT_Q_HEADS = 8
ATT_KV_HEADS = 2
ATT_DIM = 64
ATT_GROUPS = ATT_Q_HEADS // ATT_KV_HEADS
ATT_WIDTH = ATT_Q_HEADS * ATT_DIM
ATT_KV_WIDTH = ATT_KV_HEADS * ATT_DIM
WINDOW = 128
GATE_WIDTH = 3 * D_MODEL
ROPE_THETA = 10000.0
EPS = 1e-6
NEG_INF = -1e30

LANES = 128
HALO_ROWS = 16
FF_CHUNK = 256
VMEM_LIMIT = 56 * 1024 * 1024

OFF_RQ = 0
OFF_RK = OFF_RQ + RET_WIDTH
OFF_RV = OFF_RK + RET_WIDTH
OFF_RG = OFF_RV + RET_WIDTH
OFF_CB = OFF_RG + RET_WIDTH
OFF_CC = OFF_CB + CONV_WIDTH
OFF_CX = OFF_CC + CONV_WIDTH
OFF_AQ = OFF_CX + CONV_WIDTH
OFF_AK = OFF_AQ + ATT_WIDTH
OFF_AV = OFF_AK + ATT_KV_WIDTH
OFF_GATE = OFF_AV + ATT_KV_WIDTH
IN_WIDTH = OFF_GATE + GATE_WIDTH


def _params(*sem):
    return pltpu.CompilerParams(dimension_semantics=sem, vmem_limit_bytes=VMEM_LIMIT)


def _resident(shape):
    nd = len(shape)
    return pl.BlockSpec(shape, lambda *_: (0,) * nd, pipeline_mode=pl.Buffered(1))


def _row_tile(rows, cap):
    best = CHUNK
    t = CHUNK
    while t <= cap:
        if rows % t == 0:
            best = t
        t += CHUNK
    return best


def _rms(x, gain):
    ms = jnp.mean(x * x, axis=-1, keepdims=True)
    return x * lax.rsqrt(ms + EPS) * gain


def _ffn_kernel(x_ref, g_ref, wg_ref, wu_ref, wd_ref, o_ref, acc_ref):
    x = x_ref[...]
    n = _rms(x, g_ref[...]).astype(BF16)
    for c in range(D_FF // FF_CHUNK):
        sl = slice(c * FF_CHUNK, (c + 1) * FF_CHUNK)
        g = jnp.dot(n, wg_ref[:, sl], preferred_element_type=F32)
        u = jnp.dot(n, wu_ref[:, sl], preferred_element_type=F32)
        a = (g * jax.nn.sigmoid(g) * u).astype(BF16)
        d = jnp.dot(a, wd_ref[sl, :], preferred_element_type=F32)
        if c == 0:
            acc_ref[...] = d
        else:
            acc_ref[...] += d
    o_ref[...] = x + 0.5 * acc_ref[...]


def _ffn(h, gain, wg, wu, wd):
    rows = h.shape[0]
    tm = _row_tile(rows, 768)
    return pl.pallas_call(
        _ffn_kernel,
        out_shape=jax.ShapeDtypeStruct((rows, D_MODEL), F32),
        grid=(rows // tm,),
        in_specs=[
            pl.BlockSpec((tm, D_MODEL), lambda i: (i, 0)),
            _resident((1, D_MODEL)),
            _resident((D_MODEL, D_FF)),
            _resident((D_MODEL, D_FF)),
            _resident((D_FF, D_MODEL)),
        ],
        out_specs=pl.BlockSpec((tm, D_MODEL), lambda i: (i, 0)),
        scratch_shapes=[pltpu.VMEM((tm, D_MODEL), F32)],
        compiler_params=_params("parallel"),
        name="ffn",
    )(h, gain, wg, wu, wd)


def _rot128(x, cos, sin):
    return x * cos + pltpu.roll(x, 64, axis=1) * sin


def _rot64(x, cos, sin, low_half):
    partner = jnp.where(low_half, pltpu.roll(x, 96, axis=1), pltpu.roll(x, 32, axis=1))
    return x * cos + partner * sin


def _inproj_kernel(x_ref, g_ref, w_ref, cr_ref, sr_ref, ca_ref, sa_ref,
                   rq_ref, rk_ref, rv_ref, rg_ref, cb_ref, u_ref, aq_ref, ak_ref, av_ref, gs_ref):
    n = _rms(x_ref[...], g_ref[...]).astype(BF16)

    def proj(off, width):
        return jnp.dot(n, w_ref[:, off:off + width], preferred_element_type=F32)

    cr, sr = cr_ref[...], sr_ref[...]
    ca, sa = ca_ref[...], sa_ref[...]
    low_half = (lax.broadcasted_iota(jnp.int32, (1, LANES), 1) % ATT_DIM) < (ATT_DIM // 2)
    for hd in range(RET_HEADS):
        sl = slice(hd * RET_DIM, (hd + 1) * RET_DIM)
        rq_ref[:, sl] = _rot128(proj(OFF_RQ + hd * RET_DIM, RET_DIM), cr, sr).astype(BF16)
        rk = _rot128(proj(OFF_RK + hd * RET_DIM, RET_DIM), cr, sr) * (RET_DIM ** -0.5)
        rk_ref[:, sl] = rk.astype(BF16)
    rv_ref[...] = proj(OFF_RV, RET_WIDTH).astype(BF16)
    rg_ref[...] = proj(OFF_RG, RET_WIDTH).astype(BF16)
    cb_ref[...] = proj(OFF_CB, CONV_WIDTH).astype(BF16)
    u_ref[...] = (proj(OFF_CC, CONV_WIDTH) * proj(OFF_CX, CONV_WIDTH)).astype(BF16)
    for c in range(ATT_WIDTH // LANES):
        sl = slice(c * LANES, (c + 1) * LANES)
        aq = _rot64(proj(OFF_AQ + c * LANES, LANES), ca, sa, low_half) * (ATT_DIM ** -0.5)
        aq_ref[:, sl] = aq.astype(BF16)
    ak_ref[...] = _rot64(proj(OFF_AK, ATT_KV_WIDTH), ca, sa, low_half).astype(BF16)
    av_ref[...] = proj(OFF_AV, ATT_KV_WIDTH).astype(BF16)
    for c in range(GATE_WIDTH // 512):
        sl = slice(c * 512, (c + 1) * 512)
        gs_ref[:, sl] = jax.nn.sigmoid(proj(OFF_GATE + c * 512, 512)).astype(BF16)


def _inproj(h, gain, w_in, tabs):
    rows = h.shape[0]
    tm = _row_tile(rows, 512)
    row = lambda w: pl.BlockSpec((tm, w), lambda i: (i, 0))
    widths = (RET_WIDTH, RET_WIDTH, RET_WIDTH, RET_WIDTH, CONV_WIDTH, CONV_WIDTH,
              ATT_WIDTH, ATT_KV_WIDTH, ATT_KV_WIDTH, GATE_WIDTH)
    return pl.pallas_call(
        _inproj_kernel,
        out_shape=[jax.ShapeDtypeStruct((rows, w), BF16) for w in widths],
        grid=(rows // tm,),
        in_specs=[row(D_MODEL), _resident((1, D_MODEL)), _resident((D_MODEL, IN_WIDTH)),
                  row(LANES), row(LANES), row(LANES), row(LANES)],
        out_specs=[row(w) for w in widths],
        compiler_params=_params("parallel"),
        name="inproj",
    )(h, gain, w_in, *tabs)


def _log_gamma(dec_ref, direction, head, shape):
    return jnp.log1p(-jnp.exp(jnp.full(shape, dec_ref[direction, head], F32)))


def _ret_bwd_kernel(dec_ref, q_ref, k_ref, v_ref, ob_ref, state_ref):
    c = pl.program_id(1)

    @pl.when(c == 0)
    def _():
        state_ref[...] = jnp.zeros_like(state_ref)

    row = lax.broadcasted_iota(jnp.int32, (CHUNK, 1), 0).astype(F32)
    for hd in range(RET_HEADS):
        sl = slice(hd * RET_DIM, (hd + 1) * RET_DIM)
        lg = _log_gamma(dec_ref, 1, hd, (CHUNK, 1))
        q = q_ref[0, :, sl]
        k = k_ref[0, :, sl].astype(F32)
        v = v_ref[0, :, sl].astype(F32)
        s = state_ref[hd]
        ob_ref[0, :, sl] = jnp.exp(lg * (CHUNK - row)) * jnp.dot(
            q, s.astype(BF16), preferred_element_type=F32)
        vd = (v * jnp.exp(lg * row)).astype(BF16)
        kt = k.T.astype(BF16)
        state_ref[hd] = jnp.exp(lg[:1] * CHUNK) * s + jnp.dot(kt, vd, preferred_element_type=F32)


def _ret_fwd_kernel(dec_ref, q_ref, k_ref, v_ref, g_ref, ob_ref, gain_ref, y_ref, state_ref, dmat_ref):
    c = pl.program_id(1)

    @pl.when(c == 0)
    def _():
        state_ref[...] = jnp.zeros_like(state_ref)
        i = lax.broadcasted_iota(jnp.int32, (CHUNK, CHUNK), 0)
        j = lax.broadcasted_iota(jnp.int32, (CHUNK, CHUNK), 1)
        diff = (i - j).astype(F32)
        for hd in range(RET_HEADS):
            lgf = _log_gamma(dec_ref, 0, hd, (CHUNK, CHUNK))
            lgb = _log_gamma(dec_ref, 1, hd, (CHUNK, CHUNK))
            dmat_ref[hd] = jnp.where(diff >= 0, jnp.exp(lgf * jnp.maximum(diff, 0.0)),
                                     jnp.exp(lgb * jnp.maximum(-diff, 0.0)))

    row = lax.broadcasted_iota(jnp.int32, (CHUNK, 1), 0).astype(F32)
    for hd in range(RET_HEADS):
        sl = slice(hd * RET_DIM, (hd + 1) * RET_DIM)
        lg = _log_gamma(dec_ref, 0, hd, (CHUNK, 1))
        q = q_ref[0, :, sl]
        k = k_ref[0, :, sl]
        v = v_ref[0, :, sl]
        s = state_ref[hd]
        att = lax.dot_general(q, k, (((1,), (1,)), ((), ())), preferred_element_type=F32) * dmat_ref[hd]
        o = jnp.dot(att.astype(BF16), v, preferred_element_type=F32)
        o += jnp.exp(lg * (row + 1.0)) * jnp.dot(q, s.astype(BF16), preferred_element_type=F32)
        o += ob_ref[0, :, sl]
        vd = (v.astype(F32) * jnp.exp(lg * (CHUNK - 1.0 - row))).astype(BF16)
        kt = k.astype(F32).T.astype(BF16)
        state_ref[hd] = jnp.exp(lg[:1] * CHUNK) * s + jnp.dot(kt, vd, preferred_element_type=F32)
        mu = jnp.mean(o, axis=-1, keepdims=True)
        oc = o - mu
        var = jnp.mean(oc * oc, axis=-1, keepdims=True)
        on = oc * lax.rsqrt(var + EPS) * gain_ref[:, sl]
        g = g_ref[0, :, sl].astype(F32)
        y_ref[0, :, sl] = (g * jax.nn.sigmoid(g) * on).astype(BF16)


def _retention(dec, rq, rk, rv, rg, gn_gain, batch):
    rows = rq.shape[0]
    p = rows // batch
    n = p // CHUNK
    v3 = lambda t: t.reshape(batch, p, RET_WIDTH)
    smem = pl.BlockSpec(memory_space=pltpu.SMEM)
    rev = pl.BlockSpec((1, CHUNK, RET_WIDTH), lambda b, c: (b, n - 1 - c, 0))
    fwd = pl.BlockSpec((1, CHUNK, RET_WIDTH), lambda b, c: (b, c, 0))
    state = pltpu.VMEM((RET_HEADS, RET_DIM, RET_DIM), F32)
    ob = pl.pallas_call(
        _ret_bwd_kernel,
        out_shape=jax.ShapeDtypeStruct((batch, p, RET_WIDTH), F32),
        grid=(batch, n),
        in_specs=[smem, rev, rev, rev],
        out_specs=rev,
        scratch_shapes=[state],
        compiler_params=_params("parallel", "arbitrary"),
        name="ret_bwd",
    )(dec, v3(rq), v3(rk), v3(rv))
    y = pl.pallas_call(
        _ret_fwd_kernel,
        out_shape=jax.ShapeDtypeStruct((batch, p, RET_WIDTH), BF16),
        grid=(batch, n),
        in_specs=[smem, fwd, fwd, fwd, fwd, fwd, _resident((1, RET_WIDTH))],
        out_specs=fwd,
        scratch_shapes=[state, pltpu.VMEM((RET_HEADS, CHUNK, CHUNK), F32)],
        compiler_params=_params("parallel", "arbitrary"),
        name="ret_fwd",
    )(dec, v3(rq), v3(rk), v3(rv), v3(rg), ob, gn_gain)
    return y.reshape(rows, RET_WIDTH)


def _attn_kernel(sink_ref, q_ref, kp_ref, kc_ref, kn_ref, vp_ref, vc_ref, vn_ref, y_ref, *, seq_rows):
    i = pl.program_id(1)
    k = jnp.concatenate([kp_ref[0], kc_ref[0], kn_ref[0]], axis=0)
    v = jnp.concatenate([vp_ref[0], vc_ref[0], vn_ref[0]], axis=0)
    r = lax.broadcasted_iota(jnp.int32, (CHUNK, 3 * CHUNK), 0)
    j = lax.broadcasted_iota(jnp.int32, (CHUNK, 3 * CHUNK), 1)
    kpos = (i - 1) * CHUNK + j
    ok = (jnp.abs(CHUNK + r - j) <= WINDOW) & (kpos >= PAD) & (kpos < seq_rows)
    lane = lax.broadcasted_iota(jnp.int32, (1, LANES), 1)
    head0 = lane < ATT_DIM
    for c in range(ATT_GROUPS):
        q2 = q_ref[0, :, c * LANES:(c + 1) * LANES]
        outs = []
        for kv in range(ATT_KV_HEADS):
            qm = jnp.where(head0 if kv == 0 else ~head0, q2, jnp.zeros_like(q2))
            s = lax.dot_general(qm, k, (((1,), (1,)), ((), ())), preferred_element_type=F32)
            s = jnp.where(ok, s, NEG_INF)
            sink = sink_ref[kv * ATT_GROUPS + c]
            m = jnp.maximum(jnp.max(s, axis=-1, keepdims=True), sink)
            e = jnp.exp(s - m)
            denom = jnp.sum(e, axis=-1, keepdims=True) + jnp.exp(sink - m)
            o = jnp.dot(e.astype(BF16), v, preferred_element_type=F32)
            outs.append(o / denom)
        y_ref[0, :, c * LANES:(c + 1) * LANES] = jnp.where(head0, outs[0], outs[1]).astype(BF16)


def _attention(sink, aq, ak, av, batch):
    rows = aq.shape[0]
    p = rows // batch
    nb = p // CHUNK
    smem = pl.BlockSpec(memory_space=pltpu.SMEM)
    qspec = pl.BlockSpec((1, CHUNK, ATT_WIDTH), lambda b, i: (b, i, 0))
    prev = pl.BlockSpec((1, CHUNK, ATT_KV_WIDTH), lambda b, i: (b, jnp.maximum(i - 1, 0), 0))
    cur = pl.BlockSpec((1, CHUNK, ATT_KV_WIDTH), lambda b, i: (b, i, 0))
    nxt = pl.BlockSpec((1, CHUNK, ATT_KV_WIDTH), lambda b, i: (b, jnp.minimum(i + 1, nb - 1), 0))
    k3 = ak.reshape(batch, p, ATT_KV_WIDTH)
    v3 = av.reshape(batch, p, ATT_KV_WIDTH)
    y = pl.pallas_call(
        functools.partial(_attn_kernel, seq_rows=p),
        out_shape=jax.ShapeDtypeStruct((batch, p, ATT_WIDTH), BF16),
        grid=(batch, nb),
        in_specs=[smem, qspec, prev, cur, nxt, prev, cur, nxt],
        out_specs=qspec,
        compiler_params=_params("parallel", "parallel"),
        name="attn",
    )(sink, aq.reshape(batch, p, ATT_WIDTH), k3, k3, k3, v3, v3, v3)
    return y.reshape(rows, ATT_WIDTH)


def _merge_kernel(h_ref, yr_ref, ya_ref, cb_ref, u_ref, up_ref, un_ref, gs_ref, cw_ref,
                  wr_ref, wc_ref, wa_ref, wo_ref, o_ref, us_ref, *, seq_rows):
    i = pl.program_id(0)
    tm = h_ref.shape[0]
    us_ref[pl.ds(8, tm), :] = u_ref[...].astype(F32)
    us_ref[pl.ds(0, 8), :] = up_ref[pl.ds(HALO_ROWS - 8, 8), :].astype(F32)
    nxt = un_ref[pl.ds(0, 8), :].astype(F32)
    us_ref[pl.ds(8 + tm, 8), :] = jnp.where(i == pl.num_programs(0) - 1, 0.0, nxt)
    conv = (us_ref[pl.ds(7, tm), :] * cw_ref[0:1, :] + us_ref[pl.ds(8, tm), :] * cw_ref[1:2, :]
            + us_ref[pl.ds(9, tm), :] * cw_ref[2:3, :])
    yc = (cb_ref[...].astype(F32) * conv).astype(BF16)

    def branch(y, w_ref, b):
        gate = gs_ref[:, b * D_MODEL:(b + 1) * D_MODEL].astype(F32)
        return gate * jnp.dot(y, w_ref[...], preferred_element_type=F32)

    merged = branch(yr_ref[...], wr_ref, 0) + branch(yc, wc_ref, 1) + branch(ya_ref[...], wa_ref, 2)
    mix = jnp.dot(merged.astype(BF16), wo_ref[...], preferred_element_type=F32)
    pos = (i * tm + lax.broadcasted_iota(jnp.int32, (tm, 1), 0)) % seq_rows
    o_ref[...] = h_ref[...] + jnp.where(pos < PAD, 0.0, mix)


def _merge(h, yr, ya, cb, u, gs, conv_w, wr, wc, wa, wo, seq_rows):
    rows = h.shape[0]
    tm = _row_tile(rows, 512)
    hb = tm // HALO_ROWS
    last = rows // HALO_ROWS - 1
    row = lambda w: pl.BlockSpec((tm, w), lambda i: (i, 0))
    halo_prev = pl.BlockSpec((HALO_ROWS, CONV_WIDTH), lambda i: (jnp.maximum(i * hb - 1, 0), 0))
    halo_next = pl.BlockSpec((HALO_ROWS, CONV_WIDTH), lambda i: (jnp.minimum((i + 1) * hb, last), 0))
    return pl.pallas_call(
        functools.partial(_merge_kernel, seq_rows=seq_rows),
        out_shape=jax.ShapeDtypeStruct((rows, D_MODEL), F32),
        grid=(rows // tm,),
        in_specs=[row(D_MODEL), row(RET_WIDTH), row(ATT_WIDTH), row(CONV_WIDTH), row(CONV_WIDTH),
                  halo_prev, halo_next, row(GATE_WIDTH), _resident((3, CONV_WIDTH)),
                  _resident((RET_WIDTH, D_MODEL)), _resident((CONV_WIDTH, D_MODEL)),
                  _resident((ATT_WIDTH, D_MODEL)), _resident((D_MODEL, D_MODEL))],
        out_specs=row(D_MODEL),
        scratch_shapes=[pltpu.VMEM((tm + 16, CONV_WIDTH), F32)],
        compiler_params=_params("parallel"),
        name="merge",
    )(h, yr, ya, cb, u, u, u, gs, conv_w, wr, wc, wa, wo)


def _final_kernel(x_ref, g_ref, o_ref):
    o_ref[0] = _rms(x_ref[0], g_ref[...])


def _final(h3, gain):
    batch, p, _ = h3.shape
    s = p - CHUNK
    return pl.pallas_call(
        _final_kernel,
        out_shape=jax.ShapeDtypeStruct((batch, s, D_MODEL), F32),
        grid=(batch, s // CHUNK),
        in_specs=[pl.BlockSpec((1, CHUNK, D_MODEL), lambda b, j: (b, j + 1, 0)), _resident((1, D_MODEL))],
        out_specs=pl.BlockSpec((1, CHUNK, D_MODEL), lambda b, j: (b, j, 0)),
        compiler_params=_params("parallel", "parallel"),
        name="final_norm",
    )(h3, gain)


def _rope_tables(p, batch):
    pos = jnp.arange(p, dtype=F32) - float(PAD)

    def tab(d):
        inv = ROPE_THETA ** (-jnp.arange(0, d, 2, dtype=F32) / d)
        ang = pos[:, None] * inv[None, :]
        cos, sin = jnp.cos(ang), jnp.sin(ang)
        reps = LANES // d
        c = jnp.tile(jnp.concatenate([cos, cos], axis=1), (batch, reps))
        s = jnp.tile(jnp.concatenate([-sin, sin], axis=1), (batch, reps))
        return c, s

    cr, sr = tab(RET_DIM)
    ca, sa = tab(ATT_DIM)
    return cr, sr, ca, sa


def _pair_heads(t, axis):
    shape = t.shape
    t = t.reshape(shape[:axis] + (ATT_KV_HEADS, ATT_GROUPS, ATT_DIM) + shape[axis + 1:])
    t = jnp.swapaxes(t, axis, axis + 1)
    return t.reshape(shape)


def _prep_weights(w_in, w_attn_out):
    aq = _pair_heads(w_in[:, :, OFF_AQ:OFF_AK], 2)
    w_in = jnp.concatenate([w_in[:, :, :OFF_AQ], aq, w_in[:, :, OFF_AK:]], axis=2)
    return w_in.astype(BF16), _pair_heads(w_attn_out, 1).astype(BF16)


def _trunk(x, meta_tokens, layers, final_norm):
    batch, s, _ = x.shape
    p = s + CHUNK
    rows = batch * p
    meta = jnp.broadcast_to(meta_tokens[None], (batch, N_META, D_MODEL))
    h = jnp.concatenate([jnp.zeros((batch, PAD, D_MODEL), F32), meta, x], axis=1).reshape(rows, D_MODEL)
    tabs = _rope_tables(p, batch)
    for ly in layers:
        h = _ffn(h, ly["norm_ffn1"], ly["wg1"], ly["wu1"], ly["wd1"])
        rq, rk, rv, rg, cb, u, aq, ak, av, gs = _inproj(h, ly["norm_mix"], ly["w_in"], tabs)
        yr = _retention(ly["ret_decay"], rq, rk, rv, rg, ly["ret_gn_gain"], batch)
        ya = _attention(ly["attn_sink"], aq, ak, av, batch)
        h = _merge(h, yr, ya, cb, u, gs, ly["conv_w"], ly["w_ret_out"], ly["w_conv_out"],
                   ly["w_attn_out"], ly["w_o"], p)
        h = _ffn(h, ly["norm_ffn2"], ly["wg2"], ly["wu2"], ly["wd2"])
    return _final(h.reshape(batch, p, D_MODEL), final_norm.reshape(1, D_MODEL))


def kernel(x_prompt, x_sample, meta_tokens, norm_ffn1, w_ffn1_gate, w_ffn1_up, w_ffn1_down, norm_mix, w_in, ret_decay, ret_gn_gain, conv_w, attn_sink, w_ret_out, w_conv_out, w_attn_out, w_o, norm_ffn2, w_ffn2_gate, w_ffn2_up, w_ffn2_down, final_norm):
    depth = w_in.shape[0]
    w_in_b, w_attn_out_b = _prep_weights(w_in, w_attn_out)
    layers = []
    for l in range(depth):
        layers.append(dict(
            norm_ffn1=norm_ffn1[l].reshape(1, D_MODEL), wg1=w_ffn1_gate[l].astype(BF16),
            wu1=w_ffn1_up[l].astype(BF16), wd1=w_ffn1_down[l].astype(BF16),
            norm_mix=norm_mix[l].reshape(1, D_MODEL), w_in=w_in_b[l],
            ret_decay=ret_decay[l], ret_gn_gain=ret_gn_gain[l].reshape(1, RET_WIDTH),
            conv_w=conv_w[l], attn_sink=attn_sink[l],
            w_ret_out=w_ret_out[l].astype(BF16), w_conv_out=w_conv_out[l].astype(BF16),
            w_attn_out=w_attn_out_b[l], w_o=w_o[l].astype(BF16),
            norm_ffn2=norm_ffn2[l].reshape(1, D_MODEL), wg2=w_ffn2_gate[l].astype(BF16),
            wu2=w_ffn2_up[l].astype(BF16), wd2=w_ffn2_down[l].astype(BF16)))
    y_prompt = _trunk(x_prompt, meta_tokens, layers, final_norm)
    y_sample = _trunk(x_sample, meta_tokens, layers, final_norm)
    return (y_prompt, y_sample)
```

```python
import functools
import math

import jax
import jax.numpy as jnp
from jax import lax
from jax.experimental import pallas as pl
from jax.experimental.pallas import tpu as pltpu

F32 = jnp.float32
BF16 = jnp.bfloat16

D_MODEL = 1024
D_FF = 2816
N_META = 16
CHUNK = 128
PAD = CHUNK - N_META
RET_HEADS = 4
RET_DIM = 128
RET_WIDTH = RET_HEADS * RET_DIM
CONV_WIDTH = 512
ATT_Q_HEADS = 8
ATT_KV_HEADS = 2
ATT_DIM = 64
ATT_GROUPS = ATT_Q_HEADS // ATT_KV_HEADS
ATT_WIDTH = ATT_Q_HEADS * ATT_DIM
ATT_KV_WIDTH = ATT_KV_HEADS * ATT_DIM
WINDOW = 128
GATE_WIDTH = 3 * D_MODEL
ROPE_THETA = 10000.0
EPS = 1e-6
NEG_INF = -1e30

LANES = 128
HALO_ROWS = 16
FF_CHUNK = 256
SEQ_CHUNKS = 4
VMEM_LIMIT = 56 * 1024 * 1024

OFF_RQ = 0
OFF_RK = OFF_RQ + RET_WIDTH
OFF_RV = OFF_RK + RET_WIDTH
OFF_RG = OFF_RV + RET_WIDTH
OFF_CB = OFF_RG + RET_WIDTH
OFF_CC = OFF_CB + CONV_WIDTH
OFF_CX = OFF_CC + CONV_WIDTH
OFF_AQ = OFF_CX + CONV_WIDTH
OFF_AK = OFF_AQ + ATT_WIDTH
OFF_AV = OFF_AK + ATT_KV_WIDTH
OFF_GATE = OFF_AV + ATT_KV_WIDTH
IN_WIDTH = OFF_GATE + GATE_WIDTH


def _params(*sem):
    return pltpu.CompilerParams(dimension_semantics=sem, vmem_limit_bytes=VMEM_LIMIT)


def _resident(shape):
    nd = len(shape)
    return pl.BlockSpec(shape, lambda *_: (0,) * nd, pipeline_mode=pl.Buffered(1))


def _row_tile(rows, cap):
    best = CHUNK
    t = CHUNK
    while t <= cap:
        if rows % t == 0:
            best = t
        t += CHUNK
    return best


def _rms(x, gain):
    ms = jnp.mean(x * x, axis=-1, keepdims=True)
    return x * lax.rsqrt(ms + EPS) * gain


def _dot_nt(a, b):
    return lax.dot_general(a, b, (((1,), (1,)), ((), ())), preferred_element_type=F32)


def _ffn_kernel(x_ref, g_ref, wg_ref, wu_ref, wd_ref, o_ref, acc_ref):
    x = x_ref[...]
    n = _rms(x, g_ref[...]).astype(BF16)
    for c in range(D_FF // FF_CHUNK):
        sl = slice(c * FF_CHUNK, (c + 1) * FF_CHUNK)
        g = jnp.dot(n, wg_ref[:, sl], preferred_element_type=F32)
        u = jnp.dot(n, wu_ref[:, sl], preferred_element_type=F32)
        a = (g * jax.nn.sigmoid(g) * u).astype(BF16)
        d = jnp.dot(a, wd_ref[sl, :], preferred_element_type=F32)
        if c == 0:
            acc_ref[...] = d
        else:
            acc_ref[...] += d
    o_ref[...] = x + 0.5 * acc_ref[...]


def _ffn(h, gain, wg, wu, wd):
    rows = h.shape[0]
    tm = _row_tile(rows, 768)
    return pl.pallas_call(
        _ffn_kernel,
        out_shape=jax.ShapeDtypeStruct((rows, D_MODEL), F32),
        grid=(rows // tm,),
        in_specs=[
            pl.BlockSpec((tm, D_MODEL), lambda i: (i, 0)),
            _resident((1, D_MODEL)),
            _resident((D_MODEL, D_FF)),
            _resident((D_MODEL, D_FF)),
            _resident((D_FF, D_MODEL)),
        ],
        out_specs=pl.BlockSpec((tm, D_MODEL), lambda i: (i, 0)),
        scratch_shapes=[pltpu.VMEM((tm, D_MODEL), F32)],
        compiler_params=_params("parallel"),
        name="ffn",
    )(h, gain, wg, wu, wd)


def _rot128(x, cos, sin):
    return x * cos + pltpu.roll(x, 64, axis=1) * sin


def _rot64(x, cos, sin, low_half):
    partner = jnp.where(low_half, pltpu.roll(x, 96, axis=1), pltpu.roll(x, 32, axis=1))
    return x * cos + partner * sin


def _inproj_kernel(x_ref, g_ref, w_ref, cr_ref, sr_ref, ca_ref, sa_ref,
                   rq_ref, rk_ref, rv_ref, rg_ref, cb_ref, u_ref, aq_ref, ak_ref, av_ref, gs_ref):
    n = _rms(x_ref[...], g_ref[...]).astype(BF16)

    def proj(off, width):
        return jnp.dot(n, w_ref[:, off:off + width], preferred_element_type=F32)

    cr, sr = cr_ref[...], sr_ref[...]
    ca, sa = ca_ref[...], sa_ref[...]
    low_half = (lax.broadcasted_iota(jnp.int32, (1, LANES), 1) % ATT_DIM) < (ATT_DIM // 2)
    for hd in range(RET_HEADS):
        sl = slice(hd * RET_DIM, (hd + 1) * RET_DIM)
        rq_ref[:, sl] = _rot128(proj(OFF_RQ + hd * RET_DIM, RET_DIM), cr, sr).astype(BF16)
        rk = _rot128(proj(OFF_RK + hd * RET_DIM, RET_DIM), cr, sr) * (RET_DIM ** -0.5)
        rk_ref[:, sl] = rk.astype(BF16)
    rv_ref[...] = proj(OFF_RV, RET_WIDTH).astype(BF16)
    rg_ref[...] = proj(OFF_RG, RET_WIDTH).astype(BF16)
    cb_ref[...] = proj(OFF_CB, CONV_WIDTH).astype(BF16)
    u_ref[...] = (proj(OFF_CC, CONV_WIDTH) * proj(OFF_CX, CONV_WIDTH)).astype(BF16)
    for c in range(ATT_WIDTH // LANES):
        sl = slice(c * LANES, (c + 1) * LANES)
        aq = _rot64(proj(OFF_AQ + c * LANES, LANES), ca, sa, low_half) * (ATT_DIM ** -0.5)
        aq_ref[:, sl] = aq.astype(BF16)
    ak_ref[...] = _rot64(proj(OFF_AK, ATT_KV_WIDTH), ca, sa, low_half).astype(BF16)
    av_ref[...] = proj(OFF_AV, ATT_KV_WIDTH).astype(BF16)
    for c in range(GATE_WIDTH // 512):
        sl = slice(c * 512, (c + 1) * 512)
        gs_ref[:, sl] = jax.nn.sigmoid(proj(OFF_GATE + c * 512, 512)).astype(BF16)


def _inproj(h, gain, w_in, tabs):
    rows = h.shape[0]
    tm = _row_tile(rows, 512)
    row = lambda w: pl.BlockSpec((tm, w), lambda i: (i, 0))
    widths = (RET_WIDTH, RET_WIDTH, RET_WIDTH, RET_WIDTH, CONV_WIDTH, CONV_WIDTH,
              ATT_WIDTH, ATT_KV_WIDTH, ATT_KV_WIDTH, GATE_WIDTH)
    return pl.pallas_call(
        _inproj_kernel,
        out_shape=[jax.ShapeDtypeStruct((rows, w), BF16) for w in widths],
        grid=(rows // tm,),
        in_specs=[row(D_MODEL), _resident((1, D_MODEL)), _resident((D_MODEL, IN_WIDTH)),
                  row(LANES), row(LANES), row(LANES), row(LANES)],
        out_specs=[row(w) for w in widths],
        compiler_params=_params("parallel"),
        name="inproj",
    )(h, gain, w_in, *tabs)


T_QF, T_QB, T_VF, T_VB, T_CF, T_CB, T_MASK = range(7)
N_TABS = 7


def _ret_tables(dec_ref, tab_ref):
    i = lax.broadcasted_iota(jnp.int32, (CHUNK, CHUNK), 0).astype(F32)
    j = lax.broadcasted_iota(jnp.int32, (CHUNK, CHUNK), 1).astype(F32)
    diff = i - j
    for hd in range(RET_HEADS):
        lgf = jnp.log1p(-jnp.exp(jnp.full((CHUNK, CHUNK), dec_ref[0, hd], F32)))
        lgb = jnp.log1p(-jnp.exp(jnp.full((CHUNK, CHUNK), dec_ref[1, hd], F32)))
        base = hd * N_TABS
        tab_ref[base + T_QF] = jnp.exp(lgf * (i + 1.0))
        tab_ref[base + T_QB] = jnp.exp(lgb * (CHUNK - i))
        tab_ref[base + T_VF] = jnp.exp(lgf * (CHUNK - 1.0 - i))
        tab_ref[base + T_VB] = jnp.exp(lgb * i)
        tab_ref[base + T_CF] = jnp.exp(lgf * CHUNK)
        tab_ref[base + T_CB] = jnp.exp(lgb * CHUNK)
        tab_ref[base + T_MASK] = jnp.where(diff >= 0, jnp.exp(lgf * jnp.maximum(diff, 0.0)),
                                           jnp.exp(lgb * jnp.maximum(-diff, 0.0)))


def _ret_kernel(dec_ref, q_ref, k_ref, v_ref, g_ref, gain_ref, y_ref,
                sf_ref, sb_ref, sbs_ref, tab_ref, *, n_chunks):
    ph = pl.program_id(1)
    c = pl.program_id(2)
    nblk = pl.num_programs(2)

    @pl.when((ph == 0) & (c == 0))
    def _():
        sf_ref[...] = jnp.zeros_like(sf_ref)
        sb_ref[...] = jnp.zeros_like(sb_ref)
        _ret_tables(dec_ref, tab_ref)

    def state_update(s_ref, hd, k, v, t_v, t_c):
        base = hd * N_TABS
        vd = (v.astype(F32) * tab_ref[base + t_v]).astype(BF16)
        kt = k.astype(F32).T.astype(BF16)
        s_ref[hd] = tab_ref[base + t_c] * s_ref[hd] + jnp.dot(kt, vd, preferred_element_type=F32)

    @pl.when(ph == 0)
    def _():
        blk = nblk - 1 - c
        for jj in reversed(range(SEQ_CHUNKS)):
            chunk = blk * SEQ_CHUNKS + jj
            valid = chunk < n_chunks
            rows = slice(jj * CHUNK, (jj + 1) * CHUNK)
            for hd in range(RET_HEADS):
                sl = slice(hd * RET_DIM, (hd + 1) * RET_DIM)
                sbs_ref[chunk * RET_HEADS + hd] = sb_ref[hd].astype(BF16)
                k = k_ref[0, rows, sl]
                v = v_ref[0, rows, sl]
                k = jnp.where(valid, k, jnp.zeros_like(k))
                v = jnp.where(valid, v, jnp.zeros_like(v))
                state_update(sb_ref, hd, k, v, T_VB, T_CB)

    @pl.when(ph == 1)
    def _():
        for jj in range(SEQ_CHUNKS):
            chunk = c * SEQ_CHUNKS + jj
            rows = slice(jj * CHUNK, (jj + 1) * CHUNK)
            for hd in range(RET_HEADS):
                sl = slice(hd * RET_DIM, (hd + 1) * RET_DIM)
                base = hd * N_TABS
                q = q_ref[0, rows, sl]
                k = k_ref[0, rows, sl]
                v = v_ref[0, rows, sl]
                att = _dot_nt(q, k) * tab_ref[base + T_MASK]
                o = jnp.dot(att.astype(BF16), v, preferred_element_type=F32)
                qf = q.astype(F32)
                qd = jnp.concatenate([(qf * tab_ref[base + T_QF]).astype(BF16),
                                      (qf * tab_ref[base + T_QB]).astype(BF16)], axis=1)
                sd = jnp.concatenate([sf_ref[hd].astype(BF16), sbs_ref[chunk * RET_HEADS + hd]], axis=0)
                o += jnp.dot(qd, sd, preferred_element_type=F32)
                state_update(sf_ref, hd, k, v, T_VF, T_CF)
                mu = jnp.mean(o, axis=-1, keepdims=True)
                oc = o - mu
                var = jnp.mean(oc * oc, axis=-1, keepdims=True)
                on = oc * lax.rsqrt(var + EPS) * gain_ref[:, sl]
                g = g_ref[0, rows, sl].astype(F32)
                y_ref[0, rows, sl] = (g * jax.nn.sigmoid(g) * on).astype(BF16)


def _retention(dec, rq, rk, rv, rg, gn_gain, batch):
    rows = rq.shape[0]
    p = rows // batch
    n = p // CHUNK
    nblk = pl.cdiv(n, SEQ_CHUNKS)
    tr = SEQ_CHUNKS * CHUNK
    v3 = lambda t: t.reshape(batch, p, RET_WIDTH)
    smem = pl.BlockSpec(memory_space=pltpu.SMEM)
    kv_spec = pl.BlockSpec((1, tr, RET_WIDTH), lambda b, ph, c: (b, ph * c + (1 - ph) * (nblk - 1 - c), 0))
    qg_spec = pl.BlockSpec((1, tr, RET_WIDTH), lambda b, ph, c: (b, ph * c, 0))
    tile = (RET_DIM, RET_DIM)
    y = pl.pallas_call(
        functools.partial(_ret_kernel, n_chunks=n),
        out_shape=jax.ShapeDtypeStruct((batch, p, RET_WIDTH), BF16),
        grid=(batch, 2, nblk),
        in_specs=[smem, qg_spec, kv_spec, kv_spec, qg_spec, _resident((1, RET_WIDTH))],
        out_specs=qg_spec,
        scratch_shapes=[pltpu.VMEM((RET_HEADS,) + tile, F32), pltpu.VMEM((RET_HEADS,) + tile, F32),
                        pltpu.VMEM((nblk * SEQ_CHUNKS * RET_HEADS,) + tile, BF16),
                        pltpu.VMEM((RET_HEADS * N_TABS, CHUNK, CHUNK), F32)],
        compiler_params=_params("parallel", "arbitrary", "arbitrary"),
        name="retention",
    )(dec, v3(rq), v3(rk), v3(rv), v3(rg), gn_gain)
    return y.reshape(rows, RET_WIDTH)


def _attn_kernel(sink_ref, q_ref, kp_ref, km_ref, kn_ref, vp_ref, vm_ref, vn_ref, y_ref, *, seq_rows):
    blk = pl.program_id(1)
    tq = SEQ_CHUNKS * CHUNK
    span = tq + 2 * CHUNK
    lane = lax.broadcasted_iota(jnp.int32, (1, LANES), 1)
    head0 = lane < ATT_DIM
    wpos = blk * tq - CHUNK + lax.broadcasted_iota(jnp.int32, (span, 1), 0)
    inside = (wpos >= 0) & (wpos < seq_rows)
    kall = jnp.concatenate([kp_ref[0], km_ref[0], kn_ref[0]], axis=0)
    vall = jnp.concatenate([vp_ref[0], vm_ref[0], vn_ref[0]], axis=0)
    zero = jnp.zeros_like(kall)
    kall = jnp.where(inside, kall, zero)
    vall = jnp.where(inside, vall, zero)
    k0, k1 = jnp.where(head0, kall, zero), jnp.where(head0, zero, kall)
    v0, v1 = jnp.where(head0, vall, zero), jnp.where(head0, zero, vall)
    r = lax.broadcasted_iota(jnp.int32, (CHUNK, 3 * CHUNK), 0)
    j = lax.broadcasted_iota(jnp.int32, (CHUNK, 3 * CHUNK), 1)
    band = jnp.abs(CHUNK + r - j) <= WINDOW
    for jj in range(SEQ_CHUNKS):
        base = jj * CHUNK
        kpos = (blk * SEQ_CHUNKS + jj - 1) * CHUNK + j
        ok = band & (kpos >= PAD) & (kpos < seq_rows)
        kb = jnp.concatenate([k0[base:base + 3 * CHUNK], k1[base:base + 3 * CHUNK]], axis=0)
        vb = jnp.concatenate([v0[base:base + 3 * CHUNK], v1[base:base + 3 * CHUNK]], axis=0)
        qs = jnp.concatenate([q_ref[0, base:base + CHUNK, c * LANES:(c + 1) * LANES]
                              for c in range(ATT_GROUPS)], axis=0)
        s_all = _dot_nt(qs, kb)
        e_rows, inv_rows = [], []
        for c in range(ATT_GROUPS):
            e_cols, inv_cols = [], []
            for kv in range(ATT_KV_HEADS):
                s = s_all[c * CHUNK:(c + 1) * CHUNK, kv * 3 * CHUNK:(kv + 1) * 3 * CHUNK]
                s = jnp.where(ok, s, NEG_INF)
                sink = sink_ref[kv * ATT_GROUPS + c]
                m = jnp.maximum(jnp.max(s, axis=-1, keepdims=True), sink)
                e = jnp.exp(s - m)
                denom = jnp.sum(e, axis=-1, keepdims=True) + jnp.exp(sink - m)
                e_cols.append(e.astype(BF16))
                inv_cols.append(1.0 / denom)
            e_rows.append(jnp.concatenate(e_cols, axis=1))
            inv_rows.append(jnp.where(head0, inv_cols[0], inv_cols[1]))
        o = jnp.dot(jnp.concatenate(e_rows, axis=0), vb, preferred_element_type=F32)
        for c in range(ATT_GROUPS):
            y_ref[0, base:base + CHUNK, c * LANES:(c + 1) * LANES] = (
                o[c * CHUNK:(c + 1) * CHUNK] * inv_rows[c]).astype(BF16)


def _attention(sink, aq, ak, av, batch):
    rows = aq.shape[0]
    p = rows // batch
    n = p // CHUNK
    tq = SEQ_CHUNKS * CHUNK
    smem = pl.BlockSpec(memory_space=pltpu.SMEM)
    qspec = pl.BlockSpec((1, tq, ATT_WIDTH), lambda b, i: (b, i, 0))
    main = pl.BlockSpec((1, tq, ATT_KV_WIDTH), lambda b, i: (b, i, 0))
    prev = pl.BlockSpec((1, CHUNK, ATT_KV_WIDTH), lambda b, i: (b, jnp.maximum(i * SEQ_CHUNKS - 1, 0), 0))
    nxt = pl.BlockSpec((1, CHUNK, ATT_KV_WIDTH),
                       lambda b, i: (b, jnp.minimum((i + 1) * SEQ_CHUNKS, n - 1), 0))
    k3 = ak.reshape(batch, p, ATT_KV_WIDTH)
    v3 = av.reshape(batch, p, ATT_KV_WIDTH)
    y = pl.pallas_call(
        functools.partial(_attn_kernel, seq_rows=p),
        out_shape=jax.ShapeDtypeStruct((batch, p, ATT_WIDTH), BF16),
        grid=(batch, pl.cdiv(n, SEQ_CHUNKS)),
        in_specs=[smem, qspec, prev, main, nxt, prev, main, nxt],
        out_specs=qspec,
        compiler_params=_params("parallel", "parallel"),
        name="attn",
    )(sink, aq.reshape(batch, p, ATT_WIDTH), k3, k3, k3, v3, v3, v3)
    return y.reshape(rows, ATT_WIDTH)


def _merge_kernel(h_ref, yr_ref, ya_ref, cb_ref, u_ref, up_ref, un_ref, gs_ref, cw_ref,
                  wr_ref, wc_ref, wa_ref, wo_ref, o_ref, us_ref, *, seq_rows):
    i = pl.program_id(0)
    tm = h_ref.shape[0]
    us_ref[pl.ds(8, tm), :] = u_ref[...].astype(F32)
    us_ref[pl.ds(0, 8), :] = up_ref[pl.ds(HALO_ROWS - 8, 8), :].astype(F32)
    nxt = un_ref[pl.ds(0, 8), :].astype(F32)
    us_ref[pl.ds(8 + tm, 8), :] = jnp.where(i == pl.num_programs(0) - 1, 0.0, nxt)
    conv = (us_ref[pl.ds(7, tm), :] * cw_ref[0:1, :] + us_ref[pl.ds(8, tm), :] * cw_ref[1:2, :]
            + us_ref[pl.ds(9, tm), :] * cw_ref[2:3, :])
    yc = (cb_ref[...].astype(F32) * conv).astype(BF16)

    def branch(y, w_ref, b):
        gate = gs_ref[:, b * D_MODEL:(b + 1) * D_MODEL].astype(F32)
        return gate * jnp.dot(y, w_ref[...], preferred_element_type=F32)

    merged = branch(yr_ref[...], wr_ref, 0) + branch(yc, wc_ref, 1) + branch(ya_ref[...], wa_ref, 2)
    mix = jnp.dot(merged.astype(BF16), wo_ref[...], preferred_element_type=F32)
    pos = (i * tm + lax.broadcasted_iota(jnp.int32, (tm, 1), 0)) % seq_rows
    o_ref[...] = h_ref[...] + jnp.where(pos < PAD, 0.0, mix)


def _merge(h, yr, ya, cb, u, gs, conv_w, wr, wc, wa, wo, seq_rows):
    rows = h.shape[0]
    tm = _row_tile(rows, 512)
    hb = tm // HALO_ROWS
    last = rows // HALO_ROWS - 1
    row = lambda w: pl.BlockSpec((tm, w), lambda i: (i, 0))
    halo_prev = pl.BlockSpec((HALO_ROWS, CONV_WIDTH), lambda i: (jnp.maximum(i * hb - 1, 0), 0))
    halo_next = pl.BlockSpec((HALO_ROWS, CONV_WIDTH), lambda i: (jnp.minimum((i + 1) * hb, last), 0))
    return pl.pallas_call(
        functools.partial(_merge_kernel, seq_rows=seq_rows),
        out_shape=jax.ShapeDtypeStruct((rows, D_MODEL), F32),
        grid=(rows // tm,),
        in_specs=[row(D_MODEL), row(RET_WIDTH), row(ATT_WIDTH), row(CONV_WIDTH), row(CONV_WIDTH),
                  halo_prev, halo_next, row(GATE_WIDTH), _resident((3, CONV_WIDTH)),
                  _resident((RET_WIDTH, D_MODEL)), _resident((CONV_WIDTH, D_MODEL)),
                  _resident((ATT_WIDTH, D_MODEL)), _resident((D_MODEL, D_MODEL))],
        out_specs=row(D_MODEL),
        scratch_shapes=[pltpu.VMEM((tm + 16, CONV_WIDTH), F32)],
        compiler_params=_params("parallel"),
        name="merge",
    )(h, yr, ya, cb, u, u, u, gs, conv_w, wr, wc, wa, wo)


def _final_kernel(x_ref, g_ref, o_ref):
    o_ref[0] = _rms(x_ref[0], g_ref[...])


def _final(h3, gain):
    batch, p, _ = h3.shape
    s = p - CHUNK
    return pl.pallas_call(
        _final_kernel,
        out_shape=jax.ShapeDtypeStruct((batch, s, D_MODEL), F32),
        grid=(batch, s // CHUNK),
        in_specs=[pl.BlockSpec((1, CHUNK, D_MODEL), lambda b, j: (b, j + 1, 0)), _resident((1, D_MODEL))],
        out_specs=pl.BlockSpec((1, CHUNK, D_MODEL), lambda b, j: (b, j, 0)),
        compiler_params=_params("parallel", "parallel"),
        name="final_norm",
    )(h3, gain)


def _rope_tables(p, batch):
    pos = jnp.arange(p, dtype=F32) - float(PAD)

    def tab(d):
        inv = ROPE_THETA ** (-jnp.arange(0, d, 2, dtype=F32) / d)
        ang = pos[:, None] * inv[None, :]
        cos, sin = jnp.cos(ang), jnp.sin(ang)
        reps = LANES // d
        c = jnp.tile(jnp.concatenate([cos, cos], axis=1), (batch, reps))
        s = jnp.tile(jnp.concatenate([-sin, sin], axis=1), (batch, reps))
        return c, s

    cr, sr = tab(RET_DIM)
    ca, sa = tab(ATT_DIM)
    return cr, sr, ca, sa


def _pair_heads(t, axis):
    shape = t.shape
    t = t.reshape(shape[:axis] + (ATT_KV_HEADS, ATT_GROUPS, ATT_DIM) + shape[axis + 1:])
    t = jnp.swapaxes(t, axis, axis + 1)
    return t.reshape(shape)


def _prep_weights(w_in, w_attn_out):
    aq = _pair_heads(w_in[:, :, OFF_AQ:OFF_AK], 2)
    w_in = jnp.concatenate([w_in[:, :, :OFF_AQ], aq, w_in[:, :, OFF_AK:]], axis=2)
    return w_in.astype(BF16), _pair_heads(w_attn_out, 1).astype(BF16)


def _trunk(x, meta_tokens, layers, final_norm):
    batch, s, _ = x.shape
    p = s + CHUNK
    rows = batch * p
    meta = jnp.broadcast_to(meta_tokens[None], (batch, N_META, D_MODEL))
    h = jnp.concatenate([jnp.zeros((batch, PAD, D_MODEL), F32), meta, x], axis=1).reshape(rows, D_MODEL)
    tabs = _rope_tables(p, batch)
    for ly in layers:
        h = _ffn(h, ly["norm_ffn1"], ly["wg1"], ly["wu1"], ly["wd1"])
        rq, rk, rv, rg, cb, u, aq, ak, av, gs = _inproj(h, ly["norm_mix"], ly["w_in"], tabs)
        yr = _retention(ly["ret_decay"], rq, rk, rv, rg, ly["ret_gn_gain"], batch)
        ya = _attention(ly["attn_sink"], aq, ak, av, batch)
        h = _merge(h, yr, ya, cb, u, gs, ly["conv_w"], ly["w_ret_out"], ly["w_conv_out"],
                   ly["w_attn_out"], ly["w_o"], p)
        h = _ffn(h, ly["norm_ffn2"], ly["wg2"], ly["wu2"], ly["wd2"])
    return _final(h.reshape(batch, p, D_MODEL), final_norm.reshape(1, D_MODEL))


def kernel(x_prompt, x_sample, meta_tokens, norm_ffn1, w_ffn1_gate, w_ffn1_up, w_ffn1_down, norm_mix, w_in, ret_decay, ret_gn_gain, conv_w, attn_sink, w_ret_out, w_conv_out, w_attn_out, w_o, norm_ffn2, w_ffn2_gate, w_ffn2_up, w_ffn2_down, final_norm):
    depth = w_in.shape[0]
    w_in_b, w_attn_out_b = _prep_weights(w_in, w_attn_out)
    layers = []
    for l in range(depth):
        layers.append(dict(
            norm_ffn1=norm_ffn1[l].reshape(1, D_MODEL), wg1=w_ffn1_gate[l].astype(BF16),
            wu1=w_ffn1_up[l].astype(BF16), wd1=w_ffn1_down[l].astype(BF16),
            norm_mix=norm_mix[l].reshape(1, D_MODEL), w_in=w_in_b[l],
            ret_decay=ret_decay[l], ret_gn_gain=ret_gn_gain[l].reshape(1, RET_WIDTH),
            conv_w=conv_w[l], attn_sink=attn_sink[l],
            w_ret_out=w_ret_out[l].astype(BF16), w_conv_out=w_conv_out[l].astype(BF16),
            w_attn_out=w_attn_out_b[l], w_o=w_o[l].astype(BF16),
            norm_ffn2=norm_ffn2[l].reshape(1, D_MODEL), wg2=w_ffn2_gate[l].astype(BF16),
            wu2=w_ffn2_up[l].astype(BF16), wd2=w_ffn2_down[l].astype(BF16)))
    y_prompt = _trunk(x_prompt, meta_tokens, layers, final_norm)
    y_sample = _trunk(x_sample, meta_tokens, layers, final_norm)
    return (y_prompt, y_sample)
```

```python
import functools
import math

import jax
import jax.numpy as jnp
from jax import lax
from jax.experimental import pallas as pl
from jax.experimental.pallas import tpu as pltpu

F32 = jnp.float32
BF16 = jnp.bfloat16

D_MODEL = 1024
D_FF = 2816
N_META = 16
CHUNK = 128
PAD = CHUNK - N_META
RET_HEADS = 4
RET_DIM = 128
RET_WIDTH = RET_HEADS * RET_DIM
CONV_WIDTH = 512
ATT_Q_HEADS = 8
ATT_KV_HEADS = 2
ATT_DIM = 64
ATT_GROUPS = ATT_Q_HEADS // ATT_KV_HEADS
ATT_WIDTH = ATT_Q_HEADS * ATT_DIM
ATT_KV_WIDTH = ATT_KV_HEADS * ATT_DIM
WINDOW = 128
GATE_WIDTH = 3 * D_MODEL
ROPE_THETA = 10000.0
EPS = 1e-6
NEG_INF = -1e30

LANES = 128
MXU_COLS = 256
HALO_ROWS = 16
FF_CHUNK = MXU_COLS
LOG2E = math.log2(math.e)
ATT_Q_SCALE = ATT_DIM ** -0.5 * LOG2E
SEQ_CHUNKS = 4
VMEM_LIMIT = 56 * 1024 * 1024

OFF_RQ = 0
OFF_RK = OFF_RQ + RET_WIDTH
OFF_RV = OFF_RK + RET_WIDTH
OFF_RG = OFF_RV + RET_WIDTH
OFF_CB = OFF_RG + RET_WIDTH
OFF_CC = OFF_CB + CONV_WIDTH
OFF_CX = OFF_CC + CONV_WIDTH
OFF_AQ = OFF_CX + CONV_WIDTH
OFF_AK = OFF_AQ + ATT_WIDTH
OFF_AV = OFF_AK + ATT_KV_WIDTH
OFF_GATE = OFF_AV + ATT_KV_WIDTH
IN_WIDTH = OFF_GATE + GATE_WIDTH


def _params(*sem):
    return pltpu.CompilerParams(dimension_semantics=sem, vmem_limit_bytes=VMEM_LIMIT)


def _resident(shape):
    nd = len(shape)
    return pl.BlockSpec(shape, lambda *_: (0,) * nd, pipeline_mode=pl.Buffered(1))


def _row_tile(rows, cap):
    best = CHUNK
    t = CHUNK
    while t <= cap:
        if rows % t == 0:
            best = t
        t += CHUNK
    return best


def _rms(x, gain):
    ms = jnp.mean(x * x, axis=-1, keepdims=True)
    return x * lax.rsqrt(ms + EPS) * gain


def _dot_nt(a, b):
    return lax.dot_general(a, b, (((1,), (1,)), ((), ())), preferred_element_type=F32)


def _ffn_body(x, g_ref, wg_ref, wu_ref, wd_ref, acc_ref):
    n = _rms(x, g_ref[...]).astype(BF16)
    for c in range(D_FF // FF_CHUNK):
        sl = slice(c * FF_CHUNK, (c + 1) * FF_CHUNK)
        g = jnp.dot(n, wg_ref[:, sl], preferred_element_type=F32)
        u = jnp.dot(n, wu_ref[:, sl], preferred_element_type=F32)
        a = (g * jax.nn.sigmoid(g) * u).astype(BF16)
        d = jnp.dot(a, wd_ref[sl, :], preferred_element_type=F32)
        if c == 0:
            acc_ref[...] = d
        else:
            acc_ref[...] += d
    return x + 0.5 * acc_ref[...]


def _ffn_kernel(x_ref, g_ref, wg_ref, wu_ref, wd_ref, o_ref, acc_ref):
    o_ref[...] = _ffn_body(x_ref[...], g_ref, wg_ref, wu_ref, wd_ref, acc_ref)


def _ffn_final_kernel(x_ref, g_ref, wg_ref, wu_ref, wd_ref, fg_ref, o_ref, acc_ref):
    o_ref[...] = _rms(_ffn_body(x_ref[0], g_ref, wg_ref, wu_ref, wd_ref, acc_ref), fg_ref[...])


def _ffn_weight_specs():
    return [_resident((1, D_MODEL)), _resident((D_MODEL, D_FF)), _resident((D_MODEL, D_FF)),
            _resident((D_FF, D_MODEL))]


def _ffn(h, gain, wg, wu, wd):
    rows = h.shape[0]
    tm = _row_tile(rows, 1024)
    return pl.pallas_call(
        _ffn_kernel,
        out_shape=jax.ShapeDtypeStruct((rows, D_MODEL), F32),
        grid=(rows // tm,),
        in_specs=[pl.BlockSpec((tm, D_MODEL), lambda i: (i, 0))] + _ffn_weight_specs(),
        out_specs=pl.BlockSpec((tm, D_MODEL), lambda i: (i, 0)),
        scratch_shapes=[pltpu.VMEM((tm, D_MODEL), F32)],
        compiler_params=_params("parallel"),
        name="ffn",
    )(h, gain, wg, wu, wd)


def _ffn_final(h3, gain, wg, wu, wd, final_gain):
    batch, p, _ = h3.shape
    s = p - CHUNK
    tm = _row_tile(s, 1024)
    return pl.pallas_call(
        _ffn_final_kernel,
        out_shape=jax.ShapeDtypeStruct((batch, s, D_MODEL), F32),
        grid=(batch, s // tm),
        in_specs=[pl.BlockSpec((pl.Element(1), pl.Element(tm), pl.Element(D_MODEL)),
                               lambda b, j: (b, pl.multiple_of(CHUNK + j * tm, CHUNK), 0))]
        + _ffn_weight_specs() + [_resident((1, D_MODEL))],
        out_specs=pl.BlockSpec((None, tm, D_MODEL), lambda b, j: (b, j, 0)),
        scratch_shapes=[pltpu.VMEM((tm, D_MODEL), F32)],
        compiler_params=_params("parallel", "parallel"),
        name="ffn_final",
    )(h3, gain, wg, wu, wd, final_gain)


def _rot128(x, cos, sin):
    return x * cos + pltpu.roll(x, 64, axis=1) * sin


def _rot64(x, cos, sin, low_half):
    partner = jnp.where(low_half, pltpu.roll(x, 96, axis=1), pltpu.roll(x, 32, axis=1))
    return x * cos + partner * sin


def _inproj_kernel(x_ref, g_ref, w_ref, cr_ref, sr_ref, ca_ref, sa_ref,
                   rq_ref, rk_ref, rv_ref, rg_ref, cb_ref, u_ref, aq_ref, ak_ref, av_ref, gs_ref):
    n = _rms(x_ref[...], g_ref[...]).astype(BF16)

    def proj(off, width):
        return jnp.dot(n, w_ref[:, off:off + width], preferred_element_type=F32)

    cr, sr = cr_ref[...], sr_ref[...]
    ca, sa = ca_ref[...], sa_ref[...]
    low_half = (lax.broadcasted_iota(jnp.int32, (1, LANES), 1) % ATT_DIM) < (ATT_DIM // 2)
    for c in range(RET_WIDTH // MXU_COLS):
        zq = proj(OFF_RQ + c * MXU_COLS, MXU_COLS)
        zk = proj(OFF_RK + c * MXU_COLS, MXU_COLS)
        for half in range(MXU_COLS // RET_DIM):
            src = slice(half * RET_DIM, (half + 1) * RET_DIM)
            dst = slice(c * MXU_COLS + half * RET_DIM, c * MXU_COLS + (half + 1) * RET_DIM)
            rq_ref[:, dst] = _rot128(zq[:, src], cr, sr).astype(BF16)
            rk_ref[:, dst] = (_rot128(zk[:, src], cr, sr) * (RET_DIM ** -0.5)).astype(BF16)
    rv_ref[...] = proj(OFF_RV, RET_WIDTH).astype(BF16)
    rg_ref[...] = proj(OFF_RG, RET_WIDTH).astype(BF16)
    cb_ref[...] = proj(OFF_CB, CONV_WIDTH).astype(BF16)
    u_ref[...] = (proj(OFF_CC, CONV_WIDTH) * proj(OFF_CX, CONV_WIDTH)).astype(BF16)
    for c in range(ATT_WIDTH // MXU_COLS):
        zq = proj(OFF_AQ + c * MXU_COLS, MXU_COLS)
        for half in range(MXU_COLS // LANES):
            src = slice(half * LANES, (half + 1) * LANES)
            dst = slice(c * MXU_COLS + half * LANES, c * MXU_COLS + (half + 1) * LANES)
            aq_ref[:, dst] = (_rot64(zq[:, src], ca, sa, low_half) * ATT_Q_SCALE).astype(BF16)
    zkv = proj(OFF_AK, 2 * ATT_KV_WIDTH)
    ak_ref[...] = _rot64(zkv[:, :ATT_KV_WIDTH], ca, sa, low_half).astype(BF16)
    av_ref[...] = zkv[:, ATT_KV_WIDTH:].astype(BF16)
    for c in range(GATE_WIDTH // 512):
        sl = slice(c * 512, (c + 1) * 512)
        gs_ref[:, sl] = jax.nn.sigmoid(proj(OFF_GATE + c * 512, 512)).astype(BF16)


def _inproj(h, gain, w_in, tabs):
    rows = h.shape[0]
    tm = _row_tile(rows, 512)
    row = lambda w: pl.BlockSpec((tm, w), lambda i: (i, 0))
    widths = (RET_WIDTH, RET_WIDTH, RET_WIDTH, RET_WIDTH, CONV_WIDTH, CONV_WIDTH,
              ATT_WIDTH, ATT_KV_WIDTH, ATT_KV_WIDTH, GATE_WIDTH)
    return pl.pallas_call(
        _inproj_kernel,
        out_shape=[jax.ShapeDtypeStruct((rows, w), BF16) for w in widths],
        grid=(rows // tm,),
        in_specs=[row(D_MODEL), _resident((1, D_MODEL)), _resident((D_MODEL, IN_WIDTH)),
                  row(LANES), row(LANES), row(LANES), row(LANES)],
        out_specs=[row(w) for w in widths],
        compiler_params=_params("parallel"),
        name="inproj",
    )(h, gain, w_in, *tabs)


T_QF, T_QB, T_VF, T_VB, T_CF, T_CB, T_MASK = range(7)
N_TABS = 7


def _ret_tables(dec_ref, tab_ref):
    i = lax.broadcasted_iota(jnp.int32, (CHUNK, CHUNK), 0).astype(F32)
    j = lax.broadcasted_iota(jnp.int32, (CHUNK, CHUNK), 1).astype(F32)
    diff = i - j
    for hd in range(RET_HEADS):
        lgf = jnp.log1p(-jnp.exp(jnp.full((CHUNK, CHUNK), dec_ref[0, hd], F32)))
        lgb = jnp.log1p(-jnp.exp(jnp.full((CHUNK, CHUNK), dec_ref[1, hd], F32)))
        base = hd * N_TABS
        tab_ref[base + T_QF] = jnp.exp(lgf * (i + 1.0))
        tab_ref[base + T_QB] = jnp.exp(lgb * (CHUNK - i))
        tab_ref[base + T_VF] = jnp.exp(lgf * (CHUNK - 1.0 - i))
        tab_ref[base + T_VB] = jnp.exp(lgb * i)
        tab_ref[base + T_CF] = jnp.exp(lgf * CHUNK)
        tab_ref[base + T_CB] = jnp.exp(lgb * CHUNK)
        tab_ref[base + T_MASK] = jnp.where(diff >= 0, jnp.exp(lgf * jnp.maximum(diff, 0.0)),
                                           jnp.exp(lgb * jnp.maximum(-diff, 0.0)))


def _ret_kernel(dec_ref, q_ref, k_ref, v_ref, g_ref, gain_ref, y_ref,
                sf_ref, sb_ref, sbs_ref, tab_ref, *, n_chunks):
    ph = pl.program_id(1)
    c = pl.program_id(2)
    nblk = pl.num_programs(2)

    @pl.when((ph == 0) & (c == 0))
    def _():
        sf_ref[...] = jnp.zeros_like(sf_ref)
        sb_ref[...] = jnp.zeros_like(sb_ref)
        _ret_tables(dec_ref, tab_ref)

    def state_update(s_ref, hd, k, v, t_v, t_c):
        base = hd * N_TABS
        vd = (v.astype(F32) * tab_ref[base + t_v]).astype(BF16)
        kt = k.astype(F32).T.astype(BF16)
        s_ref[hd] = tab_ref[base + t_c] * s_ref[hd] + jnp.dot(kt, vd, preferred_element_type=F32)

    @pl.when(ph == 0)
    def _():
        blk = nblk - 1 - c
        for jj in reversed(range(SEQ_CHUNKS)):
            chunk = blk * SEQ_CHUNKS + jj
            valid = chunk < n_chunks
            rows = slice(jj * CHUNK, (jj + 1) * CHUNK)
            for hd in range(RET_HEADS):
                sl = slice(hd * RET_DIM, (hd + 1) * RET_DIM)
                sbs_ref[chunk * RET_HEADS + hd] = sb_ref[hd].astype(BF16)
                k = k_ref[0, rows, sl]
                v = v_ref[0, rows, sl]
                k = jnp.where(valid, k, jnp.zeros_like(k))
                v = jnp.where(valid, v, jnp.zeros_like(v))
                state_update(sb_ref, hd, k, v, T_VB, T_CB)

    @pl.when(ph == 1)
    def _():
        for jj in range(SEQ_CHUNKS):
            chunk = c * SEQ_CHUNKS + jj
            rows = slice(jj * CHUNK, (jj + 1) * CHUNK)
            for hd in range(RET_HEADS):
                sl = slice(hd * RET_DIM, (hd + 1) * RET_DIM)
                base = hd * N_TABS
                q = q_ref[0, rows, sl]
                k = k_ref[0, rows, sl]
                v = v_ref[0, rows, sl]
                att = _dot_nt(q, k) * tab_ref[base + T_MASK]
                o = jnp.dot(att.astype(BF16), v, preferred_element_type=F32)
                qf = q.astype(F32)
                qd = jnp.concatenate([(qf * tab_ref[base + T_QF]).astype(BF16),
                                      (qf * tab_ref[base + T_QB]).astype(BF16)], axis=1)
                sd = jnp.concatenate([sf_ref[hd].astype(BF16), sbs_ref[chunk * RET_HEADS + hd]], axis=0)
                o += jnp.dot(qd, sd, preferred_element_type=F32)
                state_update(sf_ref, hd, k, v, T_VF, T_CF)
                mu = jnp.mean(o, axis=-1, keepdims=True)
                oc = o - mu
                var = jnp.mean(oc * oc, axis=-1, keepdims=True)
                on = oc * lax.rsqrt(var + EPS) * gain_ref[:, sl]
                g = g_ref[0, rows, sl].astype(F32)
                y_ref[0, rows, sl] = (g * jax.nn.sigmoid(g) * on).astype(BF16)


def _retention(dec, rq, rk, rv, rg, gn_gain, batch):
    rows = rq.shape[0]
    p = rows // batch
    n = p // CHUNK
    nblk = pl.cdiv(n, SEQ_CHUNKS)
    tr = SEQ_CHUNKS * CHUNK
    v3 = lambda t: t.reshape(batch, p, RET_WIDTH)
    smem = pl.BlockSpec(memory_space=pltpu.SMEM)
    kv_spec = pl.BlockSpec((1, tr, RET_WIDTH), lambda b, ph, c: (b, ph * c + (1 - ph) * (nblk - 1 - c), 0))
    qg_spec = pl.BlockSpec((1, tr, RET_WIDTH), lambda b, ph, c: (b, ph * c, 0))
    tile = (RET_DIM, RET_DIM)
    y = pl.pallas_call(
        functools.partial(_ret_kernel, n_chunks=n),
        out_shape=jax.ShapeDtypeStruct((batch, p, RET_WIDTH), BF16),
        grid=(batch, 2, nblk),
        in_specs=[smem, qg_spec, kv_spec, kv_spec, qg_spec, _resident((1, RET_WIDTH))],
        out_specs=qg_spec,
        scratch_shapes=[pltpu.VMEM((RET_HEADS,) + tile, F32), pltpu.VMEM((RET_HEADS,) + tile, F32),
                        pltpu.VMEM((nblk * SEQ_CHUNKS * RET_HEADS,) + tile, BF16),
                        pltpu.VMEM((RET_HEADS * N_TABS, CHUNK, CHUNK), F32)],
        compiler_params=_params("parallel", "arbitrary", "arbitrary"),
        name="retention",
    )(dec, v3(rq), v3(rk), v3(rv), v3(rg), gn_gain)
    return y.reshape(rows, RET_WIDTH)


def _attn_kernel(sink_ref, q_ref, kp_ref, km_ref, kn_ref, vp_ref, vm_ref, vn_ref, y_ref, *, seq_rows):
    blk = pl.program_id(1)
    tq = SEQ_CHUNKS * CHUNK
    span = tq + 2 * CHUNK
    lane = lax.broadcasted_iota(jnp.int32, (1, LANES), 1)
    head0 = lane < ATT_DIM
    wpos = blk * tq - CHUNK + lax.broadcasted_iota(jnp.int32, (span, 1), 0)
    inside = (wpos >= 0) & (wpos < seq_rows)
    kall = jnp.concatenate([kp_ref[0], km_ref[0], kn_ref[0]], axis=0)
    vall = jnp.concatenate([vp_ref[0], vm_ref[0], vn_ref[0]], axis=0)
    zero = jnp.zeros_like(kall)
    kall = jnp.where(inside, kall, zero)
    vall = jnp.where(inside, vall, zero)
    k0, k1 = jnp.where(head0, kall, zero), jnp.where(head0, zero, kall)
    v0, v1 = jnp.where(head0, vall, zero), jnp.where(head0, zero, vall)
    r = lax.broadcasted_iota(jnp.int32, (CHUNK, 3 * CHUNK), 0)
    j = lax.broadcasted_iota(jnp.int32, (CHUNK, 3 * CHUNK), 1)
    band = jnp.abs(CHUNK + r - j) <= WINDOW
    for jj in range(SEQ_CHUNKS):
        base = jj * CHUNK
        kpos = (blk * SEQ_CHUNKS + jj - 1) * CHUNK + j
        ok = band & (kpos >= PAD) & (kpos < seq_rows)
        kb = jnp.concatenate([k0[base:base + 3 * CHUNK], k1[base:base + 3 * CHUNK]], axis=0)
        vb = jnp.concatenate([v0[base:base + 3 * CHUNK], v1[base:base + 3 * CHUNK]], axis=0)
        qs = jnp.concatenate([q_ref[0, base:base + CHUNK, c * LANES:(c + 1) * LANES]
                              for c in range(ATT_GROUPS)], axis=0)
        s_all = _dot_nt(qs, kb)
        e_rows, inv_rows = [], []
        for c in range(ATT_GROUPS):
            e_cols, inv_cols = [], []
            for kv in range(ATT_KV_HEADS):
                s = s_all[c * CHUNK:(c + 1) * CHUNK, kv * 3 * CHUNK:(kv + 1) * 3 * CHUNK]
                s = jnp.where(ok, s, NEG_INF)
                sink = sink_ref[kv * ATT_GROUPS + c] * LOG2E
                m = jnp.maximum(jnp.max(s, axis=-1, keepdims=True), sink)
                e = jnp.exp2(s - m)
                denom = jnp.sum(e, axis=-1, keepdims=True) + jnp.exp2(sink - m)
                e_cols.append(e.astype(BF16))
                inv_cols.append(1.0 / denom)
            e_rows.append(jnp.concatenate(e_cols, axis=1))
            inv_rows.append(jnp.where(head0, inv_cols[0], inv_cols[1]))
        o = jnp.dot(jnp.concatenate(e_rows, axis=0), vb, preferred_element_type=F32)
        for c in range(ATT_GROUPS):
            y_ref[0, base:base + CHUNK, c * LANES:(c + 1) * LANES] = (
                o[c * CHUNK:(c + 1) * CHUNK] * inv_rows[c]).astype(BF16)


def _attention(sink, aq, ak, av, batch):
    rows = aq.shape[0]
    p = rows // batch
    n = p // CHUNK
    tq = SEQ_CHUNKS * CHUNK
    smem = pl.BlockSpec(memory_space=pltpu.SMEM)
    qspec = pl.BlockSpec((1, tq, ATT_WIDTH), lambda b, i: (b, i, 0))
    main = pl.BlockSpec((1, tq, ATT_KV_WIDTH), lambda b, i: (b, i, 0))
    prev = pl.BlockSpec((1, CHUNK, ATT_KV_WIDTH), lambda b, i: (b, jnp.maximum(i * SEQ_CHUNKS - 1, 0), 0))
    nxt = pl.BlockSpec((1, CHUNK, ATT_KV_WIDTH),
                       lambda b, i: (b, jnp.minimum((i + 1) * SEQ_CHUNKS, n - 1), 0))
    k3 = ak.reshape(batch, p, ATT_KV_WIDTH)
    v3 = av.reshape(batch, p, ATT_KV_WIDTH)
    y = pl.pallas_call(
        functools.partial(_attn_kernel, seq_rows=p),
        out_shape=jax.ShapeDtypeStruct((batch, p, ATT_WIDTH), BF16),
        grid=(batch, pl.cdiv(n, SEQ_CHUNKS)),
        in_specs=[smem, qspec, prev, main, nxt, prev, main, nxt],
        out_specs=qspec,
        compiler_params=_params("parallel", "parallel"),
        name="attn",
    )(sink, aq.reshape(batch, p, ATT_WIDTH), k3, k3, k3, v3, v3, v3)
    return y.reshape(rows, ATT_WIDTH)


def _merge_kernel(h_ref, yr_ref, ya_ref, cb_ref, u_ref, up_ref, un_ref, gs_ref, cw_ref,
                  wr_ref, wc_ref, wa_ref, wo_ref, o_ref, us_ref, *, seq_rows):
    i = pl.program_id(0)
    tm = h_ref.shape[0]
    us_ref[pl.ds(8, tm), :] = u_ref[...].astype(F32)
    us_ref[pl.ds(0, 8), :] = up_ref[pl.ds(HALO_ROWS - 8, 8), :].astype(F32)
    nxt = un_ref[pl.ds(0, 8), :].astype(F32)
    us_ref[pl.ds(8 + tm, 8), :] = jnp.where(i == pl.num_programs(0) - 1, 0.0, nxt)
    conv = (us_ref[pl.ds(7, tm), :] * cw_ref[0:1, :] + us_ref[pl.ds(8, tm), :] * cw_ref[1:2, :]
            + us_ref[pl.ds(9, tm), :] * cw_ref[2:3, :])
    yc = (cb_ref[...].astype(F32) * conv).astype(BF16)

    def branch(y, w_ref, b):
        gate = gs_ref[:, b * D_MODEL:(b + 1) * D_MODEL].astype(F32)
        return gate * jnp.dot(y, w_ref[...], preferred_element_type=F32)

    merged = branch(yr_ref[...], wr_ref, 0) + branch(yc, wc_ref, 1) + branch(ya_ref[...], wa_ref, 2)
    mix = jnp.dot(merged.astype(BF16), wo_ref[...], preferred_element_type=F32)
    pos = (i * tm + lax.broadcasted_iota(jnp.int32, (tm, 1), 0)) % seq_rows
    o_ref[...] = h_ref[...] + jnp.where(pos < PAD, 0.0, mix)


def _merge(h, yr, ya, cb, u, gs, conv_w, wr, wc, wa, wo, seq_rows):
    rows = h.shape[0]
    tm = _row_tile(rows, 512)
    hb = tm // HALO_ROWS
    last = rows // HALO_ROWS - 1
    row = lambda w: pl.BlockSpec((tm, w), lambda i: (i, 0))
    halo_prev = pl.BlockSpec((HALO_ROWS, CONV_WIDTH), lambda i: (jnp.maximum(i * hb - 1, 0), 0))
    halo_next = pl.BlockSpec((HALO_ROWS, CONV_WIDTH), lambda i: (jnp.minimum((i + 1) * hb, last), 0))
    return pl.pallas_call(
        functools.partial(_merge_kernel, seq_rows=seq_rows),
        out_shape=jax.ShapeDtypeStruct((rows, D_MODEL), F32),
        grid=(rows // tm,),
        in_specs=[row(D_MODEL), row(RET_WIDTH), row(ATT_WIDTH), row(CONV_WIDTH), row(CONV_WIDTH),
                  halo_prev, halo_next, row(GATE_WIDTH), _resident((3, CONV_WIDTH)),
                  _resident((RET_WIDTH, D_MODEL)), _resident((CONV_WIDTH, D_MODEL)),
                  _resident((ATT_WIDTH, D_MODEL)), _resident((D_MODEL, D_MODEL))],
        out_specs=row(D_MODEL),
        scratch_shapes=[pltpu.VMEM((tm + 16, CONV_WIDTH), F32)],
        compiler_params=_params("parallel"),
        name="merge",
    )(h, yr, ya, cb, u, u, u, gs, conv_w, wr, wc, wa, wo)


def _rope_tables(p, batch):
    pos = jnp.arange(p, dtype=F32) - float(PAD)

    def tab(d):
        inv = ROPE_THETA ** (-jnp.arange(0, d, 2, dtype=F32) / d)
        ang = pos[:, None] * inv[None, :]
        cos, sin = jnp.cos(ang), jnp.sin(ang)
        reps = LANES // d
        c = jnp.tile(jnp.concatenate([cos, cos], axis=1), (batch, reps))
        s = jnp.tile(jnp.concatenate([-sin, sin], axis=1), (batch, reps))
        return c, s

    cr, sr = tab(RET_DIM)
    ca, sa = tab(ATT_DIM)
    return cr, sr, ca, sa


def _pair_heads(t, axis):
    shape = t.shape
    t = t.reshape(shape[:axis] + (ATT_KV_HEADS, ATT_GROUPS, ATT_DIM) + shape[axis + 1:])
    t = jnp.swapaxes(t, axis, axis + 1)
    return t.reshape(shape)


def _prep_weights(w_in, w_attn_out):
    aq = _pair_heads(w_in[:, :, OFF_AQ:OFF_AK].astype(BF16), 2)
    w_in = lax.dynamic_update_slice(w_in.astype(BF16), aq, (0, 0, OFF_AQ))
    return w_in, _pair_heads(w_attn_out, 1).astype(BF16)


def _trunk(x, meta_tokens, layers, final_norm):
    batch, s, _ = x.shape
    p = s + CHUNK
    rows = batch * p
    meta = jnp.broadcast_to(meta_tokens[None], (batch, N_META, D_MODEL))
    h = jnp.concatenate([jnp.zeros((batch, PAD, D_MODEL), F32), meta, x], axis=1).reshape(rows, D_MODEL)
    tabs = _rope_tables(p, batch)
    for idx, ly in enumerate(layers):
        h = _ffn(h, ly["norm_ffn1"], ly["wg1"], ly["wu1"], ly["wd1"])
        rq, rk, rv, rg, cb, u, aq, ak, av, gs = _inproj(h, ly["norm_mix"], ly["w_in"], tabs)
        yr = _retention(ly["ret_decay"], rq, rk, rv, rg, ly["ret_gn_gain"], batch)
        ya = _attention(ly["attn_sink"], aq, ak, av, batch)
        h = _merge(h, yr, ya, cb, u, gs, ly["conv_w"], ly["w_ret_out"], ly["w_conv_out"],
                   ly["w_attn_out"], ly["w_o"], p)
        if idx + 1 < len(layers):
            h = _ffn(h, ly["norm_ffn2"], ly["wg2"], ly["wu2"], ly["wd2"])
    ly = layers[-1]
    return _ffn_final(h.reshape(batch, p, D_MODEL), ly["norm_ffn2"], ly["wg2"], ly["wu2"], ly["wd2"],
                      final_norm.reshape(1, D_MODEL))


def kernel(x_prompt, x_sample, meta_tokens, norm_ffn1, w_ffn1_gate, w_ffn1_up, w_ffn1_down, norm_mix, w_in, ret_decay, ret_gn_gain, conv_w, attn_sink, w_ret_out, w_conv_out, w_attn_out, w_o, norm_ffn2, w_ffn2_gate, w_ffn2_up, w_ffn2_down, final_norm):
    depth = w_in.shape[0]
    w_in_b, w_attn_out_b = _prep_weights(w_in, w_attn_out)
    layers = []
    for l in range(depth):
        layers.append(dict(
            norm_ffn1=norm_ffn1[l].reshape(1, D_MODEL), wg1=w_ffn1_gate[l].astype(BF16),
            wu1=w_ffn1_up[l].astype(BF16), wd1=w_ffn1_down[l].astype(BF16),
            norm_mix=norm_mix[l].reshape(1, D_MODEL), w_in=w_in_b[l],
            ret_decay=ret_decay[l], ret_gn_gain=ret_gn_gain[l].reshape(1, RET_WIDTH),
            conv_w=conv_w[l], attn_sink=attn_sink[l],
            w_ret_out=w_ret_out[l].astype(BF16), w_conv_out=w_conv_out[l].astype(BF16),
            w_attn_out=w_attn_out_b[l], w_o=w_o[l].astype(BF16),
            norm_ffn2=norm_ffn2[l].reshape(1, D_MODEL), wg2=w_ffn2_gate[l].astype(BF16),
            wu2=w_ffn2_up[l].astype(BF16), wd2=w_ffn2_down[l].astype(BF16)))
    y_prompt = _trunk(x_prompt, meta_tokens, layers, final_norm)
    y_sample = _trunk(x_sample, meta_tokens, layers, final_norm)
    return (y_prompt, y_sample)
```

```python
import functools
import math

import jax
import jax.numpy as jnp
from jax import lax
from jax.experimental import pallas as pl
from jax.experimental.pallas import tpu as pltpu

F32 = jnp.float32
BF16 = jnp.bfloat16

D_MODEL = 1024
D_FF = 2816
N_META = 16
CHUNK = 128
PAD = CHUNK - N_META
RET_HEADS = 4
RET_DIM = 128
RET_WIDTH = RET_HEADS * RET_DIM
CONV_WIDTH = 512
ATT_Q_HEADS = 8
ATT_KV_HEADS = 2
ATT_DIM = 64
ATT_GROUPS = ATT_Q_HEADS // ATT_KV_HEADS
ATT_WIDTH = ATT_Q_HEADS * ATT_DIM
ATT_KV_WIDTH = ATT_KV_HEADS * ATT_DIM
WINDOW = 128
GATE_WIDTH = 3 * D_MODEL
ROPE_THETA = 10000.0
EPS = 1e-6
NEG_INF = -1e30

LANES = 128
MXU_COLS = 256
HALO_ROWS = 16
FF_CHUNK = MXU_COLS
SEQ_CHUNKS = 4
MERGE_SPLIT = 2
VMEM_LIMIT = 56 * 1024 * 1024
LOG2E = math.log2(math.e)
ATT_Q_SCALE = ATT_DIM ** -0.5 * LOG2E

OFF_RQ = 0
OFF_RK = OFF_RQ + RET_WIDTH
OFF_RV = OFF_RK + RET_WIDTH
OFF_RG = OFF_RV + RET_WIDTH
OFF_CB = OFF_RG + RET_WIDTH
OFF_CC = OFF_CB + CONV_WIDTH
OFF_CX = OFF_CC + CONV_WIDTH
OFF_AQ = OFF_CX + CONV_WIDTH
OFF_AK = OFF_AQ + ATT_WIDTH
OFF_AV = OFF_AK + ATT_KV_WIDTH
OFF_GATE = OFF_AV + ATT_KV_WIDTH
IN_WIDTH = OFF_GATE + GATE_WIDTH


def _params(*sem):
    return pltpu.CompilerParams(dimension_semantics=sem, vmem_limit_bytes=VMEM_LIMIT)


def _resident(shape):
    nd = len(shape)
    return pl.BlockSpec(shape, lambda *_: (0,) * nd, pipeline_mode=pl.Buffered(1))


def _row_tile(rows, cap):
    best = CHUNK
    t = CHUNK
    while t <= cap:
        if rows % t == 0:
            best = t
        t += CHUNK
    return best


def _rms(x, gain):
    ms = jnp.mean(x * x, axis=-1, keepdims=True)
    return x * lax.rsqrt(ms + EPS) * gain


def _dot(a, b):
    return jnp.dot(a, b, preferred_element_type=F32)


def _dot_nt(a, b):
    return lax.dot_general(a, b, (((1,), (1,)), ((), ())), preferred_element_type=F32)


def _ffn_body(x, g_ref, wg_ref, wu_ref, wd_ref, acc_ref):
    n = _rms(x, g_ref[...]).astype(BF16)
    for c in range(D_FF // FF_CHUNK):
        sl = slice(c * FF_CHUNK, (c + 1) * FF_CHUNK)
        g = _dot(n, wg_ref[:, sl])
        u = _dot(n, wu_ref[:, sl])
        a = (g * jax.nn.sigmoid(g) * u).astype(BF16)
        d = _dot(a, wd_ref[sl, :])
        if c == 0:
            acc_ref[...] = d
        else:
            acc_ref[...] += d
    return x + 0.5 * acc_ref[...]


def _ffn_kernel(x_ref, g_ref, wg_ref, wu_ref, wd_ref, o_ref, acc_ref):
    o_ref[...] = _ffn_body(x_ref[...], g_ref, wg_ref, wu_ref, wd_ref, acc_ref)


def _ffn_final_kernel(x_ref, g_ref, wg_ref, wu_ref, wd_ref, fg_ref, o_ref, acc_ref):
    o_ref[...] = _rms(_ffn_body(x_ref[0], g_ref, wg_ref, wu_ref, wd_ref, acc_ref), fg_ref[...])


def _ffn_weight_specs():
    return [_resident((1, D_MODEL)), _resident((D_MODEL, D_FF)), _resident((D_MODEL, D_FF)),
            _resident((D_FF, D_MODEL))]


def _ffn(h, gain, wg, wu, wd):
    rows = h.shape[0]
    tm = _row_tile(rows, 1024)
    return pl.pallas_call(
        _ffn_kernel,
        out_shape=jax.ShapeDtypeStruct((rows, D_MODEL), F32),
        grid=(rows // tm,),
        in_specs=[pl.BlockSpec((tm, D_MODEL), lambda i: (i, 0))] + _ffn_weight_specs(),
        out_specs=pl.BlockSpec((tm, D_MODEL), lambda i: (i, 0)),
        scratch_shapes=[pltpu.VMEM((tm, D_MODEL), F32)],
        compiler_params=_params("parallel"),
        name="ffn",
    )(h, gain, wg, wu, wd)


def _ffn_final(h3, gain, wg, wu, wd, final_gain):
    batch, p, _ = h3.shape
    s = p - CHUNK
    tm = _row_tile(s, 1024)
    return pl.pallas_call(
        _ffn_final_kernel,
        out_shape=jax.ShapeDtypeStruct((batch, s, D_MODEL), F32),
        grid=(batch, s // tm),
        in_specs=[pl.BlockSpec((pl.Element(1), pl.Element(tm), pl.Element(D_MODEL)),
                               lambda b, j: (b, pl.multiple_of(CHUNK + j * tm, CHUNK), 0))]
        + _ffn_weight_specs() + [_resident((1, D_MODEL))],
        out_specs=pl.BlockSpec((None, tm, D_MODEL), lambda b, j: (b, j, 0)),
        scratch_shapes=[pltpu.VMEM((tm, D_MODEL), F32)],
        compiler_params=_params("parallel", "parallel"),
        name="ffn_final",
    )(h3, gain, wg, wu, wd, final_gain)


def _rot128(x, cos, sin):
    return x * cos + pltpu.roll(x, 64, axis=1) * sin


def _rot64(x, cos, sin, low_half):
    partner = jnp.where(low_half, pltpu.roll(x, 96, axis=1), pltpu.roll(x, 32, axis=1))
    return x * cos + partner * sin


def _inproj_kernel(x_ref, g_ref, w_ref, cr_ref, sr_ref, ca_ref, sa_ref,
                   rq_ref, rk_ref, rv_ref, rg_ref, cb_ref, u_ref, aq_ref, ak_ref, av_ref, gs_ref):
    n = _rms(x_ref[...], g_ref[...]).astype(BF16)

    def proj(off, width):
        return _dot(n, w_ref[:, off:off + width])

    cr, sr = cr_ref[...], sr_ref[...]
    ca, sa = ca_ref[...], sa_ref[...]
    low_half = (lax.broadcasted_iota(jnp.int32, (1, LANES), 1) % ATT_DIM) < (ATT_DIM // 2)
    for c in range(RET_WIDTH // MXU_COLS):
        zq = proj(OFF_RQ + c * MXU_COLS, MXU_COLS)
        zk = proj(OFF_RK + c * MXU_COLS, MXU_COLS)
        for half in range(MXU_COLS // RET_DIM):
            src = slice(half * RET_DIM, (half + 1) * RET_DIM)
            dst = slice(c * MXU_COLS + half * RET_DIM, c * MXU_COLS + (half + 1) * RET_DIM)
            rq_ref[:, dst] = _rot128(zq[:, src], cr, sr).astype(BF16)
            rk_ref[:, dst] = (_rot128(zk[:, src], cr, sr) * (RET_DIM ** -0.5)).astype(BF16)
    rv_ref[...] = proj(OFF_RV, RET_WIDTH).astype(BF16)
    zg = proj(OFF_RG, RET_WIDTH)
    rg_ref[...] = (zg * jax.nn.sigmoid(zg)).astype(BF16)
    cb_ref[...] = proj(OFF_CB, CONV_WIDTH).astype(BF16)
    u_ref[...] = (proj(OFF_CC, CONV_WIDTH) * proj(OFF_CX, CONV_WIDTH)).astype(BF16)
    for c in range(ATT_WIDTH // MXU_COLS):
        zq = proj(OFF_AQ + c * MXU_COLS, MXU_COLS)
        for half in range(MXU_COLS // LANES):
            src = slice(half * LANES, (half + 1) * LANES)
            dst = slice(c * MXU_COLS + half * LANES, c * MXU_COLS + (half + 1) * LANES)
            aq_ref[:, dst] = (_rot64(zq[:, src], ca, sa, low_half) * ATT_Q_SCALE).astype(BF16)
    zkv = proj(OFF_AK, 2 * ATT_KV_WIDTH)
    ak_ref[...] = _rot64(zkv[:, :ATT_KV_WIDTH], ca, sa, low_half).astype(BF16)
    av_ref[...] = zkv[:, ATT_KV_WIDTH:].astype(BF16)
    for c in range(GATE_WIDTH // 512):
        sl = slice(c * 512, (c + 1) * 512)
        gs_ref[:, sl] = jax.nn.sigmoid(proj(OFF_GATE + c * 512, 512)).astype(BF16)


def _inproj(h, gain, w_in, tabs):
    rows = h.shape[0]
    tm = _row_tile(rows, 512)
    row = lambda w: pl.BlockSpec((tm, w), lambda i: (i, 0))
    widths = (RET_WIDTH, RET_WIDTH, RET_WIDTH, RET_WIDTH, CONV_WIDTH, CONV_WIDTH,
              ATT_WIDTH, ATT_KV_WIDTH, ATT_KV_WIDTH, GATE_WIDTH)
    return pl.pallas_call(
        _inproj_kernel,
        out_shape=[jax.ShapeDtypeStruct((rows, w), BF16) for w in widths],
        grid=(rows // tm,),
        in_specs=[row(D_MODEL), _resident((1, D_MODEL)), _resident((D_MODEL, IN_WIDTH)),
                  row(LANES), row(LANES), row(LANES), row(LANES)],
        out_specs=[row(w) for w in widths],
        compiler_params=_params("parallel"),
        name="inproj",
    )(h, gain, w_in, *tabs)


T_QF, T_QB, T_VF, T_VB, T_CF, T_CB, T_MASK = range(7)
N_TABS = 7


def _ret_tables(dec_ref, tab_ref):
    i = lax.broadcasted_iota(jnp.int32, (CHUNK, CHUNK), 0).astype(F32)
    j = lax.broadcasted_iota(jnp.int32, (CHUNK, CHUNK), 1).astype(F32)
    diff = i - j
    for hd in range(RET_HEADS):
        lgf = jnp.log1p(-jnp.exp(jnp.full((CHUNK, CHUNK), dec_ref[0, hd], F32)))
        lgb = jnp.log1p(-jnp.exp(jnp.full((CHUNK, CHUNK), dec_ref[1, hd], F32)))
        base = hd * N_TABS
        tab_ref[base + T_QF] = jnp.exp(lgf * (i + 1.0))
        tab_ref[base + T_QB] = jnp.exp(lgb * (CHUNK - i))
        tab_ref[base + T_VF] = jnp.exp(lgf * (CHUNK - 1.0 - i))
        tab_ref[base + T_VB] = jnp.exp(lgb * i)
        tab_ref[base + T_CF] = jnp.exp(lgf * CHUNK)
        tab_ref[base + T_CB] = jnp.exp(lgb * CHUNK)
        tab_ref[base + T_MASK] = jnp.where(diff >= 0, jnp.exp(lgf * jnp.maximum(diff, 0.0)),
                                           jnp.exp(lgb * jnp.maximum(-diff, 0.0)))


def _state_update(s_ref, tab_ref, hd, k, v, t_v, t_c):
    base = hd * N_TABS
    vd = (v.astype(F32) * tab_ref[base + t_v]).astype(BF16)
    kt = k.astype(F32).T.astype(BF16)
    s_ref[hd] = tab_ref[base + t_c] * s_ref[hd] + _dot(kt, vd)


def _ret_rev_kernel(dec_ref, k_ref, v_ref, sbs_ref, sb_ref, tab_ref, *, n_chunks):
    c = pl.program_id(1)
    blk = pl.num_programs(1) - 1 - c

    @pl.when(c == 0)
    def _():
        sb_ref[...] = jnp.zeros_like(sb_ref)
        _ret_tables(dec_ref, tab_ref)

    for jj in reversed(range(SEQ_CHUNKS)):
        valid = blk * SEQ_CHUNKS + jj < n_chunks
        rows = slice(jj * CHUNK, (jj + 1) * CHUNK)
        for hd in range(RET_HEADS):
            sl = slice(hd * RET_DIM, (hd + 1) * RET_DIM)
            sbs_ref[0, 0, jj * RET_HEADS + hd] = sb_ref[hd].astype(BF16)
            k = k_ref[0, rows, sl]
            v = v_ref[0, rows, sl]
            k = jnp.where(valid, k, jnp.zeros_like(k))
            v = jnp.where(valid, v, jnp.zeros_like(v))
            _state_update(sb_ref, tab_ref, hd, k, v, T_VB, T_CB)


def _ret_rev(dec, rk3, rv3):
    batch, p, _ = rk3.shape
    n = p // CHUNK
    nblk = pl.cdiv(n, SEQ_CHUNKS)
    rev = pl.BlockSpec((1, SEQ_CHUNKS * CHUNK, RET_WIDTH), lambda b, c: (b, nblk - 1 - c, 0))
    tiles = SEQ_CHUNKS * RET_HEADS
    return pl.pallas_call(
        functools.partial(_ret_rev_kernel, n_chunks=n),
        out_shape=jax.ShapeDtypeStruct((batch, nblk, tiles, RET_DIM, RET_DIM), BF16),
        grid=(batch, nblk),
        in_specs=[pl.BlockSpec(memory_space=pltpu.SMEM), rev, rev],
        out_specs=pl.BlockSpec((1, 1, tiles, RET_DIM, RET_DIM), lambda b, c: (b, nblk - 1 - c, 0, 0, 0)),
        scratch_shapes=[pltpu.VMEM((RET_HEADS, RET_DIM, RET_DIM), F32),
                        pltpu.VMEM((RET_HEADS * N_TABS, CHUNK, CHUNK), F32)],
        compiler_params=_params("parallel", "arbitrary"),
        name="ret_rev",
    )(dec, rk3, rv3)


def _ret_chunk(q_ref, k_ref, v_ref, g_ref, sbs_ref, gain_ref, sf_ref, tab_ref, jj):
    rows = slice(jj * CHUNK, (jj + 1) * CHUNK)
    outs = []
    for hd in range(RET_HEADS):
        sl = slice(hd * RET_DIM, (hd + 1) * RET_DIM)
        base = hd * N_TABS
        q = q_ref[0, rows, sl]
        k = k_ref[0, rows, sl]
        v = v_ref[0, rows, sl]
        att = _dot_nt(q, k) * tab_ref[base + T_MASK]
        o = _dot(att.astype(BF16), v)
        qf = q.astype(F32)
        qd = jnp.concatenate([(qf * tab_ref[base + T_QF]).astype(BF16),
                              (qf * tab_ref[base + T_QB]).astype(BF16)], axis=1)
        sd = jnp.concatenate([sf_ref[hd].astype(BF16), sbs_ref[0, 0, jj * RET_HEADS + hd]], axis=0)
        o += _dot(qd, sd)
        _state_update(sf_ref, tab_ref, hd, k, v, T_VF, T_CF)
        mu = jnp.mean(o, axis=-1, keepdims=True)
        oc = o - mu
        var = jnp.mean(oc * oc, axis=-1, keepdims=True)
        on = oc * lax.rsqrt(var + EPS) * gain_ref[:, sl]
        outs.append((g_ref[0, rows, sl].astype(F32) * on).astype(BF16))
    return jnp.concatenate(outs, axis=1)


def _attn_window(kp_ref, km_ref, kn_ref, vp_ref, vm_ref, vn_ref, blk, seq_rows):
    tq = SEQ_CHUNKS * CHUNK
    span = tq + 2 * CHUNK
    head0 = lax.broadcasted_iota(jnp.int32, (1, LANES), 1) < ATT_DIM
    wpos = blk * tq - CHUNK + lax.broadcasted_iota(jnp.int32, (span, 1), 0)
    inside = (wpos >= 0) & (wpos < seq_rows)
    kall = jnp.concatenate([kp_ref[0], km_ref[0], kn_ref[0]], axis=0)
    vall = jnp.concatenate([vp_ref[0], vm_ref[0], vn_ref[0]], axis=0)
    zero = jnp.zeros_like(kall)
    kall = jnp.where(inside, kall, zero)
    vall = jnp.where(inside, vall, zero)
    return (jnp.where(head0, kall, zero), jnp.where(head0, zero, kall),
            jnp.where(head0, vall, zero), jnp.where(head0, zero, vall))


def _attn_chunk(sink_ref, q_ref, window, blk, jj, seq_rows):
    k0, k1, v0, v1 = window
    base = jj * CHUNK
    head0 = lax.broadcasted_iota(jnp.int32, (1, LANES), 1) < ATT_DIM
    r = lax.broadcasted_iota(jnp.int32, (CHUNK, 3 * CHUNK), 0)
    j = lax.broadcasted_iota(jnp.int32, (CHUNK, 3 * CHUNK), 1)
    kpos = (blk * SEQ_CHUNKS + jj - 1) * CHUNK + j
    ok = (jnp.abs(CHUNK + r - j) <= WINDOW) & (kpos >= PAD) & (kpos < seq_rows)
    kb = jnp.concatenate([k0[base:base + 3 * CHUNK], k1[base:base + 3 * CHUNK]], axis=0)
    vb = jnp.concatenate([v0[base:base + 3 * CHUNK], v1[base:base + 3 * CHUNK]], axis=0)
    qs = jnp.concatenate([q_ref[0, base:base + CHUNK, c * LANES:(c + 1) * LANES]
                          for c in range(ATT_GROUPS)], axis=0)
    s_all = _dot_nt(qs, kb)
    e_rows, inv_rows = [], []
    for c in range(ATT_GROUPS):
        e_cols, inv_cols = [], []
        for kv in range(ATT_KV_HEADS):
            s = s_all[c * CHUNK:(c + 1) * CHUNK, kv * 3 * CHUNK:(kv + 1) * 3 * CHUNK]
            s = jnp.where(ok, s, NEG_INF)
            sink = sink_ref[kv * ATT_GROUPS + c] * LOG2E
            m = jnp.maximum(jnp.max(s, axis=-1, keepdims=True), sink)
            e = jnp.exp2(s - m)
            denom = jnp.sum(e, axis=-1, keepdims=True) + jnp.exp2(sink - m)
            e_cols.append(e.astype(BF16))
            inv_cols.append(1.0 / denom)
        e_rows.append(jnp.concatenate(e_cols, axis=1))
        inv_rows.append(jnp.where(head0, inv_cols[0], inv_cols[1]))
    o = _dot(jnp.concatenate(e_rows, axis=0), vb)
    return jnp.concatenate([(o[c * CHUNK:(c + 1) * CHUNK] * inv_rows[c]).astype(BF16)
                            for c in range(ATT_GROUPS)], axis=1)


def _mixer_kernel(dec_ref, sink_ref, rq_ref, rk_ref, rv_ref, rg_ref, sbs_ref, gain_ref,
                  aq_ref, kp_ref, km_ref, kn_ref, vp_ref, vm_ref, vn_ref,
                  h_ref, cb_ref, u_ref, up_ref, un_ref, gs_ref, cw_ref, wr_ref, wc_ref, wa_ref, wo_ref,
                  o_ref, sf_ref, tab_ref, us_ref, *, seq_rows):
    blk = pl.program_id(1)
    tq = SEQ_CHUNKS * CHUNK

    @pl.when(blk == 0)
    def _():
        sf_ref[...] = jnp.zeros_like(sf_ref)
        _ret_tables(dec_ref, tab_ref)

    upos = blk * tq - 8 + lax.broadcasted_iota(jnp.int32, (tq + 16, 1), 0)
    us_ref[pl.ds(8, tq), :] = u_ref[0].astype(F32)
    us_ref[pl.ds(0, 8), :] = up_ref[0, pl.ds(HALO_ROWS - 8, 8), :].astype(F32)
    us_ref[pl.ds(8 + tq, 8), :] = un_ref[0, pl.ds(0, 8), :].astype(F32)
    us_ref[...] = jnp.where(upos < seq_rows, us_ref[...], 0.0)

    window = _attn_window(kp_ref, km_ref, kn_ref, vp_ref, vm_ref, vn_ref, blk, seq_rows)
    group = SEQ_CHUNKS // MERGE_SPLIT
    gr = group * CHUNK
    for part in range(MERGE_SPLIT):
        yr = jnp.concatenate([_ret_chunk(rq_ref, rk_ref, rv_ref, rg_ref, sbs_ref, gain_ref, sf_ref, tab_ref, jj)
                              for jj in range(part * group, (part + 1) * group)], axis=0)
        ya = jnp.concatenate([_attn_chunk(sink_ref, aq_ref, window, blk, jj, seq_rows)
                              for jj in range(part * group, (part + 1) * group)], axis=0)
        r0 = part * gr
        conv = (us_ref[pl.ds(r0 + 7, gr), :] * cw_ref[0:1, :] + us_ref[pl.ds(r0 + 8, gr), :] * cw_ref[1:2, :]
                + us_ref[pl.ds(r0 + 9, gr), :] * cw_ref[2:3, :])
        yc = (cb_ref[0, pl.ds(r0, gr), :].astype(F32) * conv).astype(BF16)

        def branch(y, w_ref, b):
            gate = gs_ref[0, pl.ds(r0, gr), b * D_MODEL:(b + 1) * D_MODEL].astype(F32)
            return gate * _dot(y, w_ref[...])

        merged = branch(yr, wr_ref, 0) + branch(yc, wc_ref, 1) + branch(ya, wa_ref, 2)
        mix = _dot(merged.astype(BF16), wo_ref[...])
        pos = blk * tq + r0 + lax.broadcasted_iota(jnp.int32, (gr, 1), 0)
        o_ref[0, pl.ds(r0, gr), :] = h_ref[0, pl.ds(r0, gr), :] + jnp.where(pos < PAD, 0.0, mix)


def _mixer(h, dec, sink, gn_gain, conv_w, wr, wc, wa, wo, proj, batch):
    rq, rk, rv, rg, cb, u, aq, ak, av, gs = proj
    rows = h.shape[0]
    p = rows // batch
    n = p // CHUNK
    tq = SEQ_CHUNKS * CHUNK
    nblk = pl.cdiv(n, SEQ_CHUNKS)
    v3 = lambda t: t.reshape(batch, p, t.shape[-1])
    rk3, rv3, ak3, av3, u3 = v3(rk), v3(rv), v3(ak), v3(av), v3(u)
    sbs = _ret_rev(dec, rk3, rv3)

    smem = pl.BlockSpec(memory_space=pltpu.SMEM)
    main = lambda w: pl.BlockSpec((1, tq, w), lambda b, i: (b, i, 0))
    prev = pl.BlockSpec((1, CHUNK, ATT_KV_WIDTH), lambda b, i: (b, jnp.maximum(i * SEQ_CHUNKS - 1, 0), 0))
    nxt = pl.BlockSpec((1, CHUNK, ATT_KV_WIDTH),
                       lambda b, i: (b, jnp.minimum((i + 1) * SEQ_CHUNKS, n - 1), 0))
    hb = tq // HALO_ROWS
    halo_prev = pl.BlockSpec((1, HALO_ROWS, CONV_WIDTH), lambda b, i: (b, jnp.maximum(i * hb - 1, 0), 0))
    halo_next = pl.BlockSpec((1, HALO_ROWS, CONV_WIDTH),
                             lambda b, i: (b, jnp.minimum((i + 1) * hb, p // HALO_ROWS - 1), 0))
    tiles = SEQ_CHUNKS * RET_HEADS
    out = pl.pallas_call(
        functools.partial(_mixer_kernel, seq_rows=p),
        out_shape=jax.ShapeDtypeStruct((batch, p, D_MODEL), F32),
        grid=(batch, nblk),
        in_specs=[smem, smem, main(RET_WIDTH), main(RET_WIDTH), main(RET_WIDTH), main(RET_WIDTH),
                  pl.BlockSpec((1, 1, tiles, RET_DIM, RET_DIM), lambda b, i: (b, i, 0, 0, 0)),
                  _resident((1, RET_WIDTH)),
                  main(ATT_WIDTH), prev, main(ATT_KV_WIDTH), nxt, prev, main(ATT_KV_WIDTH), nxt,
                  main(D_MODEL), main(CONV_WIDTH), main(CONV_WIDTH), halo_prev, halo_next, main(GATE_WIDTH),
                  _resident((3, CONV_WIDTH)), _resident((RET_WIDTH, D_MODEL)), _resident((CONV_WIDTH, D_MODEL)),
                  _resident((ATT_WIDTH, D_MODEL)), _resident((D_MODEL, D_MODEL))],
        out_specs=main(D_MODEL),
        scratch_shapes=[pltpu.VMEM((RET_HEADS, RET_DIM, RET_DIM), F32),
                        pltpu.VMEM((RET_HEADS * N_TABS, CHUNK, CHUNK), F32),
                        pltpu.VMEM((tq + 16, CONV_WIDTH), F32)],
        compiler_params=_params("parallel", "arbitrary"),
        name="mixer",
    )(dec, sink, v3(rq), rk3, rv3, v3(rg), sbs, gn_gain,
      v3(aq), ak3, ak3, ak3, av3, av3, av3,
      v3(h), v3(cb), u3, u3, u3, v3(gs), conv_w, wr, wc, wa, wo)
    return out.reshape(rows, D_MODEL)


def _rope_tables(p, batch):
    pos = jnp.arange(p, dtype=F32) - float(PAD)

    def tab(d):
        inv = ROPE_THETA ** (-jnp.arange(0, d, 2, dtype=F32) / d)
        ang = pos[:, None] * inv[None, :]
        cos, sin = jnp.cos(ang), jnp.sin(ang)
        reps = LANES // d
        c = jnp.tile(jnp.concatenate([cos, cos], axis=1), (batch, reps))
        s = jnp.tile(jnp.concatenate([-sin, sin], axis=1), (batch, reps))
        return c, s

    cr, sr = tab(RET_DIM)
    ca, sa = tab(ATT_DIM)
    return cr, sr, ca, sa


def _pair_heads(t, axis):
    shape = t.shape
    t = t.reshape(shape[:axis] + (ATT_KV_HEADS, ATT_GROUPS, ATT_DIM) + shape[axis + 1:])
    t = jnp.swapaxes(t, axis, axis + 1)
    return t.reshape(shape)


def _prep_weights(w_in, w_attn_out):
    aq = _pair_heads(w_in[:, :, OFF_AQ:OFF_AK].astype(BF16), 2)
    w_in = lax.dynamic_update_slice(w_in.astype(BF16), aq, (0, 0, OFF_AQ))
    return w_in, _pair_heads(w_attn_out, 1).astype(BF16)


def _trunk(x, meta_tokens, layers, final_norm):
    batch, s, _ = x.shape
    p = s + CHUNK
    rows = batch * p
    meta = jnp.broadcast_to(meta_tokens[None], (batch, N_META, D_MODEL))
    h = jnp.concatenate([jnp.zeros((batch, PAD, D_MODEL), F32), meta, x], axis=1).reshape(rows, D_MODEL)
    tabs = _rope_tables(p, batch)
    for idx, ly in enumerate(layers):
        h = _ffn(h, ly["norm_ffn1"], ly["wg1"], ly["wu1"], ly["wd1"])
        proj = _inproj(h, ly["norm_mix"], ly["w_in"], tabs)
        h = _mixer(h, ly["ret_decay"], ly["attn_sink"], ly["ret_gn_gain"], ly["conv_w"], ly["w_ret_out"],
                   ly["w_conv_out"], ly["w_attn_out"], ly["w_o"], proj, batch)
        if idx + 1 < len(layers):
            h = _ffn(h, ly["norm_ffn2"], ly["wg2"], ly["wu2"], ly["wd2"])
    ly = layers[-1]
    return _ffn_final(h.reshape(batch, p, D_MODEL), ly["norm_ffn2"], ly["wg2"], ly["wu2"], ly["wd2"],
                      final_norm.reshape(1, D_MODEL))


def kernel(x_prompt, x_sample, meta_tokens, norm_ffn1, w_ffn1_gate, w_ffn1_up, w_ffn1_down, norm_mix, w_in, ret_decay, ret_gn_gain, conv_w, attn_sink, w_ret_out, w_conv_out, w_attn_out, w_o, norm_ffn2, w_ffn2_gate, w_ffn2_up, w_ffn2_down, final_norm):
    depth = w_in.shape[0]
    w_in_b, w_attn_out_b = _prep_weights(w_in, w_attn_out)
    layers = []
    for l in range(depth):
        layers.append(dict(
            norm_ffn1=norm_ffn1[l].reshape(1, D_MODEL), wg1=w_ffn1_gate[l].astype(BF16),
            wu1=w_ffn1_up[l].astype(BF16), wd1=w_ffn1_down[l].astype(BF16),
            norm_mix=norm_mix[l].reshape(1, D_MODEL), w_in=w_in_b[l],
            ret_decay=ret_decay[l], ret_gn_gain=ret_gn_gain[l].reshape(1, RET_WIDTH),
            conv_w=conv_w[l], attn_sink=attn_sink[l],
            w_ret_out=w_ret_out[l].astype(BF16), w_conv_out=w_conv_out[l].astype(BF16),
            w_attn_out=w_attn_out_b[l], w_o=w_o[l].astype(BF16),
            norm_ffn2=norm_ffn2[l].reshape(1, D_MODEL), wg2=w_ffn2_gate[l].astype(BF16),
            wu2=w_ffn2_up[l].astype(BF16), wd2=w_ffn2_down[l].astype(BF16)))
    y_prompt = _trunk(x_prompt, meta_tokens, layers, final_norm)
    y_sample = _trunk(x_sample, meta_tokens, layers, final_norm)
    return (y_prompt, y_sample)
```

```python
import functools
import math

import jax
import jax.numpy as jnp
from jax import lax
from jax.experimental import pallas as pl
from jax.experimental.pallas import tpu as pltpu

F32 = jnp.float32
BF16 = jnp.bfloat16

D_MODEL = 1024
D_FF = 2816
N_META = 16
CHUNK = 128
PAD = CHUNK - N_META
RET_HEADS = 4
RET_DIM = 128
RET_WIDTH = RET_HEADS * RET_DIM
CONV_WIDTH = 512
ATT_Q_HEADS = 8
ATT_KV_HEADS = 2
ATT_DIM = 64
ATT_GROUPS = ATT_Q_HEADS // ATT_KV_HEADS
ATT_WIDTH = ATT_Q_HEADS * ATT_DIM
ATT_KV_WIDTH = ATT_KV_HEADS * ATT_DIM
WINDOW = 128
GATE_WIDTH = 3 * D_MODEL
ROPE_THETA = 10000.0
EPS = 1e-6
NEG_INF = -1e30

LANES = 128
MXU_COLS = 256
HALO_ROWS = 16
FF_CHUNK = MXU_COLS
SEQ_CHUNKS = 4
REV_CHUNKS = 2 * SEQ_CHUNKS
MERGE_SPLIT = 2
VMEM_LIMIT = 56 * 1024 * 1024
LOG2E = math.log2(math.e)
ATT_Q_SCALE = ATT_DIM ** -0.5 * LOG2E

OFF_RQ = 0
OFF_RK = OFF_RQ + RET_WIDTH
OFF_RV = OFF_RK + RET_WIDTH
OFF_RG = OFF_RV + RET_WIDTH
OFF_CB = OFF_RG + RET_WIDTH
OFF_CC = OFF_CB + CONV_WIDTH
OFF_CX = OFF_CC + CONV_WIDTH
OFF_AQ = OFF_CX + CONV_WIDTH
OFF_AK = OFF_AQ + ATT_WIDTH
OFF_AV = OFF_AK + ATT_KV_WIDTH
OFF_GATE = OFF_AV + ATT_KV_WIDTH
IN_WIDTH = OFF_GATE + GATE_WIDTH


def _params(*sem):
    return pltpu.CompilerParams(dimension_semantics=sem, vmem_limit_bytes=VMEM_LIMIT)


def _resident(shape):
    nd = len(shape)
    return pl.BlockSpec(shape, lambda *_: (0,) * nd, pipeline_mode=pl.Buffered(1))


def _row_tile(rows, cap):
    best = CHUNK
    t = CHUNK
    while t <= cap:
        if rows % t == 0:
            best = t
        t += CHUNK
    return best


def _rms(x, gain):
    ms = jnp.mean(x * x, axis=-1, keepdims=True)
    return x * lax.rsqrt(ms + EPS) * gain


def _dot(a, b):
    return jnp.dot(a, b, preferred_element_type=F32)


def _dot_nt(a, b):
    return lax.dot_general(a, b, (((1,), (1,)), ((), ())), preferred_element_type=F32)


def _ffn_body(x, g_ref, wg_ref, wu_ref, wd_ref, acc_ref):
    n = _rms(x, g_ref[...]).astype(BF16)
    for c in range(D_FF // FF_CHUNK):
        sl = slice(c * FF_CHUNK, (c + 1) * FF_CHUNK)
        g = _dot(n, wg_ref[:, sl])
        u = _dot(n, wu_ref[:, sl])
        a = (g * jax.nn.sigmoid(g) * u).astype(BF16)
        d = _dot(a, wd_ref[sl, :])
        if c == 0:
            acc_ref[...] = d
        else:
            acc_ref[...] += d
    return x + 0.5 * acc_ref[...]


def _ffn_kernel(x_ref, g_ref, wg_ref, wu_ref, wd_ref, o_ref, acc_ref):
    o_ref[...] = _ffn_body(x_ref[...], g_ref, wg_ref, wu_ref, wd_ref, acc_ref)


def _ffn_final_kernel(x_ref, g_ref, wg_ref, wu_ref, wd_ref, fg_ref, o_ref, acc_ref):
    o_ref[...] = _rms(_ffn_body(x_ref[0], g_ref, wg_ref, wu_ref, wd_ref, acc_ref), fg_ref[...])


def _ffn_weight_specs():
    return [_resident((1, D_MODEL)), _resident((D_MODEL, D_FF)), _resident((D_MODEL, D_FF)),
            _resident((D_FF, D_MODEL))]


def _ffn(h, gain, wg, wu, wd):
    rows = h.shape[0]
    tm = _row_tile(rows, 1024)
    return pl.pallas_call(
        _ffn_kernel,
        out_shape=jax.ShapeDtypeStruct((rows, D_MODEL), F32),
        grid=(rows // tm,),
        in_specs=[pl.BlockSpec((tm, D_MODEL), lambda i: (i, 0))] + _ffn_weight_specs(),
        out_specs=pl.BlockSpec((tm, D_MODEL), lambda i: (i, 0)),
        scratch_shapes=[pltpu.VMEM((tm, D_MODEL), F32)],
        compiler_params=_params("parallel"),
        name="ffn",
    )(h, gain, wg, wu, wd)


def _ffn_final(h3, gain, wg, wu, wd, final_gain):
    batch, p, _ = h3.shape
    s = p - CHUNK
    tm = _row_tile(s, 1024)
    return pl.pallas_call(
        _ffn_final_kernel,
        out_shape=jax.ShapeDtypeStruct((batch, s, D_MODEL), F32),
        grid=(batch, s // tm),
        in_specs=[pl.BlockSpec((pl.Element(1), pl.Element(tm), pl.Element(D_MODEL)),
                               lambda b, j: (b, pl.multiple_of(CHUNK + j * tm, CHUNK), 0))]
        + _ffn_weight_specs() + [_resident((1, D_MODEL))],
        out_specs=pl.BlockSpec((None, tm, D_MODEL), lambda b, j: (b, j, 0)),
        scratch_shapes=[pltpu.VMEM((tm, D_MODEL), F32)],
        compiler_params=_params("parallel", "parallel"),
        name="ffn_final",
    )(h3, gain, wg, wu, wd, final_gain)


def _rot128(x, cos, sin):
    return x * cos + pltpu.roll(x, 64, axis=1) * sin


def _rot64(x, cos, sin, low_half):
    partner = jnp.where(low_half, pltpu.roll(x, 96, axis=1), pltpu.roll(x, 32, axis=1))
    return x * cos + partner * sin


def _log_gamma_tile(dec_ref, direction, hd):
    return jnp.log1p(-jnp.exp(jnp.full((CHUNK, CHUNK), dec_ref[direction, hd], F32)))


D_QF, D_QB, D_VF, D_VB = range(4)


def _row_decay_tables(dec_ref, tab_ref):
    i = lax.broadcasted_iota(jnp.int32, (CHUNK, CHUNK), 0).astype(F32)
    for hd in range(RET_HEADS):
        sl = slice(hd * RET_DIM, (hd + 1) * RET_DIM)
        lgf = _log_gamma_tile(dec_ref, 0, hd)
        lgb = _log_gamma_tile(dec_ref, 1, hd)
        tab_ref[D_QF, :, sl] = jnp.exp(lgf * (i + 1.0))
        tab_ref[D_QB, :, sl] = jnp.exp(lgb * (CHUNK - i))
        tab_ref[D_VF, :, sl] = jnp.exp(lgf * (CHUNK - 1.0 - i))
        tab_ref[D_VB, :, sl] = jnp.exp(lgb * i)


def _inproj_kernel(dec_ref, x_ref, xp_ref, xn_ref, g_ref, w_ref, cw_ref, cr_ref, sr_ref, ca_ref, sa_ref,
                   rq_ref, qd_ref, rk_ref, rkt_ref, rv_ref, rvf_ref, rvb_ref, rg_ref, yc_ref,
                   aq_ref, ak_ref, av_ref, gs_ref, dtab_ref, us_ref, *, total_rows):
    i = pl.program_id(0)
    tm = x_ref.shape[0]

    @pl.when(i == 0)
    def _():
        _row_decay_tables(dec_ref, dtab_ref)

    gain = g_ref[...]
    n = _rms(x_ref[...], gain).astype(BF16)

    def proj(off, width):
        return _dot(n, w_ref[:, off:off + width])

    cr, sr = cr_ref[...], sr_ref[...]
    ca, sa = ca_ref[...], sa_ref[...]
    low_half = (lax.broadcasted_iota(jnp.int32, (1, LANES), 1) % ATT_DIM) < (ATT_DIM // 2)
    chunks = [slice(ch * CHUNK, (ch + 1) * CHUNK) for ch in range(tm // CHUNK)]
    for c in range(RET_WIDTH // MXU_COLS):
        zq = proj(OFF_RQ + c * MXU_COLS, MXU_COLS)
        zk = proj(OFF_RK + c * MXU_COLS, MXU_COLS)
        for half in range(MXU_COLS // RET_DIM):
            hd = c * (MXU_COLS // RET_DIM) + half
            src = slice(half * RET_DIM, (half + 1) * RET_DIM)
            dst = slice(hd * RET_DIM, (hd + 1) * RET_DIM)
            q = _rot128(zq[:, src], cr, sr)
            k = _rot128(zk[:, src], cr, sr) * (RET_DIM ** -0.5)
            rq_ref[:, dst] = q.astype(BF16)
            rk_ref[:, dst] = k.astype(BF16)
            for ch, rows in enumerate(chunks):
                qd_ref[rows, 2 * hd * RET_DIM:(2 * hd + 1) * RET_DIM] = (q[rows] * dtab_ref[D_QF, :, dst]).astype(BF16)
                qd_ref[rows, (2 * hd + 1) * RET_DIM:(2 * hd + 2) * RET_DIM] = (q[rows] * dtab_ref[D_QB, :, dst]).astype(BF16)
                rkt_ref[ch, hd] = k[rows].T.astype(BF16)
    zv = proj(OFF_RV, RET_WIDTH)
    rv_ref[...] = zv.astype(BF16)
    for rows in chunks:
        rvf_ref[rows, :] = (zv[rows] * dtab_ref[D_VF]).astype(BF16)
        rvb_ref[rows, :] = (zv[rows] * dtab_ref[D_VB]).astype(BF16)
    zg = proj(OFF_RG, RET_WIDTH)
    rg_ref[...] = (zg * jax.nn.sigmoid(zg)).astype(BF16)

    n_ext = jnp.concatenate([_rms(xp_ref[...], gain).astype(BF16), n, _rms(xn_ref[...], gain).astype(BF16)], axis=0)
    u = (_dot(n_ext, w_ref[:, OFF_CC:OFF_CC + CONV_WIDTH]) * _dot(n_ext, w_ref[:, OFF_CX:OFF_CX + CONV_WIDTH]))
    grow = i * tm - HALO_ROWS + lax.broadcasted_iota(jnp.int32, (tm + 2 * HALO_ROWS, 1), 0)
    us_ref[...] = jnp.where((grow >= 0) & (grow < total_rows), u, 0.0)
    conv = (us_ref[pl.ds(HALO_ROWS - 1, tm), :] * cw_ref[0:1, :] + us_ref[pl.ds(HALO_ROWS, tm), :] * cw_ref[1:2, :]
            + us_ref[pl.ds(HALO_ROWS + 1, tm), :] * cw_ref[2:3, :])
    yc_ref[...] = (proj(OFF_CB, CONV_WIDTH) * conv).astype(BF16)

    for c in range(ATT_WIDTH // MXU_COLS):
        zq = proj(OFF_AQ + c * MXU_COLS, MXU_COLS)
        for half in range(MXU_COLS // LANES):
            src = slice(half * LANES, (half + 1) * LANES)
            dst = slice(c * MXU_COLS + half * LANES, c * MXU_COLS + (half + 1) * LANES)
            aq_ref[:, dst] = (_rot64(zq[:, src], ca, sa, low_half) * ATT_Q_SCALE).astype(BF16)
    zkv = proj(OFF_AK, 2 * ATT_KV_WIDTH)
    ak_ref[...] = _rot64(zkv[:, :ATT_KV_WIDTH], ca, sa, low_half).astype(BF16)
    av_ref[...] = zkv[:, ATT_KV_WIDTH:].astype(BF16)
    for c in range(GATE_WIDTH // 512):
        sl = slice(c * 512, (c + 1) * 512)
        gs_ref[:, sl] = jax.nn.sigmoid(proj(OFF_GATE + c * 512, 512)).astype(BF16)


INPROJ_OUT = ("rq", "qd", "rk", "rkt", "rv", "rvf", "rvb", "rg", "yc", "aq", "ak", "av", "gs")


def _inproj(h, dec, gain, w_in, conv_w, tabs):
    rows = h.shape[0]
    tm = _row_tile(rows, 512)
    hb = tm // HALO_ROWS
    row = lambda w: pl.BlockSpec((tm, w), lambda i: (i, 0))
    halo_prev = pl.BlockSpec((HALO_ROWS, D_MODEL), lambda i: (jnp.maximum(i * hb - 1, 0), 0))
    halo_next = pl.BlockSpec((HALO_ROWS, D_MODEL),
                              lambda i: (jnp.minimum((i + 1) * hb, rows // HALO_ROWS - 1), 0))
    widths = dict(rq=RET_WIDTH, qd=2 * RET_WIDTH, rk=RET_WIDTH, rv=RET_WIDTH, rvf=RET_WIDTH, rvb=RET_WIDTH,
                  rg=RET_WIDTH, yc=CONV_WIDTH, aq=ATT_WIDTH, ak=ATT_KV_WIDTH, av=ATT_KV_WIDTH, gs=GATE_WIDTH)
    kt_shape = (rows // CHUNK, RET_HEADS, RET_DIM, RET_DIM)
    kt_spec = pl.BlockSpec((tm // CHUNK,) + kt_shape[1:], lambda i: (i, 0, 0, 0))
    outs = pl.pallas_call(
        functools.partial(_inproj_kernel, total_rows=rows),
        out_shape=[jax.ShapeDtypeStruct(kt_shape if name == "rkt" else (rows, widths[name]), BF16)
                   for name in INPROJ_OUT],
        grid=(rows // tm,),
        in_specs=[pl.BlockSpec(memory_space=pltpu.SMEM), row(D_MODEL), halo_prev, halo_next,
                  _resident((1, D_MODEL)), _resident((D_MODEL, IN_WIDTH)), _resident((3, CONV_WIDTH)),
                  row(LANES), row(LANES), row(LANES), row(LANES)],
        out_specs=[kt_spec if name == "rkt" else row(widths[name]) for name in INPROJ_OUT],
        scratch_shapes=[pltpu.VMEM((4, CHUNK, RET_WIDTH), F32),
                        pltpu.VMEM((tm + 2 * HALO_ROWS, CONV_WIDTH), F32)],
        compiler_params=_params("arbitrary"),
        name="inproj",
    )(dec, h, h, h, gain, w_in, conv_w, *tabs)
    return dict(zip(INPROJ_OUT, outs))


T_CF, T_CB, T_MASK = range(3)
N_TABS = 3


def _chunk_tables(dec_ref, tab_ref):
    i = lax.broadcasted_iota(jnp.int32, (CHUNK, CHUNK), 0).astype(F32)
    j = lax.broadcasted_iota(jnp.int32, (CHUNK, CHUNK), 1).astype(F32)
    diff = i - j
    for hd in range(RET_HEADS):
        lgf = _log_gamma_tile(dec_ref, 0, hd)
        lgb = _log_gamma_tile(dec_ref, 1, hd)
        base = hd * N_TABS
        tab_ref[base + T_CF] = jnp.exp(lgf * CHUNK)
        tab_ref[base + T_CB] = jnp.exp(lgb * CHUNK)
        tab_ref[base + T_MASK] = jnp.where(diff >= 0, jnp.exp(lgf * jnp.maximum(diff, 0.0)),
                                           jnp.exp(lgb * jnp.maximum(-diff, 0.0)))


def _state_update(s_ref, tab_ref, hd, kt, vd, t_c):
    s_ref[hd] = tab_ref[hd * N_TABS + t_c] * s_ref[hd] + _dot(kt, vd)


def _ret_rev_kernel(dec_ref, kt_ref, vb_ref, sbs_ref, sb_ref, tab_ref, *, n_chunks):
    c = pl.program_id(1)
    blk = pl.num_programs(1) - 1 - c

    @pl.when(c == 0)
    def _():
        sb_ref[...] = jnp.zeros_like(sb_ref)
        _chunk_tables(dec_ref, tab_ref)

    for jj in reversed(range(REV_CHUNKS)):
        valid = blk * REV_CHUNKS + jj < n_chunks
        rows = slice(jj * CHUNK, (jj + 1) * CHUNK)
        for hd in range(RET_HEADS):
            sl = slice(hd * RET_DIM, (hd + 1) * RET_DIM)
            sbs_ref[0, 0, jj * RET_HEADS + hd] = sb_ref[hd].astype(BF16)
            kt = kt_ref[0, jj, hd]
            vd = vb_ref[0, rows, sl]
            kt = jnp.where(valid, kt, jnp.zeros_like(kt))
            vd = jnp.where(valid, vd, jnp.zeros_like(vd))
            _state_update(sb_ref, tab_ref, hd, kt, vd, T_CB)


def _ret_rev(dec, rkt5, rvb3):
    batch, p, _ = rvb3.shape
    n = p // CHUNK
    nblk = pl.cdiv(n, REV_CHUNKS)
    tiles = REV_CHUNKS * RET_HEADS
    tile = (RET_DIM, RET_DIM)
    sbs = pl.pallas_call(
        functools.partial(_ret_rev_kernel, n_chunks=n),
        out_shape=jax.ShapeDtypeStruct((batch, nblk, tiles) + tile, BF16),
        grid=(batch, nblk),
        in_specs=[pl.BlockSpec(memory_space=pltpu.SMEM),
                  pl.BlockSpec((1, REV_CHUNKS, RET_HEADS) + tile, lambda b, c: (b, nblk - 1 - c, 0, 0, 0)),
                  pl.BlockSpec((1, REV_CHUNKS * CHUNK, RET_WIDTH), lambda b, c: (b, nblk - 1 - c, 0))],
        out_specs=pl.BlockSpec((1, 1, tiles) + tile, lambda b, c: (b, nblk - 1 - c, 0, 0, 0)),
        scratch_shapes=[pltpu.VMEM((RET_HEADS,) + tile, F32),
                        pltpu.VMEM((RET_HEADS * N_TABS, CHUNK, CHUNK), F32)],
        compiler_params=_params("parallel", "arbitrary"),
        name="ret_rev",
    )(dec, rkt5, rvb3)
    per = REV_CHUNKS // SEQ_CHUNKS
    return sbs.reshape((batch, nblk * per, SEQ_CHUNKS * RET_HEADS) + tile)


def _ret_chunk(q_ref, qd_ref, k_ref, kt_ref, v_ref, vf_ref, g_ref, sbs_ref, gain_ref, sf_ref, tab_ref, jj):
    rows = slice(jj * CHUNK, (jj + 1) * CHUNK)
    outs = []
    for hd in range(RET_HEADS):
        sl = slice(hd * RET_DIM, (hd + 1) * RET_DIM)
        base = hd * N_TABS
        att = _dot_nt(q_ref[0, rows, sl], k_ref[0, rows, sl]) * tab_ref[base + T_MASK]
        o = _dot(att.astype(BF16), v_ref[0, rows, sl])
        sd = jnp.concatenate([sf_ref[hd].astype(BF16), sbs_ref[0, 0, jj * RET_HEADS + hd]], axis=0)
        o += _dot(qd_ref[0, rows, 2 * hd * RET_DIM:(2 * hd + 2) * RET_DIM], sd)
        _state_update(sf_ref, tab_ref, hd, kt_ref[0, jj, hd], vf_ref[0, rows, sl], T_CF)
        mu = jnp.mean(o, axis=-1, keepdims=True)
        oc = o - mu
        var = jnp.mean(oc * oc, axis=-1, keepdims=True)
        on = oc * lax.rsqrt(var + EPS) * gain_ref[:, sl]
        outs.append((g_ref[0, rows, sl].astype(F32) * on).astype(BF16))
    return jnp.concatenate(outs, axis=1)


def _attn_window(kp_ref, km_ref, kn_ref, vp_ref, vm_ref, vn_ref, blk, seq_rows):
    tq = SEQ_CHUNKS * CHUNK
    span = tq + 2 * CHUNK
    head0 = lax.broadcasted_iota(jnp.int32, (1, LANES), 1) < ATT_DIM
    wpos = blk * tq - CHUNK + lax.broadcasted_iota(jnp.int32, (span, 1), 0)
    inside = (wpos >= 0) & (wpos < seq_rows)
    kall = jnp.concatenate([kp_ref[0], km_ref[0], kn_ref[0]], axis=0)
    vall = jnp.concatenate([vp_ref[0], vm_ref[0], vn_ref[0]], axis=0)
    zero = jnp.zeros_like(kall)
    kall = jnp.where(inside, kall, zero)
    vall = jnp.where(inside, vall, zero)
    return (jnp.where(head0, kall, zero), jnp.where(head0, zero, kall),
            jnp.where(head0, vall, zero), jnp.where(head0, zero, vall))


def _attn_chunk(sink_ref, q_ref, window, blk, jj, seq_rows):
    k0, k1, v0, v1 = window
    base = jj * CHUNK
    head0 = lax.broadcasted_iota(jnp.int32, (1, LANES), 1) < ATT_DIM
    r = lax.broadcasted_iota(jnp.int32, (CHUNK, 3 * CHUNK), 0)
    j = lax.broadcasted_iota(jnp.int32, (CHUNK, 3 * CHUNK), 1)
    kpos = (blk * SEQ_CHUNKS + jj - 1) * CHUNK + j
    ok = (jnp.abs(CHUNK + r - j) <= WINDOW) & (kpos >= PAD) & (kpos < seq_rows)
    kb = jnp.concatenate([k0[base:base + 3 * CHUNK], k1[base:base + 3 * CHUNK]], axis=0)
    vb = jnp.concatenate([v0[base:base + 3 * CHUNK], v1[base:base + 3 * CHUNK]], axis=0)
    qs = jnp.concatenate([q_ref[0, base:base + CHUNK, c * LANES:(c + 1) * LANES]
                          for c in range(ATT_GROUPS)], axis=0)
    s_all = _dot_nt(qs, kb)
    e_rows, inv_rows = [], []
    for c in range(ATT_GROUPS):
        e_cols, inv_cols = [], []
        for kv in range(ATT_KV_HEADS):
            s = s_all[c * CHUNK:(c + 1) * CHUNK, kv * 3 * CHUNK:(kv + 1) * 3 * CHUNK]
            s = jnp.where(ok, s, NEG_INF)
            sink = sink_ref[kv * ATT_GROUPS + c] * LOG2E
            m = jnp.maximum(jnp.max(s, axis=-1, keepdims=True), sink)
            e = jnp.exp2(s - m)
            denom = jnp.sum(e, axis=-1, keepdims=True) + jnp.exp2(sink - m)
            e_cols.append(e.astype(BF16))
            inv_cols.append(1.0 / denom)
        e_rows.append(jnp.concatenate(e_cols, axis=1))
        inv_rows.append(jnp.where(head0, inv_cols[0], inv_cols[1]))
    o = _dot(jnp.concatenate(e_rows, axis=0), vb)
    return jnp.concatenate([(o[c * CHUNK:(c + 1) * CHUNK] * inv_rows[c]).astype(BF16)
                            for c in range(ATT_GROUPS)], axis=1)


def _mixer_kernel(dec_ref, sink_ref, rq_ref, qd_ref, rk_ref, rkt_ref, rv_ref, rvf_ref, rg_ref, sbs_ref, gain_ref,
                  aq_ref, kp_ref, km_ref, kn_ref, vp_ref, vm_ref, vn_ref,
                  h_ref, yc_ref, gs_ref, wr_ref, wc_ref, wa_ref, wo_ref,
                  o_ref, sf_ref, tab_ref, *, seq_rows):
    blk = pl.program_id(1)
    tq = SEQ_CHUNKS * CHUNK

    @pl.when(blk == 0)
    def _():
        sf_ref[...] = jnp.zeros_like(sf_ref)
        _chunk_tables(dec_ref, tab_ref)

    window = _attn_window(kp_ref, km_ref, kn_ref, vp_ref, vm_ref, vn_ref, blk, seq_rows)
    group = SEQ_CHUNKS // MERGE_SPLIT
    gr = group * CHUNK
    for part in range(MERGE_SPLIT):
        yr = jnp.concatenate([_ret_chunk(rq_ref, qd_ref, rk_ref, rkt_ref, rv_ref, rvf_ref, rg_ref, sbs_ref,
                                         gain_ref, sf_ref, tab_ref, jj)
                              for jj in range(part * group, (part + 1) * group)], axis=0)
        ya = jnp.concatenate([_attn_chunk(sink_ref, aq_ref, window, blk, jj, seq_rows)
                              for jj in range(part * group, (part + 1) * group)], axis=0)
        r0 = part * gr
        yc = yc_ref[0, pl.ds(r0, gr), :]

        def branch(y, w_ref, b):
            gate = gs_ref[0, pl.ds(r0, gr), b * D_MODEL:(b + 1) * D_MODEL].astype(F32)
            return gate * _dot(y, w_ref[...])

        merged = branch(yr, wr_ref, 0) + branch(yc, wc_ref, 1) + branch(ya, wa_ref, 2)
        mix = _dot(merged.astype(BF16), wo_ref[...])
        pos = blk * tq + r0 + lax.broadcasted_iota(jnp.int32, (gr, 1), 0)
        o_ref[0, pl.ds(r0, gr), :] = h_ref[0, pl.ds(r0, gr), :] + jnp.where(pos < PAD, 0.0, mix)


def _mixer(h, dec, sink, gn_gain, wr, wc, wa, wo, proj, batch):
    rows = h.shape[0]
    p = rows // batch
    n = p // CHUNK
    tq = SEQ_CHUNKS * CHUNK
    nblk = pl.cdiv(n, SEQ_CHUNKS)
    v3 = lambda t: t.reshape(batch, p, t.shape[-1])
    tile = (RET_DIM, RET_DIM)
    rkt5 = proj["rkt"].reshape((batch, n, RET_HEADS) + tile)
    ak3, av3 = v3(proj["ak"]), v3(proj["av"])
    sbs = _ret_rev(dec, rkt5, v3(proj["rvb"]))

    smem = pl.BlockSpec(memory_space=pltpu.SMEM)
    main = lambda w: pl.BlockSpec((1, tq, w), lambda b, i: (b, i, 0))
    prev = pl.BlockSpec((1, CHUNK, ATT_KV_WIDTH), lambda b, i: (b, jnp.maximum(i * SEQ_CHUNKS - 1, 0), 0))
    nxt = pl.BlockSpec((1, CHUNK, ATT_KV_WIDTH),
                       lambda b, i: (b, jnp.minimum((i + 1) * SEQ_CHUNKS, n - 1), 0))
    tiles = SEQ_CHUNKS * RET_HEADS
    out = pl.pallas_call(
        functools.partial(_mixer_kernel, seq_rows=p),
        out_shape=jax.ShapeDtypeStruct((batch, p, D_MODEL), F32),
        grid=(batch, nblk),
        in_specs=[smem, smem, main(RET_WIDTH), main(2 * RET_WIDTH), main(RET_WIDTH),
                  pl.BlockSpec((1, SEQ_CHUNKS, RET_HEADS) + tile, lambda b, i: (b, i, 0, 0, 0)),
                  main(RET_WIDTH), main(RET_WIDTH), main(RET_WIDTH),
                  pl.BlockSpec((1, 1, tiles) + tile, lambda b, i: (b, i, 0, 0, 0)),
                  _resident((1, RET_WIDTH)),
                  main(ATT_WIDTH), prev, main(ATT_KV_WIDTH), nxt, prev, main(ATT_KV_WIDTH), nxt,
                  main(D_MODEL), main(CONV_WIDTH), main(GATE_WIDTH),
                  _resident((RET_WIDTH, D_MODEL)), _resident((CONV_WIDTH, D_MODEL)),
                  _resident((ATT_WIDTH, D_MODEL)), _resident((D_MODEL, D_MODEL))],
        out_specs=main(D_MODEL),
        scratch_shapes=[pltpu.VMEM((RET_HEADS,) + tile, F32),
                        pltpu.VMEM((RET_HEADS * N_TABS, CHUNK, CHUNK), F32)],
        compiler_params=_params("parallel", "arbitrary"),
        name="mixer",
    )(dec, sink, v3(proj["rq"]), v3(proj["qd"]), v3(proj["rk"]), rkt5, v3(proj["rv"]), v3(proj["rvf"]),
      v3(proj["rg"]), sbs, gn_gain,
      v3(proj["aq"]), ak3, ak3, ak3, av3, av3, av3,
      v3(h), v3(proj["yc"]), v3(proj["gs"]), wr, wc, wa, wo)
    return out.reshape(rows, D_MODEL)


def _rope_tables(p, batch):
    pos = jnp.arange(p, dtype=F32) - float(PAD)

    def tab(d):
        inv = ROPE_THETA ** (-jnp.arange(0, d, 2, dtype=F32) / d)
        ang = pos[:, None] * inv[None, :]
        cos, sin = jnp.cos(ang), jnp.sin(ang)
        reps = LANES // d
        c = jnp.tile(jnp.concatenate([cos, cos], axis=1), (batch, reps))
        s = jnp.tile(jnp.concatenate([-sin, sin], axis=1), (batch, reps))
        return c, s

    cr, sr = tab(RET_DIM)
    ca, sa = tab(ATT_DIM)
    return cr, sr, ca, sa


def _pair_heads(t, axis):
    shape = t.shape
    t = t.reshape(shape[:axis] + (ATT_KV_HEADS, ATT_GROUPS, ATT_DIM) + shape[axis + 1:])
    t = jnp.swapaxes(t, axis, axis + 1)
    return t.reshape(shape)


def _prep_weights(w_in, w_attn_out):
    aq = _pair_heads(w_in[:, :, OFF_AQ:OFF_AK].astype(BF16), 2)
    w_in = lax.dynamic_update_slice(w_in.astype(BF16), aq, (0, 0, OFF_AQ))
    return w_in, _pair_heads(w_attn_out, 1).astype(BF16)


def _trunk(x, meta_tokens, layers, final_norm):
    batch, s, _ = x.shape
    p = s + CHUNK
    rows = batch * p
    meta = jnp.broadcast_to(meta_tokens[None], (batch, N_META, D_MODEL))
    h = jnp.concatenate([jnp.zeros((batch, PAD, D_MODEL), F32), meta, x], axis=1).reshape(rows, D_MODEL)
    tabs = _rope_tables(p, batch)
    for idx, ly in enumerate(layers):
        h = _ffn(h, ly["norm_ffn1"], ly["wg1"], ly["wu1"], ly["wd1"])
        proj = _inproj(h, ly["ret_decay"], ly["norm_mix"], ly["w_in"], ly["conv_w"], tabs)
        h = _mixer(h, ly["ret_decay"], ly["attn_sink"], ly["ret_gn_gain"], ly["w_ret_out"],
                   ly["w_conv_out"], ly["w_attn_out"], ly["w_o"], proj, batch)
        if idx + 1 < len(layers):
            h = _ffn(h, ly["norm_ffn2"], ly["wg2"], ly["wu2"], ly["wd2"])
    ly = layers[-1]
    return _ffn_final(h.reshape(batch, p, D_MODEL), ly["norm_ffn2"], ly["wg2"], ly["wu2"], ly["wd2"],
                      final_norm.reshape(1, D_MODEL))


def kernel(x_prompt, x_sample, meta_tokens, norm_ffn1, w_ffn1_gate, w_ffn1_up, w_ffn1_down, norm_mix, w_in, ret_decay, ret_gn_gain, conv_w, attn_sink, w_ret_out, w_conv_out, w_attn_out, w_o, norm_ffn2, w_ffn2_gate, w_ffn2_up, w_ffn2_down, final_norm):
    depth = w_in.shape[0]
    w_in_b, w_attn_out_b = _prep_weights(w_in, w_attn_out)
    layers = []
    for l in range(depth):
        layers.append(dict(
            norm_ffn1=norm_ffn1[l].reshape(1, D_MODEL), wg1=w_ffn1_gate[l].astype(BF16),
            wu1=w_ffn1_up[l].astype(BF16), wd1=w_ffn1_down[l].astype(BF16),
            norm_mix=norm_mix[l].reshape(1, D_MODEL), w_in=w_in_b[l],
            ret_decay=ret_decay[l], ret_gn_gain=ret_gn_gain[l].reshape(1, RET_WIDTH),
            conv_w=conv_w[l], attn_sink=attn_sink[l],
            w_ret_out=w_ret_out[l].astype(BF16), w_conv_out=w_conv_out[l].astype(BF16),
            w_attn_out=w_attn_out_b[l], w_o=w_o[l].astype(BF16),
            norm_ffn2=norm_ffn2[l].reshape(1, D_MODEL), wg2=w_ffn2_gate[l].astype(BF16),
            wu2=w_ffn2_up[l].astype(BF16), wd2=w_ffn2_down[l].astype(BF16)))
    y_prompt = _trunk(x_prompt, meta_tokens, layers, final_norm)
    y_sample = _trunk(x_sample, meta_tokens, layers, final_norm)
    return (y_prompt, y_sample)
```

```python
import functools
import math

import jax
import jax.numpy as jnp
from jax import lax
from jax.experimental import pallas as pl
from jax.experimental.pallas import tpu as pltpu

F32 = jnp.float32
BF16 = jnp.bfloat16

D_MODEL = 1024
D_FF = 2816
N_META = 16
CHUNK = 128
PAD = CHUNK - N_META
RET_HEADS = 4
RET_DIM = 128
RET_WIDTH = RET_HEADS * RET_DIM
CONV_WIDTH = 512
ATT_Q_HEADS = 8
ATT_KV_HEADS = 2
ATT_DIM = 64
ATT_GROUPS = ATT_Q_HEADS // ATT_KV_HEADS
ATT_WIDTH = ATT_Q_HEADS * ATT_DIM
ATT_KV_WIDTH = ATT_KV_HEADS * ATT_DIM
WINDOW = 128
GATE_WIDTH = 3 * D_MODEL
ROPE_THETA = 10000.0
EPS = 1e-6
NEG_INF = -1e30

LANES = 128
MXU_COLS = 256
HALO_ROWS = 16
FF_CHUNK = MXU_COLS
SEQ_CHUNKS = 4
REV_CHUNKS = 2 * SEQ_CHUNKS
VMEM_LIMIT = 56 * 1024 * 1024
LOG2E = math.log2(math.e)
ATT_Q_SCALE = ATT_DIM ** -0.5 * LOG2E

OFF_RQ = 0
OFF_RK = OFF_RQ + RET_WIDTH
OFF_RV = OFF_RK + RET_WIDTH
OFF_RG = OFF_RV + RET_WIDTH
OFF_CB = OFF_RG + RET_WIDTH
OFF_CC = OFF_CB + CONV_WIDTH
OFF_CX = OFF_CC + CONV_WIDTH
OFF_AQ = OFF_CX + CONV_WIDTH
OFF_AK = OFF_AQ + ATT_WIDTH
OFF_AV = OFF_AK + ATT_KV_WIDTH
OFF_GATE = OFF_AV + ATT_KV_WIDTH
IN_WIDTH = OFF_GATE + GATE_WIDTH


def _params(*sem):
    return pltpu.CompilerParams(dimension_semantics=sem, vmem_limit_bytes=VMEM_LIMIT)


def _resident(shape):
    nd = len(shape)
    return pl.BlockSpec(shape, lambda *_: (0,) * nd, pipeline_mode=pl.Buffered(1))


def _row_tile(rows, cap):
    best = CHUNK
    t = CHUNK
    while t <= cap:
        if rows % t == 0:
            best = t
        t += CHUNK
    return best


def _rms(x, gain):
    ms = jnp.mean(x * x, axis=-1, keepdims=True)
    return x * lax.rsqrt(ms + EPS) * gain


def _dot(a, b):
    return jnp.dot(a, b, preferred_element_type=F32)


def _dot_nt(a, b):
    return lax.dot_general(a, b, (((1,), (1,)), ((), ())), preferred_element_type=F32)


def _ffn_body(x, g_ref, wg_ref, wu_ref, wd_ref, acc_ref):
    n = _rms(x, g_ref[...]).astype(BF16)
    for c in range(D_FF // FF_CHUNK):
        sl = slice(c * FF_CHUNK, (c + 1) * FF_CHUNK)
        g = _dot(n, wg_ref[:, sl])
        u = _dot(n, wu_ref[:, sl])
        a = (g * jax.nn.sigmoid(g) * u).astype(BF16)
        d = _dot(a, wd_ref[sl, :])
        if c == 0:
            acc_ref[...] = d
        else:
            acc_ref[...] += d
    return x + 0.5 * acc_ref[...]


def _ffn_kernel(x_ref, g_ref, wg_ref, wu_ref, wd_ref, o_ref, acc_ref):
    o_ref[...] = _ffn_body(x_ref[...], g_ref, wg_ref, wu_ref, wd_ref, acc_ref)


def _ffn_final_kernel(x_ref, g_ref, wg_ref, wu_ref, wd_ref, fg_ref, o_ref, acc_ref):
    o_ref[...] = _rms(_ffn_body(x_ref[0], g_ref, wg_ref, wu_ref, wd_ref, acc_ref), fg_ref[...])


def _ffn_weight_specs():
    return [_resident((1, D_MODEL)), _resident((D_MODEL, D_FF)), _resident((D_MODEL, D_FF)),
            _resident((D_FF, D_MODEL))]


def _ffn(h, gain, wg, wu, wd):
    rows = h.shape[0]
    tm = _row_tile(rows, 1024)
    return pl.pallas_call(
        _ffn_kernel,
        out_shape=jax.ShapeDtypeStruct((rows, D_MODEL), F32),
        grid=(rows // tm,),
        in_specs=[pl.BlockSpec((tm, D_MODEL), lambda i: (i, 0))] + _ffn_weight_specs(),
        out_specs=pl.BlockSpec((tm, D_MODEL), lambda i: (i, 0)),
        scratch_shapes=[pltpu.VMEM((tm, D_MODEL), F32)],
        compiler_params=_params("parallel"),
        name="ffn",
    )(h, gain, wg, wu, wd)


def _ffn_final(h3, gain, wg, wu, wd, final_gain):
    batch, p, _ = h3.shape
    s = p - CHUNK
    tm = _row_tile(s, 1024)
    return pl.pallas_call(
        _ffn_final_kernel,
        out_shape=jax.ShapeDtypeStruct((batch, s, D_MODEL), F32),
        grid=(batch, s // tm),
        in_specs=[pl.BlockSpec((pl.Element(1), pl.Element(tm), pl.Element(D_MODEL)),
                               lambda b, j: (b, pl.multiple_of(CHUNK + j * tm, CHUNK), 0))]
        + _ffn_weight_specs() + [_resident((1, D_MODEL))],
        out_specs=pl.BlockSpec((None, tm, D_MODEL), lambda b, j: (b, j, 0)),
        scratch_shapes=[pltpu.VMEM((tm, D_MODEL), F32)],
        compiler_params=_params("parallel", "parallel"),
        name="ffn_final",
    )(h3, gain, wg, wu, wd, final_gain)


def _rot128(x, cos, sin):
    return x * cos + pltpu.roll(x, 64, axis=1) * sin


def _rot64(x, cos, sin, low_half):
    partner = jnp.where(low_half, pltpu.roll(x, 96, axis=1), pltpu.roll(x, 32, axis=1))
    return x * cos + partner * sin


def _log_gamma_tile(dec_ref, direction, hd):
    return jnp.log1p(-jnp.exp(jnp.full((CHUNK, CHUNK), dec_ref[direction, hd], F32)))


D_QF, D_QB, D_VF, D_VB = range(4)


def _row_decay_tables(dec_ref, tab_ref):
    i = lax.broadcasted_iota(jnp.int32, (CHUNK, CHUNK), 0).astype(F32)
    for hd in range(RET_HEADS):
        sl = slice(hd * RET_DIM, (hd + 1) * RET_DIM)
        lgf = _log_gamma_tile(dec_ref, 0, hd)
        lgb = _log_gamma_tile(dec_ref, 1, hd)
        tab_ref[D_QF, :, sl] = jnp.exp(lgf * (i + 1.0))
        tab_ref[D_QB, :, sl] = jnp.exp(lgb * (CHUNK - i))
        tab_ref[D_VF, :, sl] = jnp.exp(lgf * (CHUNK - 1.0 - i))
        tab_ref[D_VB, :, sl] = jnp.exp(lgb * i)


def _inproj_kernel(dec_ref, x_ref, xp_ref, xn_ref, g_ref, w_ref, cw_ref, cr_ref, sr_ref, ca_ref, sa_ref,
                   rq_ref, qd_ref, rk_ref, rkt_ref, rv_ref, rvf_ref, rvb_ref, rg_ref, yc_ref,
                   aq_ref, ak_ref, av_ref, gs_ref, dtab_ref, us_ref, *, total_rows):
    i = pl.program_id(0)
    tm = x_ref.shape[0]

    @pl.when(i == 0)
    def _():
        _row_decay_tables(dec_ref, dtab_ref)

    gain = g_ref[...]
    n = _rms(x_ref[...], gain).astype(BF16)

    def proj(off, width):
        return _dot(n, w_ref[:, off:off + width])

    cr, sr = cr_ref[...], sr_ref[...]
    ca, sa = ca_ref[...], sa_ref[...]
    low_half = (lax.broadcasted_iota(jnp.int32, (1, LANES), 1) % ATT_DIM) < (ATT_DIM // 2)
    chunks = [slice(ch * CHUNK, (ch + 1) * CHUNK) for ch in range(tm // CHUNK)]
    for c in range(RET_WIDTH // MXU_COLS):
        zq = proj(OFF_RQ + c * MXU_COLS, MXU_COLS)
        zk = proj(OFF_RK + c * MXU_COLS, MXU_COLS)
        for half in range(MXU_COLS // RET_DIM):
            hd = c * (MXU_COLS // RET_DIM) + half
            src = slice(half * RET_DIM, (half + 1) * RET_DIM)
            dst = slice(hd * RET_DIM, (hd + 1) * RET_DIM)
            q = _rot128(zq[:, src], cr, sr)
            k = _rot128(zk[:, src], cr, sr) * (RET_DIM ** -0.5)
            rq_ref[:, dst] = q.astype(BF16)
            rk_ref[:, dst] = k.astype(BF16)
            for ch, rows in enumerate(chunks):
                qd_ref[rows, 2 * hd * RET_DIM:(2 * hd + 1) * RET_DIM] = (q[rows] * dtab_ref[D_QF, :, dst]).astype(BF16)
                qd_ref[rows, (2 * hd + 1) * RET_DIM:(2 * hd + 2) * RET_DIM] = (q[rows] * dtab_ref[D_QB, :, dst]).astype(BF16)
                rkt_ref[ch, hd] = k[rows].T.astype(BF16)
    zv = proj(OFF_RV, RET_WIDTH)
    rv_ref[...] = zv.astype(BF16)
    for rows in chunks:
        rvf_ref[rows, :] = (zv[rows] * dtab_ref[D_VF]).astype(BF16)
        rvb_ref[rows, :] = (zv[rows] * dtab_ref[D_VB]).astype(BF16)
    zg = proj(OFF_RG, RET_WIDTH)
    rg_ref[...] = (zg * jax.nn.sigmoid(zg)).astype(BF16)

    n_ext = jnp.concatenate([_rms(xp_ref[...], gain).astype(BF16), n, _rms(xn_ref[...], gain).astype(BF16)], axis=0)
    u = (_dot(n_ext, w_ref[:, OFF_CC:OFF_CC + CONV_WIDTH]) * _dot(n_ext, w_ref[:, OFF_CX:OFF_CX + CONV_WIDTH]))
    grow = i * tm - HALO_ROWS + lax.broadcasted_iota(jnp.int32, (tm + 2 * HALO_ROWS, 1), 0)
    us_ref[...] = jnp.where((grow >= 0) & (grow < total_rows), u, 0.0)
    conv = (us_ref[pl.ds(HALO_ROWS - 1, tm), :] * cw_ref[0:1, :] + us_ref[pl.ds(HALO_ROWS, tm), :] * cw_ref[1:2, :]
            + us_ref[pl.ds(HALO_ROWS + 1, tm), :] * cw_ref[2:3, :])
    yc_ref[...] = (proj(OFF_CB, CONV_WIDTH) * conv).astype(BF16)

    for c in range(ATT_WIDTH // MXU_COLS):
        zq = proj(OFF_AQ + c * MXU_COLS, MXU_COLS)
        for half in range(MXU_COLS // LANES):
            src = slice(half * LANES, (half + 1) * LANES)
            dst = slice(c * MXU_COLS + half * LANES, c * MXU_COLS + (half + 1) * LANES)
            aq_ref[:, dst] = (_rot64(zq[:, src], ca, sa, low_half) * ATT_Q_SCALE).astype(BF16)
    zkv = proj(OFF_AK, 2 * ATT_KV_WIDTH)
    ak_ref[...] = _rot64(zkv[:, :ATT_KV_WIDTH], ca, sa, low_half).astype(BF16)
    av_ref[...] = zkv[:, ATT_KV_WIDTH:].astype(BF16)
    for c in range(GATE_WIDTH // 512):
        sl = slice(c * 512, (c + 1) * 512)
        gs_ref[:, sl] = jax.nn.sigmoid(proj(OFF_GATE + c * 512, 512)).astype(BF16)


INPROJ_OUT = ("rq", "qd", "rk", "rkt", "rv", "rvf", "rvb", "rg", "yc", "aq", "ak", "av", "gs")


def _inproj(h, dec, gain, w_in, conv_w, tabs):
    rows = h.shape[0]
    tm = _row_tile(rows, 512)
    hb = tm // HALO_ROWS
    row = lambda w: pl.BlockSpec((tm, w), lambda i: (i, 0))
    halo_prev = pl.BlockSpec((HALO_ROWS, D_MODEL), lambda i: (jnp.maximum(i * hb - 1, 0), 0))
    halo_next = pl.BlockSpec((HALO_ROWS, D_MODEL),
                              lambda i: (jnp.minimum((i + 1) * hb, rows // HALO_ROWS - 1), 0))
    widths = dict(rq=RET_WIDTH, qd=2 * RET_WIDTH, rk=RET_WIDTH, rv=RET_WIDTH, rvf=RET_WIDTH, rvb=RET_WIDTH,
                  rg=RET_WIDTH, yc=CONV_WIDTH, aq=ATT_WIDTH, ak=ATT_KV_WIDTH, av=ATT_KV_WIDTH, gs=GATE_WIDTH)
    kt_shape = (rows // CHUNK, RET_HEADS, RET_DIM, RET_DIM)
    kt_spec = pl.BlockSpec((tm // CHUNK,) + kt_shape[1:], lambda i: (i, 0, 0, 0))
    outs = pl.pallas_call(
        functools.partial(_inproj_kernel, total_rows=rows),
        out_shape=[jax.ShapeDtypeStruct(kt_shape if name == "rkt" else (rows, widths[name]), BF16)
                   for name in INPROJ_OUT],
        grid=(rows // tm,),
        in_specs=[pl.BlockSpec(memory_space=pltpu.SMEM), row(D_MODEL), halo_prev, halo_next,
                  _resident((1, D_MODEL)), _resident((D_MODEL, IN_WIDTH)), _resident((3, CONV_WIDTH)),
                  row(LANES), row(LANES), row(LANES), row(LANES)],
        out_specs=[kt_spec if name == "rkt" else row(widths[name]) for name in INPROJ_OUT],
        scratch_shapes=[pltpu.VMEM((4, CHUNK, RET_WIDTH), F32),
                        pltpu.VMEM((tm + 2 * HALO_ROWS, CONV_WIDTH), F32)],
        compiler_params=_params("arbitrary"),
        name="inproj",
    )(dec, h, h, h, gain, w_in, conv_w, *tabs)
    return dict(zip(INPROJ_OUT, outs))


T_CF, T_CB, T_MASK = range(3)
N_TABS = 3


def _chunk_tables(dec_ref, tab_ref):
    i = lax.broadcasted_iota(jnp.int32, (CHUNK, CHUNK), 0).astype(F32)
    j = lax.broadcasted_iota(jnp.int32, (CHUNK, CHUNK), 1).astype(F32)
    diff = i - j
    for hd in range(RET_HEADS):
        lgf = _log_gamma_tile(dec_ref, 0, hd)
        lgb = _log_gamma_tile(dec_ref, 1, hd)
        base = hd * N_TABS
        tab_ref[base + T_CF] = jnp.exp(lgf * CHUNK)
        tab_ref[base + T_CB] = jnp.exp(lgb * CHUNK)
        tab_ref[base + T_MASK] = jnp.where(diff >= 0, jnp.exp(lgf * jnp.maximum(diff, 0.0)),
                                           jnp.exp(lgb * jnp.maximum(-diff, 0.0)))


def _state_update(s_ref, tab_ref, hd, kt, vd, t_c):
    s_ref[hd] = tab_ref[hd * N_TABS + t_c] * s_ref[hd] + _dot(kt, vd)


def _ret_rev_kernel(dec_ref, kt_ref, vb_ref, sbs_ref, sb_ref, tab_ref, *, n_chunks):
    c = pl.program_id(1)
    blk = pl.num_programs(1) - 1 - c

    @pl.when(c == 0)
    def _():
        sb_ref[...] = jnp.zeros_like(sb_ref)
        _chunk_tables(dec_ref, tab_ref)

    for jj in reversed(range(REV_CHUNKS)):
        valid = blk * REV_CHUNKS + jj < n_chunks
        rows = slice(jj * CHUNK, (jj + 1) * CHUNK)
        for hd in range(RET_HEADS):
            sl = slice(hd * RET_DIM, (hd + 1) * RET_DIM)
            sbs_ref[0, 0, jj * RET_HEADS + hd] = sb_ref[hd].astype(BF16)
            kt = kt_ref[0, jj, hd]
            vd = vb_ref[0, rows, sl]
            kt = jnp.where(valid, kt, jnp.zeros_like(kt))
            vd = jnp.where(valid, vd, jnp.zeros_like(vd))
            _state_update(sb_ref, tab_ref, hd, kt, vd, T_CB)


def _ret_rev(dec, rkt5, rvb3):
    batch, p, _ = rvb3.shape
    n = p // CHUNK
    nblk = pl.cdiv(n, REV_CHUNKS)
    tiles = REV_CHUNKS * RET_HEADS
    tile = (RET_DIM, RET_DIM)
    sbs = pl.pallas_call(
        functools.partial(_ret_rev_kernel, n_chunks=n),
        out_shape=jax.ShapeDtypeStruct((batch, nblk, tiles) + tile, BF16),
        grid=(batch, nblk),
        in_specs=[pl.BlockSpec(memory_space=pltpu.SMEM),
                  pl.BlockSpec((1, REV_CHUNKS, RET_HEADS) + tile, lambda b, c: (b, nblk - 1 - c, 0, 0, 0)),
                  pl.BlockSpec((1, REV_CHUNKS * CHUNK, RET_WIDTH), lambda b, c: (b, nblk - 1 - c, 0))],
        out_specs=pl.BlockSpec((1, 1, tiles) + tile, lambda b, c: (b, nblk - 1 - c, 0, 0, 0)),
        scratch_shapes=[pltpu.VMEM((RET_HEADS,) + tile, F32),
                        pltpu.VMEM((RET_HEADS * N_TABS, CHUNK, CHUNK), F32)],
        compiler_params=_params("parallel", "arbitrary"),
        name="ret_rev",
    )(dec, rkt5, rvb3)
    per = REV_CHUNKS // SEQ_CHUNKS
    return sbs.reshape((batch, nblk * per, SEQ_CHUNKS * RET_HEADS) + tile)


def _ret_items(q_ref, qd_ref, k_ref, kt_ref, v_ref, vf_ref, g_ref, sbs_ref, gain_ref, sf_ref, tab_ref, jj, out):
    rows = slice(jj * CHUNK, (jj + 1) * CHUNK)

    def head_item(hd):
        def run():
            sl = slice(hd * RET_DIM, (hd + 1) * RET_DIM)
            base = hd * N_TABS
            att = _dot_nt(q_ref[0, rows, sl], k_ref[0, rows, sl]) * tab_ref[base + T_MASK]
            o = _dot(att.astype(BF16), v_ref[0, rows, sl])
            sd = jnp.concatenate([sf_ref[hd].astype(BF16), sbs_ref[0, 0, jj * RET_HEADS + hd]], axis=0)
            o += _dot(qd_ref[0, rows, 2 * hd * RET_DIM:(2 * hd + 2) * RET_DIM], sd)
            _state_update(sf_ref, tab_ref, hd, kt_ref[0, jj, hd], vf_ref[0, rows, sl], T_CF)
            mu = jnp.mean(o, axis=-1, keepdims=True)
            oc = o - mu
            var = jnp.mean(oc * oc, axis=-1, keepdims=True)
            on = oc * lax.rsqrt(var + EPS) * gain_ref[:, sl]
            out[(jj, hd)] = (g_ref[0, rows, sl].astype(F32) * on).astype(BF16)
        return run

    return [head_item(hd) for hd in range(RET_HEADS)]


def _attn_window(kp_ref, km_ref, kn_ref, vp_ref, vm_ref, vn_ref, blk, seq_rows):
    tq = SEQ_CHUNKS * CHUNK
    span = tq + 2 * CHUNK
    head0 = lax.broadcasted_iota(jnp.int32, (1, LANES), 1) < ATT_DIM
    wpos = blk * tq - CHUNK + lax.broadcasted_iota(jnp.int32, (span, 1), 0)
    inside = (wpos >= 0) & (wpos < seq_rows)
    kall = jnp.concatenate([kp_ref[0], km_ref[0], kn_ref[0]], axis=0)
    vall = jnp.concatenate([vp_ref[0], vm_ref[0], vn_ref[0]], axis=0)
    zero = jnp.zeros_like(kall)
    kall = jnp.where(inside, kall, zero)
    vall = jnp.where(inside, vall, zero)
    return (jnp.where(head0, kall, zero), jnp.where(head0, zero, kall),
            jnp.where(head0, vall, zero), jnp.where(head0, zero, vall))


def _attn_items(sink_ref, q_ref, window, blk, jj, seq_rows, out):
    k0, k1, v0, v1 = window
    base = jj * CHUNK
    st = {}

    def scores():
        r = lax.broadcasted_iota(jnp.int32, (CHUNK, 3 * CHUNK), 0)
        j = lax.broadcasted_iota(jnp.int32, (CHUNK, 3 * CHUNK), 1)
        kpos = (blk * SEQ_CHUNKS + jj - 1) * CHUNK + j
        st["ok"] = (jnp.abs(CHUNK + r - j) <= WINDOW) & (kpos >= PAD) & (kpos < seq_rows)
        kb = jnp.concatenate([k0[base:base + 3 * CHUNK], k1[base:base + 3 * CHUNK]], axis=0)
        qs = jnp.concatenate([q_ref[0, base:base + CHUNK, c * LANES:(c + 1) * LANES]
                              for c in range(ATT_GROUPS)], axis=0)
        st["s"] = _dot_nt(qs, kb)

    def softmax_item(c):
        def run():
            head0 = lax.broadcasted_iota(jnp.int32, (1, LANES), 1) < ATT_DIM
            e_cols, inv_cols = [], []
            for kv in range(ATT_KV_HEADS):
                s = st["s"][c * CHUNK:(c + 1) * CHUNK, kv * 3 * CHUNK:(kv + 1) * 3 * CHUNK]
                s = jnp.where(st["ok"], s, NEG_INF)
                sink = sink_ref[kv * ATT_GROUPS + c] * LOG2E
                m = jnp.maximum(jnp.max(s, axis=-1, keepdims=True), sink)
                e = jnp.exp2(s - m)
                denom = jnp.sum(e, axis=-1, keepdims=True) + jnp.exp2(sink - m)
                e_cols.append(e.astype(BF16))
                inv_cols.append(1.0 / denom)
            st[("e", c)] = jnp.concatenate(e_cols, axis=1)
            st[("inv", c)] = jnp.where(head0, inv_cols[0], inv_cols[1])
        return run

    def values():
        vb = jnp.concatenate([v0[base:base + 3 * CHUNK], v1[base:base + 3 * CHUNK]], axis=0)
        o = _dot(jnp.concatenate([st[("e", c)] for c in range(ATT_GROUPS)], axis=0), vb)
        out[jj] = jnp.concatenate([(o[c * CHUNK:(c + 1) * CHUNK] * st[("inv", c)]).astype(BF16)
                                   for c in range(ATT_GROUPS)], axis=1)

    return [scores] + [softmax_item(c) for c in range(ATT_GROUPS)] + [values]


def _merge_items(yr_ref, yc_ref, ya_ref, gs_ref, h_ref, o_ref, wr_ref, wc_ref, wa_ref, wo_ref, row0):
    st = {}
    n_tiles = D_MODEL // MXU_COLS
    rows = h_ref.shape[1]

    def branch_item(nt, b, y_of, w_ref):
        def run():
            cols = slice(nt * MXU_COLS, (nt + 1) * MXU_COLS)
            gate = gs_ref[0, :, b * D_MODEL + nt * MXU_COLS:b * D_MODEL + (nt + 1) * MXU_COLS]
            term = gate.astype(F32) * _dot(y_of(), w_ref[:, cols])
            acc = term if b == 0 else st[nt] + term
            st[nt] = acc.astype(BF16) if b == len(branches) - 1 else acc
        return run

    def out_item(nt):
        def run():
            cols = slice(nt * MXU_COLS, (nt + 1) * MXU_COLS)
            merged = jnp.concatenate([st[k] for k in range(n_tiles)], axis=1)
            mix = _dot(merged, wo_ref[:, cols])
            pos = row0 + lax.broadcasted_iota(jnp.int32, (rows, 1), 0)
            o_ref[0, :, cols] = h_ref[0, :, cols] + jnp.where(pos < PAD, 0.0, mix)
        return run

    branches = ((lambda: yr_ref[...], wr_ref), (lambda: yc_ref[0], wc_ref), (lambda: ya_ref[...], wa_ref))
    items = [branch_item(nt, b, y_of, w_ref) for nt in range(n_tiles) for b, (y_of, w_ref) in enumerate(branches)]
    return items + [out_item(nt) for nt in range(n_tiles)]


def _interleave(major, minor):
    done = 0
    for idx, item in enumerate(major):
        item()
        want = (idx + 1) * len(minor) // len(major)
        while done < want:
            minor[done]()
            done += 1


def _mixer_kernel(dec_ref, sink_ref, rq_ref, qd_ref, rk_ref, rkt_ref, rv_ref, rvf_ref, rg_ref, sbs_ref, gain_ref,
                  aq_ref, kp_ref, km_ref, kn_ref, vp_ref, vm_ref, vn_ref,
                  h_ref, yc_ref, gs_ref, wr_ref, wc_ref, wa_ref, wo_ref,
                  o_ref, sf_ref, tab_ref, yr_ref, ya_ref, *, seq_rows, n_blocks):
    j = pl.program_id(1)
    tq = SEQ_CHUNKS * CHUNK

    @pl.when(j == 0)
    def _():
        sf_ref[...] = jnp.zeros_like(sf_ref)
        _chunk_tables(dec_ref, tab_ref)

    def seq_items():
        window = _attn_window(kp_ref, km_ref, kn_ref, vp_ref, vm_ref, vn_ref, j, seq_rows)
        yr_parts, ya_parts = {}, {}
        items = []
        for jj in range(SEQ_CHUNKS):
            ret = _ret_items(rq_ref, qd_ref, rk_ref, rkt_ref, rv_ref, rvf_ref, rg_ref, sbs_ref,
                             gain_ref, sf_ref, tab_ref, jj, yr_parts)
            att = _attn_items(sink_ref, aq_ref, window, j, jj, seq_rows, ya_parts)
            items += [att[0], ret[0], att[1], ret[1], att[2], att[3], ret[2], att[4], ret[3], att[5]]

        def park():
            for jj in range(SEQ_CHUNKS):
                rows = slice(jj * CHUNK, (jj + 1) * CHUNK)
                ya_ref[j % 2, rows, :] = ya_parts[jj]
                for hd in range(RET_HEADS):
                    yr_ref[j % 2, rows, hd * RET_DIM:(hd + 1) * RET_DIM] = yr_parts[(jj, hd)]
        return items, park

    def merge_items():
        slot = (j + 1) % 2
        return _merge_items(yr_ref.at[slot], yc_ref, ya_ref.at[slot], gs_ref, h_ref, o_ref,
                            wr_ref, wc_ref, wa_ref, wo_ref, (j - 1) * tq)

    @pl.when(j == 0)
    def _():
        items, park = seq_items()
        for item in items:
            item()
        park()

    @pl.when((j > 0) & (j < n_blocks))
    def _():
        items, park = seq_items()
        _interleave(items, merge_items())
        park()

    @pl.when(j == n_blocks)
    def _():
        for item in merge_items():
            item()


def _mixer(h, dec, sink, gn_gain, wr, wc, wa, wo, proj, batch):
    rows = h.shape[0]
    p = rows // batch
    n = p // CHUNK
    tq = SEQ_CHUNKS * CHUNK
    nblk = pl.cdiv(n, SEQ_CHUNKS)
    v3 = lambda t: t.reshape(batch, p, t.shape[-1])
    tile = (RET_DIM, RET_DIM)
    rkt5 = proj["rkt"].reshape((batch, n, RET_HEADS) + tile)
    ak3, av3 = v3(proj["ak"]), v3(proj["av"])
    sbs = _ret_rev(dec, rkt5, v3(proj["rvb"]))

    smem = pl.BlockSpec(memory_space=pltpu.SMEM)
    seq_blk = lambda j: jnp.minimum(j, nblk - 1)
    seq = lambda w: pl.BlockSpec((1, tq, w), lambda b, j: (b, seq_blk(j), 0))
    lag = lambda w: pl.BlockSpec((1, tq, w), lambda b, j: (b, jnp.maximum(j - 1, 0), 0))
    prev = pl.BlockSpec((1, CHUNK, ATT_KV_WIDTH),
                        lambda b, j: (b, jnp.maximum(seq_blk(j) * SEQ_CHUNKS - 1, 0), 0))
    nxt = pl.BlockSpec((1, CHUNK, ATT_KV_WIDTH),
                       lambda b, j: (b, jnp.minimum((seq_blk(j) + 1) * SEQ_CHUNKS, n - 1), 0))
    tiles = SEQ_CHUNKS * RET_HEADS
    out = pl.pallas_call(
        functools.partial(_mixer_kernel, seq_rows=p, n_blocks=nblk),
        out_shape=jax.ShapeDtypeStruct((batch, p, D_MODEL), F32),
        grid=(batch, nblk + 1),
        in_specs=[smem, smem, seq(RET_WIDTH), seq(2 * RET_WIDTH), seq(RET_WIDTH),
                  pl.BlockSpec((1, SEQ_CHUNKS, RET_HEADS) + tile, lambda b, j: (b, seq_blk(j), 0, 0, 0)),
                  seq(RET_WIDTH), seq(RET_WIDTH), seq(RET_WIDTH),
                  pl.BlockSpec((1, 1, tiles) + tile, lambda b, j: (b, seq_blk(j), 0, 0, 0)),
                  _resident((1, RET_WIDTH)),
                  seq(ATT_WIDTH), prev, seq(ATT_KV_WIDTH), nxt, prev, seq(ATT_KV_WIDTH), nxt,
                  lag(D_MODEL), lag(CONV_WIDTH), lag(GATE_WIDTH),
                  _resident((RET_WIDTH, D_MODEL)), _resident((CONV_WIDTH, D_MODEL)),
                  _resident((ATT_WIDTH, D_MODEL)), _resident((D_MODEL, D_MODEL))],
        out_specs=lag(D_MODEL),
        scratch_shapes=[pltpu.VMEM((RET_HEADS,) + tile, F32),
                        pltpu.VMEM((RET_HEADS * N_TABS, CHUNK, CHUNK), F32),
                        pltpu.VMEM((2, tq, RET_WIDTH), BF16), pltpu.VMEM((2, tq, ATT_WIDTH), BF16)],
        compiler_params=_params("parallel", "arbitrary"),
        name="mixer",
    )(dec, sink, v3(proj["rq"]), v3(proj["qd"]), v3(proj["rk"]), rkt5, v3(proj["rv"]), v3(proj["rvf"]),
      v3(proj["rg"]), sbs, gn_gain,
      v3(proj["aq"]), ak3, ak3, ak3, av3, av3, av3,
      v3(h), v3(proj["yc"]), v3(proj["gs"]), wr, wc, wa, wo)
    return out.reshape(rows, D_MODEL)


def _rope_tables(p, batch):
    pos = jnp.arange(p, dtype=F32) - float(PAD)

    def tab(d):
        inv = ROPE_THETA ** (-jnp.arange(0, d, 2, dtype=F32) / d)
        ang = pos[:, None] * inv[None, :]
        cos, sin = jnp.cos(ang), jnp.sin(ang)
        reps = LANES // d
        c = jnp.tile(jnp.concatenate([cos, cos], axis=1), (batch, reps))
        s = jnp.tile(jnp.concatenate([-sin, sin], axis=1), (batch, reps))
        return c, s

    cr, sr = tab(RET_DIM)
    ca, sa = tab(ATT_DIM)
    return cr, sr, ca, sa


def _pair_heads(t, axis):
    shape = t.shape
    t = t.reshape(shape[:axis] + (ATT_KV_HEADS, ATT_GROUPS, ATT_DIM) + shape[axis + 1:])
    t = jnp.swapaxes(t, axis, axis + 1)
    return t.reshape(shape)


def _prep_weights(w_in, w_attn_out):
    aq = _pair_heads(w_in[:, :, OFF_AQ:OFF_AK].astype(BF16), 2)
    w_in = lax.dynamic_update_slice(w_in.astype(BF16), aq, (0, 0, OFF_AQ))
    return w_in, _pair_heads(w_attn_out, 1).astype(BF16)


def _trunk(x, meta_tokens, layers, final_norm):
    batch, s, _ = x.shape
    p = s + CHUNK
    rows = batch * p
    meta = jnp.broadcast_to(meta_tokens[None], (batch, N_META, D_MODEL))
    h = jnp.concatenate([jnp.zeros((batch, PAD, D_MODEL), F32), meta, x], axis=1).reshape(rows, D_MODEL)
    tabs = _rope_tables(p, batch)
    for idx, ly in enumerate(layers):
        h = _ffn(h, ly["norm_ffn1"], ly["wg1"], ly["wu1"], ly["wd1"])
        proj = _inproj(h, ly["ret_decay"], ly["norm_mix"], ly["w_in"], ly["conv_w"], tabs)
        h = _mixer(h, ly["ret_decay"], ly["attn_sink"], ly["ret_gn_gain"], ly["w_ret_out"],
                   ly["w_conv_out"], ly["w_attn_out"], ly["w_o"], proj, batch)
        if idx + 1 < len(layers):
            h = _ffn(h, ly["norm_ffn2"], ly["wg2"], ly["wu2"], ly["wd2"])
    ly = layers[-1]
    return _ffn_final(h.reshape(batch, p, D_MODEL), ly["norm_ffn2"], ly["wg2"], ly["wu2"], ly["wd2"],
                      final_norm.reshape(1, D_MODEL))


def kernel(x_prompt, x_sample, meta_tokens, norm_ffn1, w_ffn1_gate, w_ffn1_up, w_ffn1_down, norm_mix, w_in, ret_decay, ret_gn_gain, conv_w, attn_sink, w_ret_out, w_conv_out, w_attn_out, w_o, norm_ffn2, w_ffn2_gate, w_ffn2_up, w_ffn2_down, final_norm):
    depth = w_in.shape[0]
    w_in_b, w_attn_out_b = _prep_weights(w_in, w_attn_out)
    layers = []
    for l in range(depth):
        layers.append(dict(
            norm_ffn1=norm_ffn1[l].reshape(1, D_MODEL), wg1=w_ffn1_gate[l].astype(BF16),
            wu1=w_ffn1_up[l].astype(BF16), wd1=w_ffn1_down[l].astype(BF16),
            norm_mix=norm_mix[l].reshape(1, D_MODEL), w_in=w_in_b[l],
            ret_decay=ret_decay[l], ret_gn_gain=ret_gn_gain[l].reshape(1, RET_WIDTH),
            conv_w=conv_w[l], attn_sink=attn_sink[l],
            w_ret_out=w_ret_out[l].astype(BF16), w_conv_out=w_conv_out[l].astype(BF16),
            w_attn_out=w_attn_out_b[l], w_o=w_o[l].astype(BF16),
            norm_ffn2=norm_ffn2[l].reshape(1, D_MODEL), wg2=w_ffn2_gate[l].astype(BF16),
            wu2=w_ffn2_up[l].astype(BF16), wd2=w_ffn2_down[l].astype(BF16)))
    y_prompt = _trunk(x_prompt, meta_tokens, layers, final_norm)
    y_sample = _trunk(x_sample, meta_tokens, layers, final_norm)
    return (y_prompt, y_sample)
```

```python
import functools
import math

import jax
import jax.numpy as jnp
from jax import lax
from jax.experimental import pallas as pl
from jax.experimental.pallas import tpu as pltpu

F32 = jnp.float32
BF16 = jnp.bfloat16

D_MODEL = 1024
D_FF = 2816
N_META = 16
CHUNK = 128
PAD = CHUNK - N_META
RET_HEADS = 4
RET_DIM = 128
RET_WIDTH = RET_HEADS * RET_DIM
CONV_WIDTH = 512
ATT_Q_HEADS = 8
ATT_KV_HEADS = 2
ATT_DIM = 64
ATT_GROUPS = ATT_Q_HEADS // ATT_KV_HEADS
ATT_WIDTH = ATT_Q_HEADS * ATT_DIM
ATT_KV_WIDTH = ATT_KV_HEADS * ATT_DIM
WINDOW = 128
GATE_WIDTH = 3 * D_MODEL
ROPE_THETA = 10000.0
EPS = 1e-6
NEG_INF = -1e30

LANES = 128
MXU_COLS = 256
HALO_ROWS = 16
FF_CHUNK = MXU_COLS
SEQ_CHUNKS = 4
REV_CHUNKS = 2 * SEQ_CHUNKS
VMEM_LIMIT = 56 * 1024 * 1024
LOG2E = math.log2(math.e)
ATT_Q_SCALE = ATT_DIM ** -0.5 * LOG2E

OFF_RQ = 0
OFF_RK = OFF_RQ + RET_WIDTH
OFF_RV = OFF_RK + RET_WIDTH
OFF_RG = OFF_RV + RET_WIDTH
OFF_CB = OFF_RG + RET_WIDTH
OFF_CC = OFF_CB + CONV_WIDTH
OFF_CX = OFF_CC + CONV_WIDTH
OFF_AQ = OFF_CX + CONV_WIDTH
OFF_AK = OFF_AQ + ATT_WIDTH
OFF_AV = OFF_AK + ATT_KV_WIDTH
OFF_GATE = OFF_AV + ATT_KV_WIDTH
IN_WIDTH = OFF_GATE + GATE_WIDTH


def _params(*sem):
    return pltpu.CompilerParams(dimension_semantics=sem, vmem_limit_bytes=VMEM_LIMIT)


def _resident(shape, layer=None):
    nd = len(shape)
    if layer is None:
        return pl.BlockSpec(shape, lambda *_: (0,) * nd, pipeline_mode=pl.Buffered(1))
    return pl.BlockSpec((None,) + tuple(shape), lambda *_: (layer,) + (0,) * nd, pipeline_mode=pl.Buffered(1))


def _row_tile(rows, cap):
    best = CHUNK
    t = CHUNK
    while t <= cap:
        if rows % t == 0:
            best = t
        t += CHUNK
    return best


def _rms(x, gain):
    ms = jnp.mean(x * x, axis=-1, keepdims=True)
    return x * lax.rsqrt(ms + EPS) * gain


def _dot(a, b):
    return jnp.dot(a, b, preferred_element_type=F32)


def _dot_nt(a, b):
    return lax.dot_general(a, b, (((1,), (1,)), ((), ())), preferred_element_type=F32)


def _ffn_body(x, g_ref, wg_ref, wu_ref, wd_ref, acc_ref):
    n = _rms(x, g_ref[...]).astype(BF16)
    for c in range(D_FF // FF_CHUNK):
        sl = slice(c * FF_CHUNK, (c + 1) * FF_CHUNK)
        g = _dot(n, wg_ref[:, sl])
        u = _dot(n, wu_ref[:, sl])
        a = (g * jax.nn.sigmoid(g) * u).astype(BF16)
        d = _dot(a, wd_ref[sl, :])
        if c == 0:
            acc_ref[...] = d
        else:
            acc_ref[...] += d
    return x + 0.5 * acc_ref[...]


def _ffn_kernel(x_ref, g_ref, wg_ref, wu_ref, wd_ref, o_ref, acc_ref):
    o_ref[...] = _ffn_body(x_ref[...], g_ref, wg_ref, wu_ref, wd_ref, acc_ref)


def _ffn_final_kernel(x_ref, g_ref, wg_ref, wu_ref, wd_ref, fg_ref, o_ref, acc_ref):
    o_ref[...] = _rms(_ffn_body(x_ref[0], g_ref, wg_ref, wu_ref, wd_ref, acc_ref), fg_ref[...])


def _ffn_weight_specs(layer):
    return [_resident((1, D_MODEL), layer), _resident((D_MODEL, D_FF), layer), _resident((D_MODEL, D_FF), layer),
            _resident((D_FF, D_MODEL), layer)]


def _ffn(h, layer, gain, wg, wu, wd):
    rows = h.shape[0]
    tm = _row_tile(rows, 1024)
    return pl.pallas_call(
        _ffn_kernel,
        out_shape=jax.ShapeDtypeStruct((rows, D_MODEL), F32),
        grid=(rows // tm,),
        in_specs=[pl.BlockSpec((tm, D_MODEL), lambda i: (i, 0))] + _ffn_weight_specs(layer),
        out_specs=pl.BlockSpec((tm, D_MODEL), lambda i: (i, 0)),
        scratch_shapes=[pltpu.VMEM((tm, D_MODEL), F32)],
        compiler_params=_params("parallel"),
        name="ffn",
    )(h, gain, wg, wu, wd)


def _ffn_final(h3, layer, gain, wg, wu, wd, final_gain):
    batch, p, _ = h3.shape
    s = p - CHUNK
    tm = _row_tile(s, 1024)
    return pl.pallas_call(
        _ffn_final_kernel,
        out_shape=jax.ShapeDtypeStruct((batch, s, D_MODEL), F32),
        grid=(batch, s // tm),
        in_specs=[pl.BlockSpec((pl.Element(1), pl.Element(tm), pl.Element(D_MODEL)),
                               lambda b, j: (b, pl.multiple_of(CHUNK + j * tm, CHUNK), 0))]
        + _ffn_weight_specs(layer) + [_resident((1, D_MODEL))],
        out_specs=pl.BlockSpec((None, tm, D_MODEL), lambda b, j: (b, j, 0)),
        scratch_shapes=[pltpu.VMEM((tm, D_MODEL), F32)],
        compiler_params=_params("parallel", "parallel"),
        name="ffn_final",
    )(h3, gain, wg, wu, wd, final_gain)


def _rot128(x, cos, sin):
    return x * cos + pltpu.roll(x, 64, axis=1) * sin


def _rot64(x, cos, sin, low_half):
    partner = jnp.where(low_half, pltpu.roll(x, 96, axis=1), pltpu.roll(x, 32, axis=1))
    return x * cos + partner * sin


def _log_gamma_tile(dec_ref, direction, hd):
    return jnp.log1p(-jnp.exp(jnp.full((CHUNK, CHUNK), dec_ref[direction, hd], F32)))


D_QF, D_QB, D_VF, D_VB = range(4)


def _row_decay_tables(dec_ref, tab_ref):
    i = lax.broadcasted_iota(jnp.int32, (CHUNK, CHUNK), 0).astype(F32)
    for hd in range(RET_HEADS):
        sl = slice(hd * RET_DIM, (hd + 1) * RET_DIM)
        lgf = _log_gamma_tile(dec_ref, 0, hd)
        lgb = _log_gamma_tile(dec_ref, 1, hd)
        tab_ref[D_QF, :, sl] = jnp.exp(lgf * (i + 1.0))
        tab_ref[D_QB, :, sl] = jnp.exp(lgb * (CHUNK - i))
        tab_ref[D_VF, :, sl] = jnp.exp(lgf * (CHUNK - 1.0 - i))
        tab_ref[D_VB, :, sl] = jnp.exp(lgb * i)


def _inproj_kernel(dec_ref, x_ref, xp_ref, xn_ref, g_ref, w_ref, cw_ref, cr_ref, sr_ref, ca_ref, sa_ref,
                   rq_ref, qd_ref, rkt_ref, rv_ref, rvf_ref, rvb_ref, rg_ref, yc_ref,
                   aq_ref, ak_ref, av_ref, gs_ref, dtab_ref, us_ref, *, total_rows):
    i = pl.program_id(0)
    tm = x_ref.shape[0]

    @pl.when(i == 0)
    def _():
        _row_decay_tables(dec_ref, dtab_ref)

    gain = g_ref[...]
    n = _rms(x_ref[...], gain).astype(BF16)

    def proj(off, width):
        return _dot(n, w_ref[:, off:off + width])

    cr, sr = cr_ref[...], sr_ref[...]
    ca, sa = ca_ref[...], sa_ref[...]
    low_half = (lax.broadcasted_iota(jnp.int32, (1, LANES), 1) % ATT_DIM) < (ATT_DIM // 2)
    chunks = [slice(ch * CHUNK, (ch + 1) * CHUNK) for ch in range(tm // CHUNK)]
    for c in range(RET_WIDTH // MXU_COLS):
        zq = proj(OFF_RQ + c * MXU_COLS, MXU_COLS)
        zk = proj(OFF_RK + c * MXU_COLS, MXU_COLS)
        for half in range(MXU_COLS // RET_DIM):
            hd = c * (MXU_COLS // RET_DIM) + half
            src = slice(half * RET_DIM, (half + 1) * RET_DIM)
            dst = slice(hd * RET_DIM, (hd + 1) * RET_DIM)
            q = _rot128(zq[:, src], cr, sr)
            k = _rot128(zk[:, src], cr, sr) * (RET_DIM ** -0.5)
            rq_ref[:, dst] = q.astype(BF16)
            for ch, rows in enumerate(chunks):
                qd_ref[rows, 2 * hd * RET_DIM:(2 * hd + 1) * RET_DIM] = (q[rows] * dtab_ref[D_QF, :, dst]).astype(BF16)
                qd_ref[rows, (2 * hd + 1) * RET_DIM:(2 * hd + 2) * RET_DIM] = (q[rows] * dtab_ref[D_QB, :, dst]).astype(BF16)
                rkt_ref[ch, hd] = k[rows].T.astype(BF16)
    zv = proj(OFF_RV, RET_WIDTH)
    rv_ref[...] = zv.astype(BF16)
    for rows in chunks:
        rvf_ref[rows, :] = (zv[rows] * dtab_ref[D_VF]).astype(BF16)
        rvb_ref[rows, :] = (zv[rows] * dtab_ref[D_VB]).astype(BF16)
    zg = proj(OFF_RG, RET_WIDTH)
    rg_ref[...] = (zg * jax.nn.sigmoid(zg)).astype(BF16)

    n_ext = jnp.concatenate([_rms(xp_ref[...], gain).astype(BF16), n, _rms(xn_ref[...], gain).astype(BF16)], axis=0)
    u = (_dot(n_ext, w_ref[:, OFF_CC:OFF_CC + CONV_WIDTH]) * _dot(n_ext, w_ref[:, OFF_CX:OFF_CX + CONV_WIDTH]))
    grow = i * tm - HALO_ROWS + lax.broadcasted_iota(jnp.int32, (tm + 2 * HALO_ROWS, 1), 0)
    us_ref[...] = jnp.where((grow >= 0) & (grow < total_rows), u, 0.0)
    conv = (us_ref[pl.ds(HALO_ROWS - 1, tm), :] * cw_ref[0:1, :] + us_ref[pl.ds(HALO_ROWS, tm), :] * cw_ref[1:2, :]
            + us_ref[pl.ds(HALO_ROWS + 1, tm), :] * cw_ref[2:3, :])
    yc_ref[...] = (proj(OFF_CB, CONV_WIDTH) * conv).astype(BF16)

    for c in range(ATT_WIDTH // MXU_COLS):
        zq = proj(OFF_AQ + c * MXU_COLS, MXU_COLS)
        for half in range(MXU_COLS // LANES):
            src = slice(half * LANES, (half + 1) * LANES)
            dst = slice(c * MXU_COLS + half * LANES, c * MXU_COLS + (half + 1) * LANES)
            aq_ref[:, dst] = (_rot64(zq[:, src], ca, sa, low_half) * ATT_Q_SCALE).astype(BF16)
    zkv = proj(OFF_AK, 2 * ATT_KV_WIDTH)
    ak_ref[...] = _rot64(zkv[:, :ATT_KV_WIDTH], ca, sa, low_half).astype(BF16)
    av_ref[...] = zkv[:, ATT_KV_WIDTH:].astype(BF16)
    for c in range(GATE_WIDTH // 512):
        sl = slice(c * 512, (c + 1) * 512)
        gs_ref[:, sl] = jax.nn.sigmoid(proj(OFF_GATE + c * 512, 512)).astype(BF16)


INPROJ_OUT = ("rq", "qd", "rkt", "rv", "rvf", "rvb", "rg", "yc", "aq", "ak", "av", "gs")


def _inproj(h, layer, dec, gain, w_in, conv_w, tabs):
    rows = h.shape[0]
    tm = _row_tile(rows, 512)
    hb = tm // HALO_ROWS
    row = lambda w: pl.BlockSpec((tm, w), lambda i: (i, 0))
    halo_prev = pl.BlockSpec((HALO_ROWS, D_MODEL), lambda i: (jnp.maximum(i * hb - 1, 0), 0))
    halo_next = pl.BlockSpec((HALO_ROWS, D_MODEL),
                              lambda i: (jnp.minimum((i + 1) * hb, rows // HALO_ROWS - 1), 0))
    widths = dict(rq=RET_WIDTH, qd=2 * RET_WIDTH, rv=RET_WIDTH, rvf=RET_WIDTH, rvb=RET_WIDTH,
                  rg=RET_WIDTH, yc=CONV_WIDTH, aq=ATT_WIDTH, ak=ATT_KV_WIDTH, av=ATT_KV_WIDTH, gs=GATE_WIDTH)
    kt_shape = (rows // CHUNK, RET_HEADS, RET_DIM, RET_DIM)
    kt_spec = pl.BlockSpec((tm // CHUNK,) + kt_shape[1:], lambda i: (i, 0, 0, 0))
    outs = pl.pallas_call(
        functools.partial(_inproj_kernel, total_rows=rows),
        out_shape=[jax.ShapeDtypeStruct(kt_shape if name == "rkt" else (rows, widths[name]), BF16)
                   for name in INPROJ_OUT],
        grid=(rows // tm,),
        in_specs=[pl.BlockSpec(memory_space=pltpu.SMEM), row(D_MODEL), halo_prev, halo_next,
                  _resident((1, D_MODEL), layer), _resident((D_MODEL, IN_WIDTH), layer),
                  _resident((3, CONV_WIDTH), layer), row(LANES), row(LANES), row(LANES), row(LANES)],
        out_specs=[kt_spec if name == "rkt" else row(widths[name]) for name in INPROJ_OUT],
        scratch_shapes=[pltpu.VMEM((4, CHUNK, RET_WIDTH), F32),
                        pltpu.VMEM((tm + 2 * HALO_ROWS, CONV_WIDTH), F32)],
        compiler_params=_params("arbitrary"),
        name="inproj",
    )(dec, h, h, h, gain, w_in, conv_w, *tabs)
    return dict(zip(INPROJ_OUT, outs))


T_CF, T_CB, T_MASK = range(3)
N_TABS = 3


def _chunk_tables(dec_ref, tab_ref):
    i = lax.broadcasted_iota(jnp.int32, (CHUNK, CHUNK), 0).astype(F32)
    j = lax.broadcasted_iota(jnp.int32, (CHUNK, CHUNK), 1).astype(F32)
    diff = i - j
    for hd in range(RET_HEADS):
        lgf = _log_gamma_tile(dec_ref, 0, hd)
        lgb = _log_gamma_tile(dec_ref, 1, hd)
        base = hd * N_TABS
        tab_ref[base + T_CF] = jnp.exp(lgf * CHUNK)
        tab_ref[base + T_CB] = jnp.exp(lgb * CHUNK)
        tab_ref[base + T_MASK] = jnp.where(diff >= 0, jnp.exp(lgf * jnp.maximum(diff, 0.0)),
                                           jnp.exp(lgb * jnp.maximum(-diff, 0.0)))


def _state_update(s_ref, tab_ref, hd, kt, vd, t_c):
    s_ref[hd] = tab_ref[hd * N_TABS + t_c] * s_ref[hd] + _dot(kt, vd)


def _ret_rev_kernel(dec_ref, kt_ref, vb_ref, sbs_ref, sb_ref, tab_ref, *, n_chunks):
    c = pl.program_id(1)
    blk = pl.num_programs(1) - 1 - c

    @pl.when(c == 0)
    def _():
        sb_ref[...] = jnp.zeros_like(sb_ref)
        _chunk_tables(dec_ref, tab_ref)

    for jj in reversed(range(REV_CHUNKS)):
        valid = blk * REV_CHUNKS + jj < n_chunks
        rows = slice(jj * CHUNK, (jj + 1) * CHUNK)
        for hd in range(RET_HEADS):
            sl = slice(hd * RET_DIM, (hd + 1) * RET_DIM)
            sbs_ref[0, 0, jj * RET_HEADS + hd] = sb_ref[hd].astype(BF16)
            kt = kt_ref[0, jj, hd]
            vd = vb_ref[0, rows, sl]
            kt = jnp.where(valid, kt, jnp.zeros_like(kt))
            vd = jnp.where(valid, vd, jnp.zeros_like(vd))
            _state_update(sb_ref, tab_ref, hd, kt, vd, T_CB)


def _ret_rev(dec, rkt5, rvb3):
    batch, p, _ = rvb3.shape
    n = p // CHUNK
    nblk = pl.cdiv(n, REV_CHUNKS)
    tiles = REV_CHUNKS * RET_HEADS
    tile = (RET_DIM, RET_DIM)
    sbs = pl.pallas_call(
        functools.partial(_ret_rev_kernel, n_chunks=n),
        out_shape=jax.ShapeDtypeStruct((batch, nblk, tiles) + tile, BF16),
        grid=(batch, nblk),
        in_specs=[pl.BlockSpec(memory_space=pltpu.SMEM),
                  pl.BlockSpec((1, REV_CHUNKS, RET_HEADS) + tile, lambda b, c: (b, nblk - 1 - c, 0, 0, 0)),
                  pl.BlockSpec((1, REV_CHUNKS * CHUNK, RET_WIDTH), lambda b, c: (b, nblk - 1 - c, 0))],
        out_specs=pl.BlockSpec((1, 1, tiles) + tile, lambda b, c: (b, nblk - 1 - c, 0, 0, 0)),
        scratch_shapes=[pltpu.VMEM((RET_HEADS,) + tile, F32),
                        pltpu.VMEM((RET_HEADS * N_TABS, CHUNK, CHUNK), F32)],
        compiler_params=_params("parallel", "arbitrary"),
        name="ret_rev",
    )(dec, rkt5, rvb3)
    per = REV_CHUNKS // SEQ_CHUNKS
    return sbs.reshape((batch, nblk * per, SEQ_CHUNKS * RET_HEADS) + tile)


def _ret_items(q_ref, qd_ref, kt_ref, v_ref, vf_ref, g_ref, sbs_ref, gain_ref, sf_ref, tab_ref, jj, out):
    rows = slice(jj * CHUNK, (jj + 1) * CHUNK)

    def head_item(hd):
        def run():
            sl = slice(hd * RET_DIM, (hd + 1) * RET_DIM)
            kt = kt_ref[0, jj, hd]
            att = _dot(q_ref[0, rows, sl], kt) * tab_ref[hd * N_TABS + T_MASK]
            o = _dot(att.astype(BF16), v_ref[0, rows, sl])
            sd = jnp.concatenate([sf_ref[hd].astype(BF16), sbs_ref[0, 0, jj * RET_HEADS + hd]], axis=0)
            o += _dot(qd_ref[0, rows, 2 * hd * RET_DIM:(2 * hd + 2) * RET_DIM], sd)
            _state_update(sf_ref, tab_ref, hd, kt, vf_ref[0, rows, sl], T_CF)
            mu = jnp.mean(o, axis=-1, keepdims=True)
            oc = o - mu
            var = jnp.mean(oc * oc, axis=-1, keepdims=True)
            on = oc * lax.rsqrt(var + EPS) * gain_ref[:, sl]
            out[(jj, hd)] = (g_ref[0, rows, sl].astype(F32) * on).astype(BF16)
        return run

    return [head_item(hd) for hd in range(RET_HEADS)]


def _attn_window(kp_ref, km_ref, kn_ref, vp_ref, vm_ref, vn_ref, blk, seq_rows):
    tq = SEQ_CHUNKS * CHUNK
    span = tq + 2 * CHUNK
    head0 = lax.broadcasted_iota(jnp.int32, (1, LANES), 1) < ATT_DIM
    wpos = blk * tq - CHUNK + lax.broadcasted_iota(jnp.int32, (span, 1), 0)
    inside = (wpos >= 0) & (wpos < seq_rows)
    kall = jnp.concatenate([kp_ref[0], km_ref[0], kn_ref[0]], axis=0)
    vall = jnp.concatenate([vp_ref[0], vm_ref[0], vn_ref[0]], axis=0)
    zero = jnp.zeros_like(kall)
    kall = jnp.where(inside, kall, zero)
    vall = jnp.where(inside, vall, zero)
    return (jnp.where(head0, kall, zero), jnp.where(head0, zero, kall),
            jnp.where(head0, vall, zero), jnp.where(head0, zero, vall))


def _attn_items(sink_ref, q_ref, window, blk, jj, seq_rows, out):
    k0, k1, v0, v1 = window
    base = jj * CHUNK
    st = {}

    def scores():
        r = lax.broadcasted_iota(jnp.int32, (CHUNK, 3 * CHUNK), 0)
        j = lax.broadcasted_iota(jnp.int32, (CHUNK, 3 * CHUNK), 1)
        kpos = (blk * SEQ_CHUNKS + jj - 1) * CHUNK + j
        st["ok"] = (jnp.abs(CHUNK + r - j) <= WINDOW) & (kpos >= PAD) & (kpos < seq_rows)
        kb = jnp.concatenate([k0[base:base + 3 * CHUNK], k1[base:base + 3 * CHUNK]], axis=0)
        qs = jnp.concatenate([q_ref[0, base:base + CHUNK, c * LANES:(c + 1) * LANES]
                              for c in range(ATT_GROUPS)], axis=0)
        st["s"] = _dot_nt(qs, kb)

    def softmax_item(c):
        def run():
            head0 = lax.broadcasted_iota(jnp.int32, (1, LANES), 1) < ATT_DIM
            e_cols, inv_cols = [], []
            for kv in range(ATT_KV_HEADS):
                s = st["s"][c * CHUNK:(c + 1) * CHUNK, kv * 3 * CHUNK:(kv + 1) * 3 * CHUNK]
                s = jnp.where(st["ok"], s, NEG_INF)
                sink = sink_ref[kv * ATT_GROUPS + c] * LOG2E
                m = jnp.maximum(jnp.max(s, axis=-1, keepdims=True), sink)
                e = jnp.exp2(s - m)
                denom = jnp.sum(e, axis=-1, keepdims=True) + jnp.exp2(sink - m)
                e_cols.append(e.astype(BF16))
                inv_cols.append(1.0 / denom)
            st[("e", c)] = jnp.concatenate(e_cols, axis=1)
            st[("inv", c)] = jnp.where(head0, inv_cols[0], inv_cols[1])
        return run

    def values():
        vb = jnp.concatenate([v0[base:base + 3 * CHUNK], v1[base:base + 3 * CHUNK]], axis=0)
        o = _dot(jnp.concatenate([st[("e", c)] for c in range(ATT_GROUPS)], axis=0), vb)
        out[jj] = jnp.concatenate([(o[c * CHUNK:(c + 1) * CHUNK] * st[("inv", c)]).astype(BF16)
                                   for c in range(ATT_GROUPS)], axis=1)

    return [scores] + [softmax_item(c) for c in range(ATT_GROUPS)] + [values]


def _merge_items(yr_ref, yc_ref, ya_ref, gs_ref, h_ref, o_ref, wr_ref, wc_ref, wa_ref, wo_ref, row0):
    st = {}
    n_tiles = D_MODEL // MXU_COLS
    rows = h_ref.shape[1]

    def branch_item(nt, b, y_of, w_ref):
        def run():
            cols = slice(nt * MXU_COLS, (nt + 1) * MXU_COLS)
            gate = gs_ref[0, :, b * D_MODEL + nt * MXU_COLS:b * D_MODEL + (nt + 1) * MXU_COLS]
            term = gate.astype(F32) * _dot(y_of(), w_ref[:, cols])
            acc = term if b == 0 else st[nt] + term
            st[nt] = acc.astype(BF16) if b == len(branches) - 1 else acc
        return run

    def out_item(nt):
        def run():
            cols = slice(nt * MXU_COLS, (nt + 1) * MXU_COLS)
            merged = jnp.concatenate([st[k] for k in range(n_tiles)], axis=1)
            mix = _dot(merged, wo_ref[:, cols])
            pos = row0 + lax.broadcasted_iota(jnp.int32, (rows, 1), 0)
            o_ref[0, :, cols] = h_ref[0, :, cols] + jnp.where(pos < PAD, 0.0, mix)
        return run

    branches = ((lambda: yr_ref[...], wr_ref), (lambda: yc_ref[0], wc_ref), (lambda: ya_ref[...], wa_ref))
    items = [branch_item(nt, b, y_of, w_ref) for nt in range(n_tiles) for b, (y_of, w_ref) in enumerate(branches)]
    return items + [out_item(nt) for nt in range(n_tiles)]


def _interleave(major, minor):
    done = 0
    for idx, item in enumerate(major):
        item()
        want = (idx + 1) * len(minor) // len(major)
        while done < want:
            minor[done]()
            done += 1


def _mixer_kernel(dec_ref, sink_ref, rq_ref, qd_ref, rkt_ref, rv_ref, rvf_ref, rg_ref, sbs_ref, gain_ref,
                  aq_ref, kp_ref, km_ref, kn_ref, vp_ref, vm_ref, vn_ref,
                  h_ref, yc_ref, gs_ref, wr_ref, wc_ref, wa_ref, wo_ref,
                  o_ref, sf_ref, tab_ref, yr_ref, ya_ref, *, seq_rows, n_blocks):
    j = pl.program_id(1)
    tq = SEQ_CHUNKS * CHUNK

    @pl.when(j == 0)
    def _():
        sf_ref[...] = jnp.zeros_like(sf_ref)
        _chunk_tables(dec_ref, tab_ref)

    def seq_items():
        window = _attn_window(kp_ref, km_ref, kn_ref, vp_ref, vm_ref, vn_ref, j, seq_rows)
        yr_parts, ya_parts = {}, {}
        items = []
        for jj in range(SEQ_CHUNKS):
            ret = _ret_items(rq_ref, qd_ref, rkt_ref, rv_ref, rvf_ref, rg_ref, sbs_ref,
                             gain_ref, sf_ref, tab_ref, jj, yr_parts)
            att = _attn_items(sink_ref, aq_ref, window, j, jj, seq_rows, ya_parts)
            items += [att[0], ret[0], att[1], ret[1], att[2], att[3], ret[2], att[4], ret[3], att[5]]

        def park():
            for jj in range(SEQ_CHUNKS):
                rows = slice(jj * CHUNK, (jj + 1) * CHUNK)
                ya_ref[j % 2, rows, :] = ya_parts[jj]
                for hd in range(RET_HEADS):
                    yr_ref[j % 2, rows, hd * RET_DIM:(hd + 1) * RET_DIM] = yr_parts[(jj, hd)]
        return items, park

    def merge_items():
        slot = (j + 1) % 2
        return _merge_items(yr_ref.at[slot], yc_ref, ya_ref.at[slot], gs_ref, h_ref, o_ref,
                            wr_ref, wc_ref, wa_ref, wo_ref, (j - 1) * tq)

    @pl.when(j == 0)
    def _():
        items, park = seq_items()
        for item in items:
            item()
        park()

    @pl.when((j > 0) & (j < n_blocks))
    def _():
        items, park = seq_items()
        _interleave(items, merge_items())
        park()

    @pl.when(j == n_blocks)
    def _():
        for item in merge_items():
            item()


def _mixer(h, layer, dec, sink, gn_gain, wr, wc, wa, wo, proj, batch):
    rows = h.shape[0]
    p = rows // batch
    n = p // CHUNK
    tq = SEQ_CHUNKS * CHUNK
    nblk = pl.cdiv(n, SEQ_CHUNKS)
    v3 = lambda t: t.reshape(batch, p, t.shape[-1])
    tile = (RET_DIM, RET_DIM)
    rkt5 = proj["rkt"].reshape((batch, n, RET_HEADS) + tile)
    ak3, av3 = v3(proj["ak"]), v3(proj["av"])
    sbs = _ret_rev(dec, rkt5, v3(proj["rvb"]))

    smem = pl.BlockSpec(memory_space=pltpu.SMEM)
    seq_blk = lambda j: jnp.minimum(j, nblk - 1)
    seq = lambda w: pl.BlockSpec((1, tq, w), lambda b, j: (b, seq_blk(j), 0))
    lag = lambda w: pl.BlockSpec((1, tq, w), lambda b, j: (b, jnp.maximum(j - 1, 0), 0))
    prev = pl.BlockSpec((1, CHUNK, ATT_KV_WIDTH),
                        lambda b, j: (b, jnp.maximum(seq_blk(j) * SEQ_CHUNKS - 1, 0), 0))
    nxt = pl.BlockSpec((1, CHUNK, ATT_KV_WIDTH),
                       lambda b, j: (b, jnp.minimum((seq_blk(j) + 1) * SEQ_CHUNKS, n - 1), 0))
    tiles = SEQ_CHUNKS * RET_HEADS
    out = pl.pallas_call(
        functools.partial(_mixer_kernel, seq_rows=p, n_blocks=nblk),
        out_shape=jax.ShapeDtypeStruct((batch, p, D_MODEL), F32),
        grid=(batch, nblk + 1),
        in_specs=[smem, smem, seq(RET_WIDTH), seq(2 * RET_WIDTH),
                  pl.BlockSpec((1, SEQ_CHUNKS, RET_HEADS) + tile, lambda b, j: (b, seq_blk(j), 0, 0, 0)),
                  seq(RET_WIDTH), seq(RET_WIDTH), seq(RET_WIDTH),
                  pl.BlockSpec((1, 1, tiles) + tile, lambda b, j: (b, seq_blk(j), 0, 0, 0)),
                  _resident((1, RET_WIDTH), layer),
                  seq(ATT_WIDTH), prev, seq(ATT_KV_WIDTH), nxt, prev, seq(ATT_KV_WIDTH), nxt,
                  lag(D_MODEL), lag(CONV_WIDTH), lag(GATE_WIDTH),
                  _resident((RET_WIDTH, D_MODEL), layer), _resident((CONV_WIDTH, D_MODEL), layer),
                  _resident((ATT_WIDTH, D_MODEL), layer), _resident((D_MODEL, D_MODEL), layer)],
        out_specs=lag(D_MODEL),
        scratch_shapes=[pltpu.VMEM((RET_HEADS,) + tile, F32),
                        pltpu.VMEM((RET_HEADS * N_TABS, CHUNK, CHUNK), F32),
                        pltpu.VMEM((2, tq, RET_WIDTH), BF16), pltpu.VMEM((2, tq, ATT_WIDTH), BF16)],
        compiler_params=_params("parallel", "arbitrary"),
        name="mixer",
    )(dec, sink, v3(proj["rq"]), v3(proj["qd"]), rkt5, v3(proj["rv"]), v3(proj["rvf"]),
      v3(proj["rg"]), sbs, gn_gain,
      v3(proj["aq"]), ak3, ak3, ak3, av3, av3, av3,
      v3(h), v3(proj["yc"]), v3(proj["gs"]), wr, wc, wa, wo)
    return out.reshape(rows, D_MODEL)


def _rope_tables(p, batch):
    pos = (jnp.arange(batch * p, dtype=jnp.int32) % p - PAD).astype(F32)[:, None]
    lane = jnp.arange(LANES, dtype=jnp.int32)[None, :]

    def tab(d):
        inv = ROPE_THETA ** (-((lane % (d // 2)) * 2).astype(F32) / d)
        ang = pos * inv
        return jnp.cos(ang), jnp.where(lane % d < d // 2, -jnp.sin(ang), jnp.sin(ang))

    cr, sr = tab(RET_DIM)
    ca, sa = tab(ATT_DIM)
    return cr, sr, ca, sa


def _pair_heads(t, axis):
    shape = t.shape
    t = t.reshape(shape[:axis] + (ATT_KV_HEADS, ATT_GROUPS, ATT_DIM) + shape[axis + 1:])
    t = jnp.swapaxes(t, axis, axis + 1)
    return t.reshape(shape)


def _prep_weights(w_in, w_attn_out):
    aq = _pair_heads(w_in[:, :, OFF_AQ:OFF_AK].astype(BF16), 2)
    w_in = lax.dynamic_update_slice(w_in.astype(BF16), aq, (0, 0, OFF_AQ))
    return w_in, _pair_heads(w_attn_out, 1).astype(BF16)


def _trunk(x, meta_tokens, w, final_norm):
    batch, s, _ = x.shape
    p = s + CHUNK
    rows = batch * p
    meta = jnp.broadcast_to(meta_tokens[None], (batch, N_META, D_MODEL))
    h = jnp.concatenate([jnp.zeros((batch, PAD, D_MODEL), F32), meta, x], axis=1).reshape(rows, D_MODEL)
    tabs = _rope_tables(p, batch)
    depth = w["w_in"].shape[0]
    for l in range(depth):
        h = _ffn(h, l, w["norm_ffn1"], w["wg1"], w["wu1"], w["wd1"])
        proj = _inproj(h, l, w["ret_decay"][l], w["norm_mix"], w["w_in"], w["conv_w"], tabs)
        h = _mixer(h, l, w["ret_decay"][l], w["attn_sink"][l], w["ret_gn_gain"], w["w_ret_out"],
                   w["w_conv_out"], w["w_attn_out"], w["w_o"], proj, batch)
        if l + 1 < depth:
            h = _ffn(h, l, w["norm_ffn2"], w["wg2"], w["wu2"], w["wd2"])
    return _ffn_final(h.reshape(batch, p, D_MODEL), depth - 1, w["norm_ffn2"], w["wg2"], w["wu2"], w["wd2"],
                      final_norm.reshape(1, D_MODEL))


def kernel(x_prompt, x_sample, meta_tokens, norm_ffn1, w_ffn1_gate, w_ffn1_up, w_ffn1_down, norm_mix, w_in, ret_decay, ret_gn_gain, conv_w, attn_sink, w_ret_out, w_conv_out, w_attn_out, w_o, norm_ffn2, w_ffn2_gate, w_ffn2_up, w_ffn2_down, final_norm):
    depth = w_in.shape[0]
    w_in_b, w_attn_out_b = _prep_weights(w_in, w_attn_out)
    row = lambda t: t.reshape(depth, 1, t.shape[-1])
    w = dict(
        norm_ffn1=row(norm_ffn1), wg1=w_ffn1_gate.astype(BF16), wu1=w_ffn1_up.astype(BF16),
        wd1=w_ffn1_down.astype(BF16), norm_mix=row(norm_mix), w_in=w_in_b, ret_decay=ret_decay,
        ret_gn_gain=row(ret_gn_gain), conv_w=conv_w, attn_sink=attn_sink,
        w_ret_out=w_ret_out.astype(BF16), w_conv_out=w_conv_out.astype(BF16), w_attn_out=w_attn_out_b,
        w_o=w_o.astype(BF16), norm_ffn2=row(norm_ffn2), wg2=w_ffn2_gate.astype(BF16),
        wu2=w_ffn2_up.astype(BF16), wd2=w_ffn2_down.astype(BF16))
    y_prompt = _trunk(x_prompt, meta_tokens, w, final_norm)
    y_sample = _trunk(x_sample, meta_tokens, w, final_norm)
    return (y_prompt, y_sample)
```

```python
import functools
import math

import jax
import jax.numpy as jnp
from jax import lax
from jax.experimental import pallas as pl
from jax.experimental.pallas import tpu as pltpu

F32 = jnp.float32
BF16 = jnp.bfloat16

D_MODEL = 1024
D_FF = 2816
N_META = 16
CHUNK = 128
PAD = CHUNK - N_META
RET_HEADS = 4
RET_DIM = 128
RET_WIDTH = RET_HEADS * RET_DIM
CONV_WIDTH = 512
ATT_Q_HEADS = 8
ATT_KV_HEADS = 2
ATT_DIM = 64
ATT_GROUPS = ATT_Q_HEADS // ATT_KV_HEADS
ATT_WIDTH = ATT_Q_HEADS * ATT_DIM
ATT_KV_WIDTH = ATT_KV_HEADS * ATT_DIM
WINDOW = 128
GATE_WIDTH = 3 * D_MODEL
ROPE_THETA = 10000.0
EPS = 1e-6
NEG_INF = -1e30

LANES = 128
MXU_COLS = 256
HALO_ROWS = 16
FF_CHUNK = MXU_COLS
INPROJ_ROWS = 512
SEQ_CHUNKS = 4
REV_CHUNKS = 2 * SEQ_CHUNKS
VMEM_LIMIT = 56 * 1024 * 1024
LOG2E = math.log2(math.e)
ATT_Q_SCALE = ATT_DIM ** -0.5 * LOG2E

OFF_RQ = 0
OFF_RK = OFF_RQ + RET_WIDTH
OFF_RV = OFF_RK + RET_WIDTH
OFF_RG = OFF_RV + RET_WIDTH
OFF_CB = OFF_RG + RET_WIDTH
OFF_CC = OFF_CB + CONV_WIDTH
OFF_CX = OFF_CC + CONV_WIDTH
OFF_AQ = OFF_CX + CONV_WIDTH
OFF_AK = OFF_AQ + ATT_WIDTH
OFF_AV = OFF_AK + ATT_KV_WIDTH
OFF_GATE = OFF_AV + ATT_KV_WIDTH
IN_WIDTH = OFF_GATE + GATE_WIDTH

RP_Q = 0
RP_QD = RP_Q + RET_WIDTH
RP_V = RP_QD + 2 * RET_WIDTH
RP_VF = RP_V + RET_WIDTH
RP_G = RP_VF + RET_WIDTH
RP_WIDTH = RP_G + RET_WIDTH
AP_K = ATT_WIDTH
AP_V = AP_K + ATT_KV_WIDTH
AP_WIDTH = AP_V + ATT_KV_WIDTH


def _params(*sem):
    return pltpu.CompilerParams(dimension_semantics=sem, vmem_limit_bytes=VMEM_LIMIT)


def _resident(shape, layer=None):
    nd = len(shape)
    if layer is None:
        return pl.BlockSpec(shape, lambda *_: (0,) * nd, pipeline_mode=pl.Buffered(1))
    return pl.BlockSpec((None,) + tuple(shape), lambda *_: (layer,) + (0,) * nd, pipeline_mode=pl.Buffered(1))


def _row_tile(rows, cap):
    best = CHUNK
    t = CHUNK
    while t <= cap:
        if rows % t == 0:
            best = t
        t += CHUNK
    return best


def _rms(x, gain):
    ms = jnp.mean(x * x, axis=-1, keepdims=True)
    return x * lax.rsqrt(ms + EPS) * gain


def _dot(a, b):
    return jnp.dot(a, b, preferred_element_type=F32)


def _dot_nt(a, b):
    return lax.dot_general(a, b, (((1,), (1,)), ((), ())), preferred_element_type=F32)


def _ffn_body(x, g_ref, wg_ref, wu_ref, wd_ref, acc_ref):
    n = _rms(x, g_ref[...]).astype(BF16)
    for c in range(D_FF // FF_CHUNK):
        sl = slice(c * FF_CHUNK, (c + 1) * FF_CHUNK)
        g = _dot(n, wg_ref[:, sl])
        u = _dot(n, wu_ref[:, sl])
        a = (g * jax.nn.sigmoid(g) * u).astype(BF16)
        d = _dot(a, wd_ref[sl, :])
        if c == 0:
            acc_ref[...] = d
        else:
            acc_ref[...] += d
    return x + 0.5 * acc_ref[...]


def _ffn_kernel(x_ref, g_ref, wg_ref, wu_ref, wd_ref, o_ref, acc_ref):
    o_ref[...] = _ffn_body(x_ref[...], g_ref, wg_ref, wu_ref, wd_ref, acc_ref)


def _ffn_final_kernel(x_ref, g_ref, wg_ref, wu_ref, wd_ref, fg_ref, o_ref, acc_ref):
    o_ref[...] = _rms(_ffn_body(x_ref[0], g_ref, wg_ref, wu_ref, wd_ref, acc_ref), fg_ref[...])


def _ffn_weight_specs(layer):
    return [_resident((1, D_MODEL), layer), _resident((D_MODEL, D_FF), layer), _resident((D_MODEL, D_FF), layer),
            _resident((D_FF, D_MODEL), layer)]


def _ffn(h, layer, gain, wg, wu, wd):
    rows = h.shape[0]
    tm = _row_tile(rows, 1024)
    return pl.pallas_call(
        _ffn_kernel,
        out_shape=jax.ShapeDtypeStruct((rows, D_MODEL), F32),
        grid=(rows // tm,),
        in_specs=[pl.BlockSpec((tm, D_MODEL), lambda i: (i, 0))] + _ffn_weight_specs(layer),
        out_specs=pl.BlockSpec((tm, D_MODEL), lambda i: (i, 0)),
        scratch_shapes=[pltpu.VMEM((tm, D_MODEL), F32)],
        compiler_params=_params("parallel"),
        name="ffn",
    )(h, gain, wg, wu, wd)


def _ffn_final(h3, layer, gain, wg, wu, wd, final_gain):
    batch, p, _ = h3.shape
    s = p - CHUNK
    tm = _row_tile(s, 1024)
    return pl.pallas_call(
        _ffn_final_kernel,
        out_shape=jax.ShapeDtypeStruct((batch, s, D_MODEL), F32),
        grid=(batch, s // tm),
        in_specs=[pl.BlockSpec((pl.Element(1), pl.Element(tm), pl.Element(D_MODEL)),
                               lambda b, j: (b, pl.multiple_of(CHUNK + j * tm, CHUNK), 0))]
        + _ffn_weight_specs(layer) + [_resident((1, D_MODEL))],
        out_specs=pl.BlockSpec((None, tm, D_MODEL), lambda b, j: (b, j, 0)),
        scratch_shapes=[pltpu.VMEM((tm, D_MODEL), F32)],
        compiler_params=_params("parallel", "parallel"),
        name="ffn_final",
    )(h3, gain, wg, wu, wd, final_gain)


def _rot128(x, cos, sin):
    return x * cos + pltpu.roll(x, 64, axis=1) * sin


def _rot64(x, cos, sin, low_half):
    partner = jnp.where(low_half, pltpu.roll(x, 96, axis=1), pltpu.roll(x, 32, axis=1))
    return x * cos + partner * sin


def _log_gamma_tile(dec_ref, direction, hd):
    return jnp.log1p(-jnp.exp(jnp.full((CHUNK, CHUNK), dec_ref[direction, hd], F32)))


D_QF, D_QB, D_VF, D_VB = range(4)


def _row_decay_tables(dec_ref, tab_ref):
    i = lax.broadcasted_iota(jnp.int32, (CHUNK, CHUNK), 0).astype(F32)
    for hd in range(RET_HEADS):
        sl = slice(hd * RET_DIM, (hd + 1) * RET_DIM)
        lgf = _log_gamma_tile(dec_ref, 0, hd)
        lgb = _log_gamma_tile(dec_ref, 1, hd)
        tab_ref[D_QF, :, sl] = jnp.exp(lgf * (i + 1.0))
        tab_ref[D_QB, :, sl] = jnp.exp(lgb * (CHUNK - i))
        tab_ref[D_VF, :, sl] = jnp.exp(lgf * (CHUNK - 1.0 - i))
        tab_ref[D_VB, :, sl] = jnp.exp(lgb * i)


def _inproj_kernel(dec_ref, x_ref, xp_ref, xn_ref, g_ref, w_ref, cw_ref, cr_ref, sr_ref, ca_ref, sa_ref,
                   rp_ref, rkt_ref, rvb_ref, yc_ref, ap_ref, gs_ref, dtab_ref, us_ref, *, total_rows):
    i = pl.program_id(0)
    tm = x_ref.shape[0]

    @pl.when(i == 0)
    def _():
        _row_decay_tables(dec_ref, dtab_ref)

    gain = g_ref[...]
    n = _rms(x_ref[...], gain).astype(BF16)

    def proj(off, width):
        return _dot(n, w_ref[:, off:off + width])

    cr, sr = cr_ref[...], sr_ref[...]
    ca, sa = ca_ref[...], sa_ref[...]
    low_half = (lax.broadcasted_iota(jnp.int32, (1, LANES), 1) % ATT_DIM) < (ATT_DIM // 2)
    chunks = [slice(ch * CHUNK, (ch + 1) * CHUNK) for ch in range(tm // CHUNK)]
    for c in range(RET_WIDTH // MXU_COLS):
        zq = proj(OFF_RQ + c * MXU_COLS, MXU_COLS)
        zk = proj(OFF_RK + c * MXU_COLS, MXU_COLS)
        for half in range(MXU_COLS // RET_DIM):
            hd = c * (MXU_COLS // RET_DIM) + half
            src = slice(half * RET_DIM, (half + 1) * RET_DIM)
            dst = slice(hd * RET_DIM, (hd + 1) * RET_DIM)
            q = _rot128(zq[:, src], cr, sr)
            k = _rot128(zk[:, src], cr, sr) * (RET_DIM ** -0.5)
            rp_ref[:, RP_Q + hd * RET_DIM:RP_Q + (hd + 1) * RET_DIM] = q.astype(BF16)
            for ch, rows in enumerate(chunks):
                lo = RP_QD + 2 * hd * RET_DIM
                rp_ref[rows, lo:lo + RET_DIM] = (q[rows] * dtab_ref[D_QF, :, dst]).astype(BF16)
                rp_ref[rows, lo + RET_DIM:lo + 2 * RET_DIM] = (q[rows] * dtab_ref[D_QB, :, dst]).astype(BF16)
                rkt_ref[ch, hd] = k[rows].T.astype(BF16)
    zv = proj(OFF_RV, RET_WIDTH)
    rp_ref[:, RP_V:RP_V + RET_WIDTH] = zv.astype(BF16)
    for rows in chunks:
        rp_ref[rows, RP_VF:RP_VF + RET_WIDTH] = (zv[rows] * dtab_ref[D_VF]).astype(BF16)
        rvb_ref[rows, :] = (zv[rows] * dtab_ref[D_VB]).astype(BF16)
    zg = proj(OFF_RG, RET_WIDTH)
    rp_ref[:, RP_G:RP_G + RET_WIDTH] = (zg * jax.nn.sigmoid(zg)).astype(BF16)

    n_ext = jnp.concatenate([_rms(xp_ref[...], gain).astype(BF16), n, _rms(xn_ref[...], gain).astype(BF16)], axis=0)
    u = (_dot(n_ext, w_ref[:, OFF_CC:OFF_CC + CONV_WIDTH]) * _dot(n_ext, w_ref[:, OFF_CX:OFF_CX + CONV_WIDTH]))
    grow = i * tm - HALO_ROWS + lax.broadcasted_iota(jnp.int32, (tm + 2 * HALO_ROWS, 1), 0)
    us_ref[...] = jnp.where((grow >= 0) & (grow < total_rows), u, 0.0)
    conv = (us_ref[pl.ds(HALO_ROWS - 1, tm), :] * cw_ref[0:1, :] + us_ref[pl.ds(HALO_ROWS, tm), :] * cw_ref[1:2, :]
            + us_ref[pl.ds(HALO_ROWS + 1, tm), :] * cw_ref[2:3, :])
    yc_ref[...] = (proj(OFF_CB, CONV_WIDTH) * conv).astype(BF16)

    for c in range(ATT_WIDTH // MXU_COLS):
        zq = proj(OFF_AQ + c * MXU_COLS, MXU_COLS)
        for half in range(MXU_COLS // LANES):
            src = slice(half * LANES, (half + 1) * LANES)
            dst = slice(c * MXU_COLS + half * LANES, c * MXU_COLS + (half + 1) * LANES)
            ap_ref[:, dst] = (_rot64(zq[:, src], ca, sa, low_half) * ATT_Q_SCALE).astype(BF16)
    zkv = proj(OFF_AK, 2 * ATT_KV_WIDTH)
    ap_ref[:, AP_K:AP_K + ATT_KV_WIDTH] = _rot64(zkv[:, :ATT_KV_WIDTH], ca, sa, low_half).astype(BF16)
    ap_ref[:, AP_V:AP_V + ATT_KV_WIDTH] = zkv[:, ATT_KV_WIDTH:].astype(BF16)
    for c in range(GATE_WIDTH // 512):
        sl = slice(c * 512, (c + 1) * 512)
        gs_ref[:, sl] = jax.nn.sigmoid(proj(OFF_GATE + c * 512, 512)).astype(BF16)


INPROJ_OUT = ("rp", "rkt", "rvb", "yc", "ap", "gs")


def _inproj(h, layer, dec, gain, w_in, conv_w, tabs, seq_rows):
    rows = h.shape[0]
    tm = _row_tile(rows, INPROJ_ROWS)
    hb = tm // HALO_ROWS
    row = lambda w: pl.BlockSpec((tm, w), lambda i: (i, 0))
    tab = pl.BlockSpec((pl.Element(tm), pl.Element(LANES)),
                       lambda i: (pl.multiple_of((i * tm) % seq_rows, CHUNK), 0))
    halo_prev = pl.BlockSpec((HALO_ROWS, D_MODEL), lambda i: (jnp.maximum(i * hb - 1, 0), 0))
    halo_next = pl.BlockSpec((HALO_ROWS, D_MODEL),
                              lambda i: (jnp.minimum((i + 1) * hb, rows // HALO_ROWS - 1), 0))
    widths = dict(rp=RP_WIDTH, rvb=RET_WIDTH, yc=CONV_WIDTH, ap=AP_WIDTH, gs=GATE_WIDTH)
    kt_shape = (rows // CHUNK, RET_HEADS, RET_DIM, RET_DIM)
    kt_spec = pl.BlockSpec((tm // CHUNK,) + kt_shape[1:], lambda i: (i, 0, 0, 0))
    outs = pl.pallas_call(
        functools.partial(_inproj_kernel, total_rows=rows),
        out_shape=[jax.ShapeDtypeStruct(kt_shape if name == "rkt" else (rows, widths[name]), BF16)
                   for name in INPROJ_OUT],
        grid=(rows // tm,),
        in_specs=[pl.BlockSpec(memory_space=pltpu.SMEM), row(D_MODEL), halo_prev, halo_next,
                  _resident((1, D_MODEL), layer), _resident((D_MODEL, IN_WIDTH), layer),
                  _resident((3, CONV_WIDTH), layer), tab, tab, tab, tab],
        out_specs=[kt_spec if name == "rkt" else row(widths[name]) for name in INPROJ_OUT],
        scratch_shapes=[pltpu.VMEM((4, CHUNK, RET_WIDTH), F32),
                        pltpu.VMEM((tm + 2 * HALO_ROWS, CONV_WIDTH), F32)],
        compiler_params=_params("arbitrary"),
        name="inproj",
    )(dec, h, h, h, gain, w_in, conv_w, *tabs)
    return dict(zip(INPROJ_OUT, outs))


T_CF, T_CB, T_MASK = range(3)
N_TABS = 3


def _chunk_tables(dec_ref, tab_ref):
    i = lax.broadcasted_iota(jnp.int32, (CHUNK, CHUNK), 0).astype(F32)
    j = lax.broadcasted_iota(jnp.int32, (CHUNK, CHUNK), 1).astype(F32)
    diff = i - j
    for hd in range(RET_HEADS):
        lgf = _log_gamma_tile(dec_ref, 0, hd)
        lgb = _log_gamma_tile(dec_ref, 1, hd)
        base = hd * N_TABS
        tab_ref[base + T_CF] = jnp.exp(lgf * CHUNK)
        tab_ref[base + T_CB] = jnp.exp(lgb * CHUNK)
        tab_ref[base + T_MASK] = jnp.where(diff >= 0, jnp.exp(lgf * jnp.maximum(diff, 0.0)),
                                           jnp.exp(lgb * jnp.maximum(-diff, 0.0)))


def _state_update(s_ref, tab_ref, hd, kt, vd, t_c):
    s_ref[hd] = tab_ref[hd * N_TABS + t_c] * s_ref[hd] + _dot(kt, vd)


def _ret_rev_kernel(dec_ref, kt_ref, vb_ref, sbs_ref, sb_ref, tab_ref, *, n_chunks):
    c = pl.program_id(1)
    blk = pl.num_programs(1) - 1 - c

    @pl.when(c == 0)
    def _():
        sb_ref[...] = jnp.zeros_like(sb_ref)
        _chunk_tables(dec_ref, tab_ref)

    for jj in reversed(range(REV_CHUNKS)):
        valid = blk * REV_CHUNKS + jj < n_chunks
        rows = slice(jj * CHUNK, (jj + 1) * CHUNK)
        for hd in range(RET_HEADS):
            sl = slice(hd * RET_DIM, (hd + 1) * RET_DIM)
            sbs_ref[0, 0, jj * RET_HEADS + hd] = sb_ref[hd].astype(BF16)
            kt = kt_ref[0, jj, hd]
            vd = vb_ref[0, rows, sl]
            kt = jnp.where(valid, kt, jnp.zeros_like(kt))
            vd = jnp.where(valid, vd, jnp.zeros_like(vd))
            _state_update(sb_ref, tab_ref, hd, kt, vd, T_CB)


def _ret_rev(dec, rkt5, rvb3):
    batch, p, _ = rvb3.shape
    n = p // CHUNK
    nblk = pl.cdiv(n, REV_CHUNKS)
    tiles = REV_CHUNKS * RET_HEADS
    tile = (RET_DIM, RET_DIM)
    sbs = pl.pallas_call(
        functools.partial(_ret_rev_kernel, n_chunks=n),
        out_shape=jax.ShapeDtypeStruct((batch, nblk, tiles) + tile, BF16),
        grid=(batch, nblk),
        in_specs=[pl.BlockSpec(memory_space=pltpu.SMEM),
                  pl.BlockSpec((1, REV_CHUNKS, RET_HEADS) + tile, lambda b, c: (b, nblk - 1 - c, 0, 0, 0)),
                  pl.BlockSpec((1, REV_CHUNKS * CHUNK, RET_WIDTH), lambda b, c: (b, nblk - 1 - c, 0))],
        out_specs=pl.BlockSpec((1, 1, tiles) + tile, lambda b, c: (b, nblk - 1 - c, 0, 0, 0)),
        scratch_shapes=[pltpu.VMEM((RET_HEADS,) + tile, F32),
                        pltpu.VMEM((RET_HEADS * N_TABS, CHUNK, CHUNK), F32)],
        compiler_params=_params("parallel", "arbitrary"),
        name="ret_rev",
    )(dec, rkt5, rvb3)
    per = REV_CHUNKS // SEQ_CHUNKS
    return sbs.reshape((batch, nblk * per, SEQ_CHUNKS * RET_HEADS) + tile)


def _ret_items(rp_ref, kt_ref, sbs_ref, gain_ref, sf_ref, tab_ref, jj, out):
    rows = slice(jj * CHUNK, (jj + 1) * CHUNK)

    def head_item(hd):
        def run():
            sl = slice(hd * RET_DIM, (hd + 1) * RET_DIM)
            part = lambda off, width=RET_DIM: rp_ref[0, rows, off + hd * width:off + (hd + 1) * width]
            kt = kt_ref[0, jj, hd]
            att = _dot(part(RP_Q), kt) * tab_ref[hd * N_TABS + T_MASK]
            o = _dot(att.astype(BF16), part(RP_V))
            sd = jnp.concatenate([sf_ref[hd].astype(BF16), sbs_ref[0, 0, jj * RET_HEADS + hd]], axis=0)
            o += _dot(part(RP_QD, 2 * RET_DIM), sd)
            _state_update(sf_ref, tab_ref, hd, kt, part(RP_VF), T_CF)
            mu = jnp.mean(o, axis=-1, keepdims=True)
            oc = o - mu
            var = jnp.mean(oc * oc, axis=-1, keepdims=True)
            on = oc * lax.rsqrt(var + EPS) * gain_ref[:, sl]
            out[(jj, hd)] = (part(RP_G).astype(F32) * on).astype(BF16)
        return run

    return [head_item(hd) for hd in range(RET_HEADS)]


def _attn_window(kvp_ref, ap_ref, kvn_ref, blk, seq_rows):
    tq = SEQ_CHUNKS * CHUNK
    span = tq + 2 * CHUNK
    head0 = lax.broadcasted_iota(jnp.int32, (1, LANES), 1) < ATT_DIM
    wpos = blk * tq - CHUNK + lax.broadcasted_iota(jnp.int32, (span, 1), 0)
    inside = (wpos >= 0) & (wpos < seq_rows)
    kw = ATT_KV_WIDTH
    kall = jnp.concatenate([kvp_ref[0, :, :kw], ap_ref[0, :, AP_K:AP_K + kw], kvn_ref[0, :, :kw]], axis=0)
    vall = jnp.concatenate([kvp_ref[0, :, kw:], ap_ref[0, :, AP_V:AP_V + kw], kvn_ref[0, :, kw:]], axis=0)
    zero = jnp.zeros_like(kall)
    kall = jnp.where(inside, kall, zero)
    vall = jnp.where(inside, vall, zero)
    return (jnp.where(head0, kall, zero), jnp.where(head0, zero, kall),
            jnp.where(head0, vall, zero), jnp.where(head0, zero, vall))


def _attn_items(sink_ref, q_ref, window, blk, jj, seq_rows, out):
    k0, k1, v0, v1 = window
    base = jj * CHUNK
    st = {}

    def scores():
        r = lax.broadcasted_iota(jnp.int32, (CHUNK, 3 * CHUNK), 0)
        j = lax.broadcasted_iota(jnp.int32, (CHUNK, 3 * CHUNK), 1)
        kpos = (blk * SEQ_CHUNKS + jj - 1) * CHUNK + j
        st["ok"] = (jnp.abs(CHUNK + r - j) <= WINDOW) & (kpos >= PAD) & (kpos < seq_rows)
        kb = jnp.concatenate([k0[base:base + 3 * CHUNK], k1[base:base + 3 * CHUNK]], axis=0)
        qs = jnp.concatenate([q_ref[0, base:base + CHUNK, c * LANES:(c + 1) * LANES]
                              for c in range(ATT_GROUPS)], axis=0)
        st["s"] = _dot_nt(qs, kb)

    def softmax_item(c):
        def run():
            head0 = lax.broadcasted_iota(jnp.int32, (1, LANES), 1) < ATT_DIM
            e_cols, inv_cols = [], []
            for kv in range(ATT_KV_HEADS):
                s = st["s"][c * CHUNK:(c + 1) * CHUNK, kv * 3 * CHUNK:(kv + 1) * 3 * CHUNK]
                s = jnp.where(st["ok"], s, NEG_INF)
                sink = sink_ref[kv * ATT_GROUPS + c] * LOG2E
                m = jnp.maximum(jnp.max(s, axis=-1, keepdims=True), sink)
                e = jnp.exp2(s - m)
                denom = jnp.sum(e, axis=-1, keepdims=True) + jnp.exp2(sink - m)
                e_cols.append(e.astype(BF16))
                inv_cols.append(1.0 / denom)
            st[("e", c)] = jnp.concatenate(e_cols, axis=1)
            st[("inv", c)] = jnp.where(head0, inv_cols[0], inv_cols[1])
        return run

    def values():
        vb = jnp.concatenate([v0[base:base + 3 * CHUNK], v1[base:base + 3 * CHUNK]], axis=0)
        o = _dot(jnp.concatenate([st[("e", c)] for c in range(ATT_GROUPS)], axis=0), vb)
        out[jj] = jnp.concatenate([(o[c * CHUNK:(c + 1) * CHUNK] * st[("inv", c)]).astype(BF16)
                                   for c in range(ATT_GROUPS)], axis=1)

    return [scores] + [softmax_item(c) for c in range(ATT_GROUPS)] + [values]


def _merge_items(yr_ref, yc_ref, ya_ref, gs_ref, h_ref, o_ref, wr_ref, wc_ref, wa_ref, wo_ref, row0):
    st = {}
    n_tiles = D_MODEL // MXU_COLS
    rows = h_ref.shape[1]

    def branch_item(nt, b, y_of, w_ref):
        def run():
            cols = slice(nt * MXU_COLS, (nt + 1) * MXU_COLS)
            gate = gs_ref[0, :, b * D_MODEL + nt * MXU_COLS:b * D_MODEL + (nt + 1) * MXU_COLS]
            term = gate.astype(F32) * _dot(y_of(), w_ref[:, cols])
            acc = term if b == 0 else st[nt] + term
            st[nt] = acc.astype(BF16) if b == len(branches) - 1 else acc
        return run

    def out_item(nt):
        def run():
            cols = slice(nt * MXU_COLS, (nt + 1) * MXU_COLS)
            merged = jnp.concatenate([st[k] for k in range(n_tiles)], axis=1)
            mix = _dot(merged, wo_ref[:, cols])
            pos = row0 + lax.broadcasted_iota(jnp.int32, (rows, 1), 0)
            o_ref[0, :, cols] = h_ref[0, :, cols] + jnp.where(pos < PAD, 0.0, mix)
        return run

    branches = ((lambda: yr_ref[...], wr_ref), (lambda: yc_ref[0], wc_ref), (lambda: ya_ref[...], wa_ref))
    items = [branch_item(nt, b, y_of, w_ref) for nt in range(n_tiles) for b, (y_of, w_ref) in enumerate(branches)]
    return items + [out_item(nt) for nt in range(n_tiles)]


def _interleave(major, minor):
    done = 0
    for idx, item in enumerate(major):
        item()
        want = (idx + 1) * len(minor) // len(major)
        while done < want:
            minor[done]()
            done += 1


def _mixer_kernel(dec_ref, sink_ref, rp_ref, rkt_ref, sbs_ref, gain_ref, ap_ref, kvp_ref, kvn_ref,
                  h_ref, yc_ref, gs_ref, wr_ref, wc_ref, wa_ref, wo_ref,
                  o_ref, sf_ref, tab_ref, yr_ref, ya_ref, *, seq_rows, n_blocks):
    j = pl.program_id(1)
    tq = SEQ_CHUNKS * CHUNK

    @pl.when(j == 0)
    def _():
        sf_ref[...] = jnp.zeros_like(sf_ref)
        _chunk_tables(dec_ref, tab_ref)

    def seq_items():
        window = _attn_window(kvp_ref, ap_ref, kvn_ref, j, seq_rows)
        yr_parts, ya_parts = {}, {}
        items = []
        for jj in range(SEQ_CHUNKS):
            ret = _ret_items(rp_ref, rkt_ref, sbs_ref, gain_ref, sf_ref, tab_ref, jj, yr_parts)
            att = _attn_items(sink_ref, ap_ref, window, j, jj, seq_rows, ya_parts)
            items += [att[0], ret[0], att[1], ret[1], att[2], att[3], ret[2], att[4], ret[3], att[5]]

        def park():
            for jj in range(SEQ_CHUNKS):
                rows = slice(jj * CHUNK, (jj + 1) * CHUNK)
                ya_ref[j % 2, rows, :] = ya_parts[jj]
                for hd in range(RET_HEADS):
                    yr_ref[j % 2, rows, hd * RET_DIM:(hd + 1) * RET_DIM] = yr_parts[(jj, hd)]
        return items, park

    def merge_items():
        slot = (j + 1) % 2
        return _merge_items(yr_ref.at[slot], yc_ref, ya_ref.at[slot], gs_ref, h_ref, o_ref,
                            wr_ref, wc_ref, wa_ref, wo_ref, (j - 1) * tq)

    @pl.when(j == 0)
    def _():
        items, park = seq_items()
        for item in items:
            item()
        park()

    @pl.when((j > 0) & (j < n_blocks))
    def _():
        items, park = seq_items()
        _interleave(items, merge_items())
        park()

    @pl.when(j == n_blocks)
    def _():
        for item in merge_items():
            item()


def _mixer(h, layer, dec, sink, gn_gain, wr, wc, wa, wo, proj, batch):
    rows = h.shape[0]
    p = rows // batch
    n = p // CHUNK
    tq = SEQ_CHUNKS * CHUNK
    nblk = pl.cdiv(n, SEQ_CHUNKS)
    v3 = lambda t: t.reshape(batch, p, t.shape[-1])
    tile = (RET_DIM, RET_DIM)
    rkt5 = proj["rkt"].reshape((batch, n, RET_HEADS) + tile)
    ap3 = v3(proj["ap"])
    sbs = _ret_rev(dec, rkt5, v3(proj["rvb"]))

    smem = pl.BlockSpec(memory_space=pltpu.SMEM)
    seq_blk = lambda j: jnp.minimum(j, nblk - 1)
    seq = lambda w: pl.BlockSpec((1, tq, w), lambda b, j: (b, seq_blk(j), 0))
    lag = lambda w: pl.BlockSpec((1, tq, w), lambda b, j: (b, jnp.maximum(j - 1, 0), 0))
    kv_cols = 2 * ATT_KV_WIDTH
    prev = pl.BlockSpec((1, CHUNK, kv_cols),
                        lambda b, j: (b, jnp.maximum(seq_blk(j) * SEQ_CHUNKS - 1, 0), AP_K // kv_cols))
    nxt = pl.BlockSpec((1, CHUNK, kv_cols),
                       lambda b, j: (b, jnp.minimum((seq_blk(j) + 1) * SEQ_CHUNKS, n - 1), AP_K // kv_cols))
    tiles = SEQ_CHUNKS * RET_HEADS
    out = pl.pallas_call(
        functools.partial(_mixer_kernel, seq_rows=p, n_blocks=nblk),
        out_shape=jax.ShapeDtypeStruct((batch, p, D_MODEL), F32),
        grid=(batch, nblk + 1),
        in_specs=[smem, smem, seq(RP_WIDTH),
                  pl.BlockSpec((1, SEQ_CHUNKS, RET_HEADS) + tile, lambda b, j: (b, seq_blk(j), 0, 0, 0)),
                  pl.BlockSpec((1, 1, tiles) + tile, lambda b, j: (b, seq_blk(j), 0, 0, 0)),
                  _resident((1, RET_WIDTH), layer),
                  seq(AP_WIDTH), prev, nxt,
                  lag(D_MODEL), lag(CONV_WIDTH), lag(GATE_WIDTH),
                  _resident((RET_WIDTH, D_MODEL), layer), _resident((CONV_WIDTH, D_MODEL), layer),
                  _resident((ATT_WIDTH, D_MODEL), layer), _resident((D_MODEL, D_MODEL), layer)],
        out_specs=lag(D_MODEL),
        scratch_shapes=[pltpu.VMEM((RET_HEADS,) + tile, F32),
                        pltpu.VMEM((RET_HEADS * N_TABS, CHUNK, CHUNK), F32),
                        pltpu.VMEM((2, tq, RET_WIDTH), BF16), pltpu.VMEM((2, tq, ATT_WIDTH), BF16)],
        compiler_params=_params("parallel", "arbitrary"),
        name="mixer",
    )(dec, sink, v3(proj["rp"]), rkt5, sbs, gn_gain, ap3, ap3, ap3,
      v3(h), v3(proj["yc"]), v3(proj["gs"]), wr, wc, wa, wo)
    return out.reshape(rows, D_MODEL)


def _rope_tables(p, tile_rows):
    pos = (jnp.arange(p + tile_rows, dtype=jnp.int32) % p - PAD).astype(F32)[:, None]
    lane = jnp.arange(LANES, dtype=jnp.int32)[None, :]

    def tab(d):
        inv = ROPE_THETA ** (-((lane % (d // 2)) * 2).astype(F32) / d)
        ang = pos * inv
        return jnp.cos(ang), jnp.where(lane % d < d // 2, -jnp.sin(ang), jnp.sin(ang))

    cr, sr = tab(RET_DIM)
    ca, sa = tab(ATT_DIM)
    return cr, sr, ca, sa


def _pair_heads(t, axis):
    shape = t.shape
    t = t.reshape(shape[:axis] + (ATT_KV_HEADS, ATT_GROUPS, ATT_DIM) + shape[axis + 1:])
    t = jnp.swapaxes(t, axis, axis + 1)
    return t.reshape(shape)


def _prep_weights(w_in, w_attn_out):
    aq = _pair_heads(w_in[:, :, OFF_AQ:OFF_AK].astype(BF16), 2)
    w_in = lax.dynamic_update_slice(w_in.astype(BF16), aq, (0, 0, OFF_AQ))
    return w_in, _pair_heads(w_attn_out, 1).astype(BF16)


def _trunk(x, meta_tokens, w, final_norm):
    batch, s, _ = x.shape
    p = s + CHUNK
    rows = batch * p
    meta = jnp.broadcast_to(meta_tokens[None], (batch, N_META, D_MODEL))
    h = jnp.concatenate([jnp.zeros((batch, PAD, D_MODEL), F32), meta, x], axis=1).reshape(rows, D_MODEL)
    tabs = _rope_tables(p, _row_tile(rows, INPROJ_ROWS))
    depth = w["w_in"].shape[0]
    for l in range(depth):
        h = _ffn(h, l, w["norm_ffn1"], w["wg1"], w["wu1"], w["wd1"])
        proj = _inproj(h, l, w["ret_decay"][l], w["norm_mix"], w["w_in"], w["conv_w"], tabs, p)
        h = _mixer(h, l, w["ret_decay"][l], w["attn_sink"][l], w["ret_gn_gain"], w["w_ret_out"],
                   w["w_conv_out"], w["w_attn_out"], w["w_o"], proj, batch)
        if l + 1 < depth:
            h = _ffn(h, l, w["norm_ffn2"], w["wg2"], w["wu2"], w["wd2"])
    return _ffn_final(h.reshape(batch, p, D_MODEL), depth - 1, w["norm_ffn2"], w["wg2"], w["wu2"], w["wd2"],
                      final_norm.reshape(1, D_MODEL))


def kernel(x_prompt, x_sample, meta_tokens, norm_ffn1, w_ffn1_gate, w_ffn1_up, w_ffn1_down, norm_mix, w_in, ret_decay, ret_gn_gain, conv_w, attn_sink, w_ret_out, w_conv_out, w_attn_out, w_o, norm_ffn2, w_ffn2_gate, w_ffn2_up, w_ffn2_down, final_norm):
    depth = w_in.shape[0]
    w_in_b, w_attn_out_b = _prep_weights(w_in, w_attn_out)
    row = lambda t: t.reshape(depth, 1, t.shape[-1])
    w = dict(
        norm_ffn1=row(norm_ffn1), wg1=w_ffn1_gate.astype(BF16), wu1=w_ffn1_up.astype(BF16),
        wd1=w_ffn1_down.astype(BF16), norm_mix=row(norm_mix), w_in=w_in_b, ret_decay=ret_decay,
        ret_gn_gain=row(ret_gn_gain), conv_w=conv_w, attn_sink=attn_sink,
        w_ret_out=w_ret_out.astype(BF16), w_conv_out=w_conv_out.astype(BF16), w_attn_out=w_attn_out_b,
        w_o=w_o.astype(BF16), norm_ffn2=row(norm_ffn2), wg2=w_ffn2_gate.astype(BF16),
        wu2=w_ffn2_up.astype(BF16), wd2=w_ffn2_down.astype(BF16))
    y_prompt = _trunk(x_prompt, meta_tokens, w, final_norm)
    y_sample = _trunk(x_sample, meta_tokens, w, final_norm)
    return (y_prompt, y_sample)
```

```python
import functools
import math

import jax
import jax.numpy as jnp
from jax import lax
from jax.experimental import pallas as pl
from jax.experimental.pallas import tpu as pltpu

F32 = jnp.float32
BF16 = jnp.bfloat16

D_MODEL = 1024
D_FF = 2816
N_META = 16
CHUNK = 128
PAD = CHUNK - N_META
RET_HEADS = 4
RET_DIM = 128
RET_WIDTH = RET_HEADS * RET_DIM
CONV_WIDTH = 512
ATT_Q_HEADS = 8
ATT_KV_HEADS = 2
ATT_DIM = 64
ATT_GROUPS = ATT_Q_HEADS // ATT_KV_HEADS
ATT_WIDTH = ATT_Q_HEADS * ATT_DIM
ATT_KV_WIDTH = ATT_KV_HEADS * ATT_DIM
WINDOW = 128
GATE_WIDTH = 3 * D_MODEL
ROPE_THETA = 10000.0
EPS = 1e-6
NEG_INF = -1e30

LANES = 128
MXU_COLS = 256
HALO_ROWS = 16
FF_CHUNK = MXU_COLS
INPROJ_ROWS = 512
SEQ_CHUNKS = 4
VMEM_LIMIT = 56 * 1024 * 1024
LOG2E = math.log2(math.e)
ATT_Q_SCALE = ATT_DIM ** -0.5 * LOG2E

OFF_RQ = 0
OFF_RK = OFF_RQ + RET_WIDTH
OFF_RV = OFF_RK + RET_WIDTH
OFF_RG = OFF_RV + RET_WIDTH
OFF_CB = OFF_RG + RET_WIDTH
OFF_CC = OFF_CB + CONV_WIDTH
OFF_CX = OFF_CC + CONV_WIDTH
OFF_AQ = OFF_CX + CONV_WIDTH
OFF_AK = OFF_AQ + ATT_WIDTH
OFF_AV = OFF_AK + ATT_KV_WIDTH
OFF_GATE = OFF_AV + ATT_KV_WIDTH
IN_WIDTH = OFF_GATE + GATE_WIDTH

RP_Q = 0
RP_QD = RP_Q + RET_WIDTH
RP_V = RP_QD + 2 * RET_WIDTH
RP_VF = RP_V + RET_WIDTH
RP_G = RP_VF + RET_WIDTH
RP_WIDTH = RP_G + RET_WIDTH
AP_K = ATT_WIDTH
AP_V = AP_K + ATT_KV_WIDTH
AP_WIDTH = AP_V + ATT_KV_WIDTH


def _params(*sem):
    return pltpu.CompilerParams(dimension_semantics=sem, vmem_limit_bytes=VMEM_LIMIT)


def _resident(shape, layer=None):
    nd = len(shape)
    if layer is None:
        return pl.BlockSpec(shape, lambda *_: (0,) * nd, pipeline_mode=pl.Buffered(1))
    return pl.BlockSpec((None,) + tuple(shape), lambda *_: (layer,) + (0,) * nd, pipeline_mode=pl.Buffered(1))


def _row_tile(rows, cap):
    best = CHUNK
    t = CHUNK
    while t <= cap:
        if rows % t == 0:
            best = t
        t += CHUNK
    return best


def _rms(x, gain):
    ms = jnp.mean(x * x, axis=-1, keepdims=True)
    return x * lax.rsqrt(ms + EPS) * gain


def _dot(a, b):
    return jnp.dot(a, b, preferred_element_type=F32)


def _dot_nt(a, b):
    return lax.dot_general(a, b, (((1,), (1,)), ((), ())), preferred_element_type=F32)


def _ffn_body(x, g_ref, wg_ref, wu_ref, wd_ref, acc_ref):
    n = _rms(x, g_ref[...]).astype(BF16)
    for c in range(D_FF // FF_CHUNK):
        sl = slice(c * FF_CHUNK, (c + 1) * FF_CHUNK)
        g = _dot(n, wg_ref[:, sl])
        u = _dot(n, wu_ref[:, sl])
        a = (g * jax.nn.sigmoid(g) * u).astype(BF16)
        d = _dot(a, wd_ref[sl, :])
        if c == 0:
            acc_ref[...] = d
        else:
            acc_ref[...] += d
    return x + 0.5 * acc_ref[...]


def _ffn_kernel(x_ref, g_ref, wg_ref, wu_ref, wd_ref, o_ref, acc_ref):
    o_ref[...] = _ffn_body(x_ref[...], g_ref, wg_ref, wu_ref, wd_ref, acc_ref)


def _ffn_final_kernel(x_ref, g_ref, wg_ref, wu_ref, wd_ref, fg_ref, o_ref, acc_ref):
    o_ref[...] = _rms(_ffn_body(x_ref[0], g_ref, wg_ref, wu_ref, wd_ref, acc_ref), fg_ref[...])


def _ffn_weight_specs(layer):
    return [_resident((1, D_MODEL), layer), _resident((D_MODEL, D_FF), layer), _resident((D_MODEL, D_FF), layer),
            _resident((D_FF, D_MODEL), layer)]


def _ffn(h, layer, gain, wg, wu, wd):
    rows = h.shape[0]
    tm = _row_tile(rows, 1024)
    return pl.pallas_call(
        _ffn_kernel,
        out_shape=jax.ShapeDtypeStruct((rows, D_MODEL), F32),
        grid=(rows // tm,),
        in_specs=[pl.BlockSpec((tm, D_MODEL), lambda i: (i, 0))] + _ffn_weight_specs(layer),
        out_specs=pl.BlockSpec((tm, D_MODEL), lambda i: (i, 0)),
        scratch_shapes=[pltpu.VMEM((tm, D_MODEL), F32)],
        compiler_params=_params("parallel"),
        name="ffn",
    )(h, gain, wg, wu, wd)


def _ffn_final(h3, layer, gain, wg, wu, wd, final_gain):
    batch, p, _ = h3.shape
    s = p - CHUNK
    tm = _row_tile(s, 1024)
    return pl.pallas_call(
        _ffn_final_kernel,
        out_shape=jax.ShapeDtypeStruct((batch, s, D_MODEL), F32),
        grid=(batch, s // tm),
        in_specs=[pl.BlockSpec((pl.Element(1), pl.Element(tm), pl.Element(D_MODEL)),
                               lambda b, j: (b, pl.multiple_of(CHUNK + j * tm, CHUNK), 0))]
        + _ffn_weight_specs(layer) + [_resident((1, D_MODEL))],
        out_specs=pl.BlockSpec((None, tm, D_MODEL), lambda b, j: (b, j, 0)),
        scratch_shapes=[pltpu.VMEM((tm, D_MODEL), F32)],
        compiler_params=_params("parallel", "parallel"),
        name="ffn_final",
    )(h3, gain, wg, wu, wd, final_gain)


def _rot128(x, cos, sin):
    return x * cos + pltpu.roll(x, 64, axis=1) * sin


def _rot64(x, cos, sin, low_half):
    partner = jnp.where(low_half, pltpu.roll(x, 96, axis=1), pltpu.roll(x, 32, axis=1))
    return x * cos + partner * sin


def _log_gamma_tile(dec_ref, direction, hd):
    return jnp.log1p(-jnp.exp(jnp.full((CHUNK, CHUNK), dec_ref[direction, hd], F32)))


D_QF, D_QB, D_VF, D_VB, D_CB = range(5)


def _row_decay_tables(dec_ref, tab_ref):
    i = lax.broadcasted_iota(jnp.int32, (CHUNK, CHUNK), 0).astype(F32)
    for hd in range(RET_HEADS):
        sl = slice(hd * RET_DIM, (hd + 1) * RET_DIM)
        lgf = _log_gamma_tile(dec_ref, 0, hd)
        lgb = _log_gamma_tile(dec_ref, 1, hd)
        tab_ref[D_QF, :, sl] = jnp.exp(lgf * (i + 1.0))
        tab_ref[D_QB, :, sl] = jnp.exp(lgb * (CHUNK - i))
        tab_ref[D_VF, :, sl] = jnp.exp(lgf * (CHUNK - 1.0 - i))
        tab_ref[D_VB, :, sl] = jnp.exp(lgb * i)
        tab_ref[D_CB, :, sl] = jnp.exp(lgb * CHUNK)


def _inproj_kernel(dec_ref, x_ref, xp_ref, xn_ref, g_ref, w_ref, cw_ref, cr_ref, sr_ref, ca_ref, sa_ref,
                   rp_ref, rkt_ref, sbs_ref, yc_ref, ap_ref, gs_ref, dtab_ref, us_ref, sb_ref,
                   *, total_rows, seq_rows):
    i = pl.num_programs(0) - 1 - pl.program_id(0)
    tm = x_ref.shape[0]

    @pl.when(pl.program_id(0) == 0)
    def _():
        _row_decay_tables(dec_ref, dtab_ref)
        sb_ref[...] = jnp.zeros_like(sb_ref)

    gain = g_ref[...]
    n = _rms(x_ref[...], gain).astype(BF16)

    def proj(off, width):
        return _dot(n, w_ref[:, off:off + width])

    cr, sr = cr_ref[...], sr_ref[...]
    ca, sa = ca_ref[...], sa_ref[...]
    low_half = (lax.broadcasted_iota(jnp.int32, (1, LANES), 1) % ATT_DIM) < (ATT_DIM // 2)
    chunks = [slice(ch * CHUNK, (ch + 1) * CHUNK) for ch in range(tm // CHUNK)]
    kts = {}
    for c in range(RET_WIDTH // MXU_COLS):
        zq = proj(OFF_RQ + c * MXU_COLS, MXU_COLS)
        zk = proj(OFF_RK + c * MXU_COLS, MXU_COLS)
        for half in range(MXU_COLS // RET_DIM):
            hd = c * (MXU_COLS // RET_DIM) + half
            src = slice(half * RET_DIM, (half + 1) * RET_DIM)
            dst = slice(hd * RET_DIM, (hd + 1) * RET_DIM)
            q = _rot128(zq[:, src], cr, sr)
            k = _rot128(zk[:, src], cr, sr) * (RET_DIM ** -0.5)
            rp_ref[:, RP_Q + hd * RET_DIM:RP_Q + (hd + 1) * RET_DIM] = q.astype(BF16)
            for ch, rows in enumerate(chunks):
                lo = RP_QD + 2 * hd * RET_DIM
                rp_ref[rows, lo:lo + RET_DIM] = (q[rows] * dtab_ref[D_QF, :, dst]).astype(BF16)
                rp_ref[rows, lo + RET_DIM:lo + 2 * RET_DIM] = (q[rows] * dtab_ref[D_QB, :, dst]).astype(BF16)
                kts[(ch, hd)] = k[rows].T.astype(BF16)
                rkt_ref[ch, hd] = kts[(ch, hd)]
    zv = proj(OFF_RV, RET_WIDTH)
    rp_ref[:, RP_V:RP_V + RET_WIDTH] = zv.astype(BF16)
    for rows in chunks:
        rp_ref[rows, RP_VF:RP_VF + RET_WIDTH] = (zv[rows] * dtab_ref[D_VF]).astype(BF16)

    seq_chunks = seq_rows // CHUNK
    for ch in reversed(range(len(chunks))):
        chunk = i * len(chunks) + ch
        ends_sequence = lax.rem(chunk, seq_chunks) == seq_chunks - 1
        vdb = (zv[chunks[ch]] * dtab_ref[D_VB]).astype(BF16)
        for hd in range(RET_HEADS):
            sl = slice(hd * RET_DIM, (hd + 1) * RET_DIM)
            later = jnp.where(ends_sequence, 0.0, sb_ref[hd])
            sbs_ref[ch, hd] = later.astype(BF16)
            sb_ref[hd] = dtab_ref[D_CB, :, sl] * later + _dot(kts[(ch, hd)], vdb[:, sl])
    zg = proj(OFF_RG, RET_WIDTH)
    rp_ref[:, RP_G:RP_G + RET_WIDTH] = (zg * jax.nn.sigmoid(zg)).astype(BF16)

    n_ext = jnp.concatenate([_rms(xp_ref[...], gain).astype(BF16), n, _rms(xn_ref[...], gain).astype(BF16)], axis=0)
    u = (_dot(n_ext, w_ref[:, OFF_CC:OFF_CC + CONV_WIDTH]) * _dot(n_ext, w_ref[:, OFF_CX:OFF_CX + CONV_WIDTH]))
    grow = i * tm - HALO_ROWS + lax.broadcasted_iota(jnp.int32, (tm + 2 * HALO_ROWS, 1), 0)
    us_ref[...] = jnp.where((grow >= 0) & (grow < total_rows), u, 0.0)
    conv = (us_ref[pl.ds(HALO_ROWS - 1, tm), :] * cw_ref[0:1, :] + us_ref[pl.ds(HALO_ROWS, tm), :] * cw_ref[1:2, :]
            + us_ref[pl.ds(HALO_ROWS + 1, tm), :] * cw_ref[2:3, :])
    yc_ref[...] = (proj(OFF_CB, CONV_WIDTH) * conv).astype(BF16)

    for c in range(ATT_WIDTH // MXU_COLS):
        zq = proj(OFF_AQ + c * MXU_COLS, MXU_COLS)
        for half in range(MXU_COLS // LANES):
            src = slice(half * LANES, (half + 1) * LANES)
            dst = slice(c * MXU_COLS + half * LANES, c * MXU_COLS + (half + 1) * LANES)
            ap_ref[:, dst] = (_rot64(zq[:, src], ca, sa, low_half) * ATT_Q_SCALE).astype(BF16)
    zkv = proj(OFF_AK, 2 * ATT_KV_WIDTH)
    ap_ref[:, AP_K:AP_K + ATT_KV_WIDTH] = _rot64(zkv[:, :ATT_KV_WIDTH], ca, sa, low_half).astype(BF16)
    ap_ref[:, AP_V:AP_V + ATT_KV_WIDTH] = zkv[:, ATT_KV_WIDTH:].astype(BF16)
    for c in range(GATE_WIDTH // 512):
        sl = slice(c * 512, (c + 1) * 512)
        gs_ref[:, sl] = jax.nn.sigmoid(proj(OFF_GATE + c * 512, 512)).astype(BF16)


INPROJ_OUT = ("rp", "rkt", "sbs", "yc", "ap", "gs")
INPROJ_TILES = ("rkt", "sbs")


def _inproj(h, layer, dec, gain, w_in, conv_w, tabs, seq_rows):
    rows = h.shape[0]
    tm = _row_tile(rows, INPROJ_ROWS)
    nt = rows // tm
    hb = tm // HALO_ROWS
    tile_of = lambda i: nt - 1 - i
    row = lambda w: pl.BlockSpec((tm, w), lambda i: (tile_of(i), 0))
    tab = pl.BlockSpec((pl.Element(tm), pl.Element(LANES)),
                       lambda i: (pl.multiple_of((tile_of(i) * tm) % seq_rows, CHUNK), 0))
    halo_prev = pl.BlockSpec((HALO_ROWS, D_MODEL), lambda i: (jnp.maximum(tile_of(i) * hb - 1, 0), 0))
    halo_next = pl.BlockSpec((HALO_ROWS, D_MODEL),
                              lambda i: (jnp.minimum((tile_of(i) + 1) * hb, rows // HALO_ROWS - 1), 0))
    widths = dict(rp=RP_WIDTH, yc=CONV_WIDTH, ap=AP_WIDTH, gs=GATE_WIDTH)
    tiles_shape = (rows // CHUNK, RET_HEADS, RET_DIM, RET_DIM)
    tiles_spec = pl.BlockSpec((tm // CHUNK,) + tiles_shape[1:], lambda i: (tile_of(i), 0, 0, 0))
    outs = pl.pallas_call(
        functools.partial(_inproj_kernel, total_rows=rows, seq_rows=seq_rows),
        out_shape=[jax.ShapeDtypeStruct(tiles_shape if name in INPROJ_TILES else (rows, widths[name]), BF16)
                   for name in INPROJ_OUT],
        grid=(nt,),
        in_specs=[pl.BlockSpec(memory_space=pltpu.SMEM), row(D_MODEL), halo_prev, halo_next,
                  _resident((1, D_MODEL), layer), _resident((D_MODEL, IN_WIDTH), layer),
                  _resident((3, CONV_WIDTH), layer), tab, tab, tab, tab],
        out_specs=[tiles_spec if name in INPROJ_TILES else row(widths[name]) for name in INPROJ_OUT],
        scratch_shapes=[pltpu.VMEM((5, CHUNK, RET_WIDTH), F32),
                        pltpu.VMEM((tm + 2 * HALO_ROWS, CONV_WIDTH), F32),
                        pltpu.VMEM((RET_HEADS, RET_DIM, RET_DIM), F32)],
        compiler_params=_params("arbitrary"),
        name="inproj",
    )(dec, h, h, h, gain, w_in, conv_w, *tabs)
    return dict(zip(INPROJ_OUT, outs))


T_CF, T_MASK = range(2)
N_TABS = 2


def _chunk_tables(dec_ref, tab_ref):
    i = lax.broadcasted_iota(jnp.int32, (CHUNK, CHUNK), 0).astype(F32)
    j = lax.broadcasted_iota(jnp.int32, (CHUNK, CHUNK), 1).astype(F32)
    diff = i - j
    for hd in range(RET_HEADS):
        lgf = _log_gamma_tile(dec_ref, 0, hd)
        lgb = _log_gamma_tile(dec_ref, 1, hd)
        base = hd * N_TABS
        tab_ref[base + T_CF] = jnp.exp(lgf * CHUNK)
        tab_ref[base + T_MASK] = jnp.where(diff >= 0, jnp.exp(lgf * jnp.maximum(diff, 0.0)),
                                           jnp.exp(lgb * jnp.maximum(-diff, 0.0)))


def _state_update(s_ref, tab_ref, hd, kt, vd, t_c):
    s_ref[hd] = tab_ref[hd * N_TABS + t_c] * s_ref[hd] + _dot(kt, vd)


def _ret_items(rp_ref, kt_ref, sbs_ref, gain_ref, sf_ref, tab_ref, jj, out):
    rows = slice(jj * CHUNK, (jj + 1) * CHUNK)

    def head_item(hd):
        def run():
            sl = slice(hd * RET_DIM, (hd + 1) * RET_DIM)
            part = lambda off, width=RET_DIM: rp_ref[0, rows, off + hd * width:off + (hd + 1) * width]
            kt = kt_ref[0, jj, hd]
            att = _dot(part(RP_Q), kt) * tab_ref[hd * N_TABS + T_MASK]
            o = _dot(att.astype(BF16), part(RP_V))
            sd = jnp.concatenate([sf_ref[hd].astype(BF16), sbs_ref[0, jj, hd]], axis=0)
            o += _dot(part(RP_QD, 2 * RET_DIM), sd)
            _state_update(sf_ref, tab_ref, hd, kt, part(RP_VF), T_CF)
            mu = jnp.mean(o, axis=-1, keepdims=True)
            oc = o - mu
            var = jnp.mean(oc * oc, axis=-1, keepdims=True)
            on = oc * lax.rsqrt(var + EPS) * gain_ref[:, sl]
            out[(jj, hd)] = (part(RP_G).astype(F32) * on).astype(BF16)
        return run

    return [head_item(hd) for hd in range(RET_HEADS)]


def _attn_window(kvp_ref, ap_ref, kvn_ref, blk, seq_rows):
    tq = SEQ_CHUNKS * CHUNK
    span = tq + 2 * CHUNK
    head0 = lax.broadcasted_iota(jnp.int32, (1, LANES), 1) < ATT_DIM
    wpos = blk * tq - CHUNK + lax.broadcasted_iota(jnp.int32, (span, 1), 0)
    inside = (wpos >= 0) & (wpos < seq_rows)
    kw = ATT_KV_WIDTH
    kall = jnp.concatenate([kvp_ref[0, :, :kw], ap_ref[0, :, AP_K:AP_K + kw], kvn_ref[0, :, :kw]], axis=0)
    vall = jnp.concatenate([kvp_ref[0, :, kw:], ap_ref[0, :, AP_V:AP_V + kw], kvn_ref[0, :, kw:]], axis=0)
    zero = jnp.zeros_like(kall)
    kall = jnp.where(inside, kall, zero)
    vall = jnp.where(inside, vall, zero)
    return (jnp.where(head0, kall, zero), jnp.where(head0, zero, kall),
            jnp.where(head0, vall, zero), jnp.where(head0, zero, vall))


def _attn_items(sink_ref, q_ref, window, blk, jj, seq_rows, out):
    k0, k1, v0, v1 = window
    base = jj * CHUNK
    st = {}

    def scores():
        r = lax.broadcasted_iota(jnp.int32, (CHUNK, 3 * CHUNK), 0)
        j = lax.broadcasted_iota(jnp.int32, (CHUNK, 3 * CHUNK), 1)
        kpos = (blk * SEQ_CHUNKS + jj - 1) * CHUNK + j
        st["ok"] = (jnp.abs(CHUNK + r - j) <= WINDOW) & (kpos >= PAD) & (kpos < seq_rows)
        kb = jnp.concatenate([k0[base:base + 3 * CHUNK], k1[base:base + 3 * CHUNK]], axis=0)
        qs = jnp.concatenate([q_ref[0, base:base + CHUNK, c * LANES:(c + 1) * LANES]
                              for c in range(ATT_GROUPS)], axis=0)
        st["s"] = _dot_nt(qs, kb)

    def softmax_item(c):
        def run():
            head0 = lax.broadcasted_iota(jnp.int32, (1, LANES), 1) < ATT_DIM
            e_cols, inv_cols = [], []
            for kv in range(ATT_KV_HEADS):
                s = st["s"][c * CHUNK:(c + 1) * CHUNK, kv * 3 * CHUNK:(kv + 1) * 3 * CHUNK]
                s = jnp.where(st["ok"], s, NEG_INF)
                sink = sink_ref[kv * ATT_GROUPS + c] * LOG2E
                m = jnp.maximum(jnp.max(s, axis=-1, keepdims=True), sink)
                e = jnp.exp2(s - m)
                denom = jnp.sum(e, axis=-1, keepdims=True) + jnp.exp2(sink - m)
                e_cols.append(e.astype(BF16))
                inv_cols.append(1.0 / denom)
            st[("e", c)] = jnp.concatenate(e_cols, axis=1)
            st[("inv", c)] = jnp.where(head0, inv_cols[0], inv_cols[1])
        return run

    def values():
        vb = jnp.concatenate([v0[base:base + 3 * CHUNK], v1[base:base + 3 * CHUNK]], axis=0)
        o = _dot(jnp.concatenate([st[("e", c)] for c in range(ATT_GROUPS)], axis=0), vb)
        out[jj] = jnp.concatenate([(o[c * CHUNK:(c + 1) * CHUNK] * st[("inv", c)]).astype(BF16)
                                   for c in range(ATT_GROUPS)], axis=1)

    return [scores] + [softmax_item(c) for c in range(ATT_GROUPS)] + [values]


def _merge_items(yr_ref, yc_ref, ya_ref, gs_ref, h_ref, o_ref, wr_ref, wc_ref, wa_ref, wo_ref, row0):
    st = {}
    n_tiles = D_MODEL // MXU_COLS
    rows = h_ref.shape[1]

    def branch_item(nt, b, y_of, w_ref):
        def run():
            cols = slice(nt * MXU_COLS, (nt + 1) * MXU_COLS)
            gate = gs_ref[0, :, b * D_MODEL + nt * MXU_COLS:b * D_MODEL + (nt + 1) * MXU_COLS]
            term = gate.astype(F32) * _dot(y_of(), w_ref[:, cols])
            acc = term if b == 0 else st[nt] + term
            st[nt] = acc.astype(BF16) if b == len(branches) - 1 else acc
        return run

    def out_item(nt):
        def run():
            cols = slice(nt * MXU_COLS, (nt + 1) * MXU_COLS)
            merged = jnp.concatenate([st[k] for k in range(n_tiles)], axis=1)
            mix = _dot(merged, wo_ref[:, cols])
            pos = row0 + lax.broadcasted_iota(jnp.int32, (rows, 1), 0)
            o_ref[0, :, cols] = h_ref[0, :, cols] + jnp.where(pos < PAD, 0.0, mix)
        return run

    branches = ((lambda: yr_ref[...], wr_ref), (lambda: yc_ref[0], wc_ref), (lambda: ya_ref[...], wa_ref))
    items = [branch_item(nt, b, y_of, w_ref) for nt in range(n_tiles) for b, (y_of, w_ref) in enumerate(branches)]
    return items + [out_item(nt) for nt in range(n_tiles)]


def _interleave(major, minor):
    done = 0
    for idx, item in enumerate(major):
        item()
        want = (idx + 1) * len(minor) // len(major)
        while done < want:
            minor[done]()
            done += 1


def _mixer_kernel(dec_ref, sink_ref, rp_ref, rkt_ref, sbs_ref, gain_ref, ap_ref, kvp_ref, kvn_ref,
                  h_ref, yc_ref, gs_ref, wr_ref, wc_ref, wa_ref, wo_ref,
                  o_ref, sf_ref, tab_ref, yr_ref, ya_ref, *, seq_rows, n_blocks):
    j = pl.program_id(1)
    tq = SEQ_CHUNKS * CHUNK

    @pl.when(j == 0)
    def _():
        sf_ref[...] = jnp.zeros_like(sf_ref)
        _chunk_tables(dec_ref, tab_ref)

    def seq_items():
        window = _attn_window(kvp_ref, ap_ref, kvn_ref, j, seq_rows)
        yr_parts, ya_parts = {}, {}
        items = []
        for jj in range(SEQ_CHUNKS):
            ret = _ret_items(rp_ref, rkt_ref, sbs_ref, gain_ref, sf_ref, tab_ref, jj, yr_parts)
            att = _attn_items(sink_ref, ap_ref, window, j, jj, seq_rows, ya_parts)
            items += [att[0], ret[0], att[1], ret[1], att[2], att[3], ret[2], att[4], ret[3], att[5]]

        def park():
            for jj in range(SEQ_CHUNKS):
                rows = slice(jj * CHUNK, (jj + 1) * CHUNK)
                ya_ref[j % 2, rows, :] = ya_parts[jj]
                for hd in range(RET_HEADS):
                    yr_ref[j % 2, rows, hd * RET_DIM:(hd + 1) * RET_DIM] = yr_parts[(jj, hd)]
        return items, park

    def merge_items():
        slot = (j + 1) % 2
        return _merge_items(yr_ref.at[slot], yc_ref, ya_ref.at[slot], gs_ref, h_ref, o_ref,
                            wr_ref, wc_ref, wa_ref, wo_ref, (j - 1) * tq)

    @pl.when(j == 0)
    def _():
        items, park = seq_items()
        for item in items:
            item()
        park()

    @pl.when((j > 0) & (j < n_blocks))
    def _():
        items, park = seq_items()
        _interleave(items, merge_items())
        park()

    @pl.when(j == n_blocks)
    def _():
        for item in merge_items():
            item()


def _mixer(h, layer, dec, sink, gn_gain, wr, wc, wa, wo, proj, batch):
    rows = h.shape[0]
    p = rows // batch
    n = p // CHUNK
    tq = SEQ_CHUNKS * CHUNK
    nblk = pl.cdiv(n, SEQ_CHUNKS)
    v3 = lambda t: t.reshape(batch, p, t.shape[-1])
    tile = (RET_DIM, RET_DIM)
    rkt5 = proj["rkt"].reshape((batch, n, RET_HEADS) + tile)
    sbs5 = proj["sbs"].reshape((batch, n, RET_HEADS) + tile)
    ap3 = v3(proj["ap"])
    tiles_spec = pl.BlockSpec((1, SEQ_CHUNKS, RET_HEADS) + tile, lambda b, j: (b, seq_blk(j), 0, 0, 0))

    smem = pl.BlockSpec(memory_space=pltpu.SMEM)
    seq_blk = lambda j: jnp.minimum(j, nblk - 1)
    seq = lambda w: pl.BlockSpec((1, tq, w), lambda b, j: (b, seq_blk(j), 0))
    lag = lambda w: pl.BlockSpec((1, tq, w), lambda b, j: (b, jnp.maximum(j - 1, 0), 0))
    kv_cols = 2 * ATT_KV_WIDTH
    prev = pl.BlockSpec((1, CHUNK, kv_cols),
                        lambda b, j: (b, jnp.maximum(seq_blk(j) * SEQ_CHUNKS - 1, 0), AP_K // kv_cols))
    nxt = pl.BlockSpec((1, CHUNK, kv_cols),
                       lambda b, j: (b, jnp.minimum((seq_blk(j) + 1) * SEQ_CHUNKS, n - 1), AP_K // kv_cols))
    out = pl.pallas_call(
        functools.partial(_mixer_kernel, seq_rows=p, n_blocks=nblk),
        out_shape=jax.ShapeDtypeStruct((batch, p, D_MODEL), F32),
        grid=(batch, nblk + 1),
        in_specs=[smem, smem, seq(RP_WIDTH), tiles_spec, tiles_spec,
                  _resident((1, RET_WIDTH), layer),
                  seq(AP_WIDTH), prev, nxt,
                  lag(D_MODEL), lag(CONV_WIDTH), lag(GATE_WIDTH),
                  _resident((RET_WIDTH, D_MODEL), layer), _resident((CONV_WIDTH, D_MODEL), layer),
                  _resident((ATT_WIDTH, D_MODEL), layer), _resident((D_MODEL, D_MODEL), layer)],
        out_specs=lag(D_MODEL),
        scratch_shapes=[pltpu.VMEM((RET_HEADS,) + tile, F32),
                        pltpu.VMEM((RET_HEADS * N_TABS, CHUNK, CHUNK), F32),
                        pltpu.VMEM((2, tq, RET_WIDTH), BF16), pltpu.VMEM((2, tq, ATT_WIDTH), BF16)],
        compiler_params=_params("parallel", "arbitrary"),
        name="mixer",
    )(dec, sink, v3(proj["rp"]), rkt5, sbs5, gn_gain, ap3, ap3, ap3,
      v3(h), v3(proj["yc"]), v3(proj["gs"]), wr, wc, wa, wo)
    return out.reshape(rows, D_MODEL)


def _rope_tables(p, tile_rows):
    pos = (jnp.arange(p + tile_rows, dtype=jnp.int32) % p - PAD).astype(F32)[:, None]

    def tab(d):
        inv = ROPE_THETA ** (-jnp.arange(0, d, 2, dtype=F32) / d)
        ang = pos * inv[None, :]
        cos, sin = jnp.cos(ang), jnp.sin(ang)
        reps = LANES // d
        return (jnp.tile(jnp.concatenate([cos, cos], axis=1), (1, reps)),
                jnp.tile(jnp.concatenate([-sin, sin], axis=1), (1, reps)))

    cr, sr = tab(RET_DIM)
    ca, sa = tab(ATT_DIM)
    return cr, sr, ca, sa


def _pair_heads(t, axis):
    shape = t.shape
    t = t.reshape(shape[:axis] + (ATT_KV_HEADS, ATT_GROUPS, ATT_DIM) + shape[axis + 1:])
    t = jnp.swapaxes(t, axis, axis + 1)
    return t.reshape(shape)


def _prep_weights(w_in, w_attn_out):
    aq = _pair_heads(w_in[:, :, OFF_AQ:OFF_AK].astype(BF16), 2)
    w_in = lax.dynamic_update_slice(w_in.astype(BF16), aq, (0, 0, OFF_AQ))
    return w_in, _pair_heads(w_attn_out, 1).astype(BF16)


def _trunk(x, meta_tokens, w, final_norm):
    batch, s, _ = x.shape
    p = s + CHUNK
    rows = batch * p
    meta = jnp.broadcast_to(meta_tokens[None], (batch, N_META, D_MODEL))
    h = jnp.concatenate([jnp.zeros((batch, PAD, D_MODEL), F32), meta, x], axis=1).reshape(rows, D_MODEL)
    tabs = _rope_tables(p, _row_tile(rows, INPROJ_ROWS))
    depth = w["w_in"].shape[0]
    for l in range(depth):
        h = _ffn(h, l, w["norm_ffn1"], w["wg1"], w["wu1"], w["wd1"])
        proj = _inproj(h, l, w["ret_decay"][l], w["norm_mix"], w["w_in"], w["conv_w"], tabs, p)
        h = _mixer(h, l, w["ret_decay"][l], w["attn_sink"][l], w["ret_gn_gain"], w["w_ret_out"],
                   w["w_conv_out"], w["w_attn_out"], w["w_o"], proj, batch)
        if l + 1 < depth:
            h = _ffn(h, l, w["norm_ffn2"], w["wg2"], w["wu2"], w["wd2"])
    return _ffn_final(h.reshape(batch, p, D_MODEL), depth - 1, w["norm_ffn2"], w["wg2"], w["wu2"], w["wd2"],
                      final_norm.reshape(1, D_MODEL))


def kernel(x_prompt, x_sample, meta_tokens, norm_ffn1, w_ffn1_gate, w_ffn1_up, w_ffn1_down, norm_mix, w_in, ret_decay, ret_gn_gain, conv_w, attn_sink, w_ret_out, w_conv_out, w_attn_out, w_o, norm_ffn2, w_ffn2_gate, w_ffn2_up, w_ffn2_down, final_norm):
    depth = w_in.shape[0]
    w_in_b, w_attn_out_b = _prep_weights(w_in, w_attn_out)
    row = lambda t: t.reshape(depth, 1, t.shape[-1])
    w = dict(
        norm_ffn1=row(norm_ffn1), wg1=w_ffn1_gate.astype(BF16), wu1=w_ffn1_up.astype(BF16),
        wd1=w_ffn1_down.astype(BF16), norm_mix=row(norm_mix), w_in=w_in_b, ret_decay=ret_decay,
        ret_gn_gain=row(ret_gn_gain), conv_w=conv_w, attn_sink=attn_sink,
        w_ret_out=w_ret_out.astype(BF16), w_conv_out=w_conv_out.astype(BF16), w_attn_out=w_attn_out_b,
        w_o=w_o.astype(BF16), norm_ffn2=row(norm_ffn2), wg2=w_ffn2_gate.astype(BF16),
        wu2=w_ffn2_up.astype(BF16), wd2=w_ffn2_down.astype(BF16))
    y_prompt = _trunk(x_prompt, meta_tokens, w, final_norm)
    y_sample = _trunk(x_sample, meta_tokens, w, final_norm)
    return (y_prompt, y_sample)
```

```python
import functools
import math

import jax
import jax.numpy as jnp
from jax import lax
from jax.experimental import pallas as pl
from jax.experimental.pallas import tpu as pltpu

F32 = jnp.float32
BF16 = jnp.bfloat16

D_MODEL = 1024
D_FF = 2816
N_META = 16
CHUNK = 128
PAD = CHUNK - N_META
RET_HEADS = 4
RET_DIM = 128
RET_WIDTH = RET_HEADS * RET_DIM
CONV_WIDTH = 512
ATT_Q_HEADS = 8
ATT_KV_HEADS = 2
ATT_DIM = 64
ATT_GROUPS = ATT_Q_HEADS // ATT_KV_HEADS
ATT_WIDTH = ATT_Q_HEADS * ATT_DIM
ATT_KV_WIDTH = ATT_KV_HEADS * ATT_DIM
WINDOW = 128
GATE_WIDTH = 3 * D_MODEL
ROPE_THETA = 10000.0
EPS = 1e-6
NEG_INF = -1e30

LANES = 128
MXU_COLS = 256
HALO_ROWS = 16
FF_CHUNK = MXU_COLS
INPROJ_ROWS = 512
SEQ_CHUNKS = 4
VMEM_LIMIT = 56 * 1024 * 1024
LOG2E = math.log2(math.e)
ATT_Q_SCALE = ATT_DIM ** -0.5 * LOG2E

OFF_RQ = 0
OFF_RK = OFF_RQ + RET_WIDTH
OFF_RV = OFF_RK + RET_WIDTH
OFF_RG = OFF_RV + RET_WIDTH
OFF_CB = OFF_RG + RET_WIDTH
OFF_CC = OFF_CB + CONV_WIDTH
OFF_CX = OFF_CC + CONV_WIDTH
OFF_AQ = OFF_CX + CONV_WIDTH
OFF_AK = OFF_AQ + ATT_WIDTH
OFF_AV = OFF_AK + ATT_KV_WIDTH
OFF_GATE = OFF_AV + ATT_KV_WIDTH
IN_WIDTH = OFF_GATE + GATE_WIDTH

RP_Q = 0
RP_QD = RP_Q + RET_WIDTH
RP_V = RP_QD + 2 * RET_WIDTH
RP_VF = RP_V + RET_WIDTH
RP_G = RP_VF + RET_WIDTH
RP_WIDTH = RP_G + RET_WIDTH
AP_K = ATT_WIDTH
AP_V = AP_K + ATT_KV_WIDTH
AP_WIDTH = AP_V + ATT_KV_WIDTH


def _params(*sem):
    return pltpu.CompilerParams(dimension_semantics=sem, vmem_limit_bytes=VMEM_LIMIT)


def _resident(shape, layer=None):
    nd = len(shape)
    if layer is None:
        return pl.BlockSpec(shape, lambda *_: (0,) * nd, pipeline_mode=pl.Buffered(1))
    return pl.BlockSpec((None,) + tuple(shape), lambda *_: (layer,) + (0,) * nd, pipeline_mode=pl.Buffered(1))


def _row_tile(rows, cap):
    best = CHUNK
    t = CHUNK
    while t <= cap:
        if rows % t == 0:
            best = t
        t += CHUNK
    return best


def _rms(x, gain):
    ms = jnp.mean(x * x, axis=-1, keepdims=True)
    return x * lax.rsqrt(ms + EPS) * gain


def _dot(a, b):
    return jnp.dot(a, b, preferred_element_type=F32)


def _dot_nt(a, b):
    return lax.dot_general(a, b, (((1,), (1,)), ((), ())), preferred_element_type=F32)


def _ffn_body(x, g_ref, wg_ref, wu_ref, wd_ref, acc_ref):
    n = _rms(x, g_ref[...]).astype(BF16)
    for c in range(D_FF // FF_CHUNK):
        sl = slice(c * FF_CHUNK, (c + 1) * FF_CHUNK)
        g = _dot(n, wg_ref[:, sl])
        u = _dot(n, wu_ref[:, sl])
        a = (g * jax.nn.sigmoid(g) * u).astype(BF16)
        d = _dot(a, wd_ref[sl, :])
        if c == 0:
            acc_ref[...] = d
        else:
            acc_ref[...] += d
    return x + 0.5 * acc_ref[...]


def _ffn_kernel(x_ref, g_ref, wg_ref, wu_ref, wd_ref, o_ref, acc_ref):
    o_ref[...] = _ffn_body(x_ref[...], g_ref, wg_ref, wu_ref, wd_ref, acc_ref)


def _ffn_final_kernel(x_ref, g_ref, wg_ref, wu_ref, wd_ref, fg_ref, o_ref, acc_ref):
    o_ref[...] = _rms(_ffn_body(x_ref[0], g_ref, wg_ref, wu_ref, wd_ref, acc_ref), fg_ref[...])


def _ffn_weight_specs(layer):
    return [_resident((1, D_MODEL), layer), _resident((D_MODEL, D_FF), layer), _resident((D_MODEL, D_FF), layer),
            _resident((D_FF, D_MODEL), layer)]


def _ffn(h, layer, gain, wg, wu, wd):
    rows = h.shape[0]
    tm = _row_tile(rows, 1024)
    return pl.pallas_call(
        _ffn_kernel,
        out_shape=jax.ShapeDtypeStruct((rows, D_MODEL), F32),
        grid=(rows // tm,),
        in_specs=[pl.BlockSpec((tm, D_MODEL), lambda i: (i, 0))] + _ffn_weight_specs(layer),
        out_specs=pl.BlockSpec((tm, D_MODEL), lambda i: (i, 0)),
        scratch_shapes=[pltpu.VMEM((tm, D_MODEL), F32)],
        compiler_params=_params("parallel"),
        name="ffn",
    )(h, gain, wg, wu, wd)


def _ffn_final(h3, layer, gain, wg, wu, wd, final_gain):
    batch, p, _ = h3.shape
    s = p - CHUNK
    tm = _row_tile(s, 1024)
    return pl.pallas_call(
        _ffn_final_kernel,
        out_shape=jax.ShapeDtypeStruct((batch, s, D_MODEL), F32),
        grid=(batch, s // tm),
        in_specs=[pl.BlockSpec((pl.Element(1), pl.Element(tm), pl.Element(D_MODEL)),
                               lambda b, j: (b, pl.multiple_of(CHUNK + j * tm, CHUNK), 0))]
        + _ffn_weight_specs(layer) + [_resident((1, D_MODEL))],
        out_specs=pl.BlockSpec((None, tm, D_MODEL), lambda b, j: (b, j, 0)),
        scratch_shapes=[pltpu.VMEM((tm, D_MODEL), F32)],
        compiler_params=_params("parallel", "parallel"),
        name="ffn_final",
    )(h3, gain, wg, wu, wd, final_gain)


def _rot128(x, cos, sin):
    return x * cos + pltpu.roll(x, 64, axis=1) * sin


def _rot64(x, cos, sin, low_half):
    partner = jnp.where(low_half, pltpu.roll(x, 96, axis=1), pltpu.roll(x, 32, axis=1))
    return x * cos + partner * sin


def _log_gamma_tile(dec_ref, direction, hd):
    return jnp.log1p(-jnp.exp(jnp.full((CHUNK, CHUNK), dec_ref[direction, hd], F32)))


D_QF, D_QB, D_VF, D_VB, D_CB = range(5)


def _row_decay_tables(dec_ref, tab_ref):
    i = lax.broadcasted_iota(jnp.int32, (CHUNK, CHUNK), 0).astype(F32)
    for hd in range(RET_HEADS):
        sl = slice(hd * RET_DIM, (hd + 1) * RET_DIM)
        lgf = _log_gamma_tile(dec_ref, 0, hd)
        lgb = _log_gamma_tile(dec_ref, 1, hd)
        tab_ref[D_QF, :, sl] = jnp.exp(lgf * (i + 1.0))
        tab_ref[D_QB, :, sl] = jnp.exp(lgb * (CHUNK - i))
        tab_ref[D_VF, :, sl] = jnp.exp(lgf * (CHUNK - 1.0 - i))
        tab_ref[D_VB, :, sl] = jnp.exp(lgb * i)
        tab_ref[D_CB, :, sl] = jnp.exp(lgb * CHUNK)


def _inproj_kernel(dec_ref, x_ref, xp_ref, xn_ref, g_ref, w_ref, cw_ref, cr_ref, sr_ref, ca_ref, sa_ref,
                   rp_ref, rkt_ref, sbs_ref, yc_ref, ap_ref, gs_ref, dtab_ref, us_ref, sb_ref,
                   *, total_rows, seq_rows):
    i = pl.num_programs(0) - 1 - pl.program_id(0)
    tm = x_ref.shape[0]

    @pl.when(pl.program_id(0) == 0)
    def _():
        _row_decay_tables(dec_ref, dtab_ref)
        sb_ref[...] = jnp.zeros_like(sb_ref)

    gain = g_ref[...]
    n = _rms(x_ref[...], gain).astype(BF16)

    def proj(off, width):
        return _dot(n, w_ref[:, off:off + width])

    cr, sr = cr_ref[...], sr_ref[...]
    ca, sa = ca_ref[...], sa_ref[...]
    low_half = (lax.broadcasted_iota(jnp.int32, (1, LANES), 1) % ATT_DIM) < (ATT_DIM // 2)
    chunks = [slice(ch * CHUNK, (ch + 1) * CHUNK) for ch in range(tm // CHUNK)]
    kts = {}
    for c in range(RET_WIDTH // MXU_COLS):
        zq = proj(OFF_RQ + c * MXU_COLS, MXU_COLS)
        zk = proj(OFF_RK + c * MXU_COLS, MXU_COLS)
        for half in range(MXU_COLS // RET_DIM):
            hd = c * (MXU_COLS // RET_DIM) + half
            src = slice(half * RET_DIM, (half + 1) * RET_DIM)
            dst = slice(hd * RET_DIM, (hd + 1) * RET_DIM)
            q = _rot128(zq[:, src], cr, sr)
            k = _rot128(zk[:, src], cr, sr) * (RET_DIM ** -0.5)
            rp_ref[:, RP_Q + hd * RET_DIM:RP_Q + (hd + 1) * RET_DIM] = q.astype(BF16)
            for ch, rows in enumerate(chunks):
                lo = RP_QD + 2 * hd * RET_DIM
                rp_ref[rows, lo:lo + RET_DIM] = (q[rows] * dtab_ref[D_QF, :, dst]).astype(BF16)
                rp_ref[rows, lo + RET_DIM:lo + 2 * RET_DIM] = (q[rows] * dtab_ref[D_QB, :, dst]).astype(BF16)
                kts[(ch, hd)] = k[rows].T.astype(BF16)
                rkt_ref[ch, hd] = kts[(ch, hd)]
    zv = proj(OFF_RV, RET_WIDTH)
    rp_ref[:, RP_V:RP_V + RET_WIDTH] = zv.astype(BF16)
    for rows in chunks:
        rp_ref[rows, RP_VF:RP_VF + RET_WIDTH] = (zv[rows] * dtab_ref[D_VF]).astype(BF16)

    seq_chunks = seq_rows // CHUNK

    def sweep_item(ch, hd):
        def run():
            sl = slice(hd * RET_DIM, (hd + 1) * RET_DIM)
            ends_sequence = lax.rem(i * len(chunks) + ch, seq_chunks) == seq_chunks - 1
            vdb = (zv[chunks[ch], sl] * dtab_ref[D_VB, :, sl]).astype(BF16)
            later = jnp.where(ends_sequence, 0.0, sb_ref[hd])
            sbs_ref[ch, hd] = later.astype(BF16)
            sb_ref[hd] = dtab_ref[D_CB, :, sl] * later + _dot(kts[(ch, hd)], vdb)
        return run

    def gate_item():
        zg = proj(OFF_RG, RET_WIDTH)
        rp_ref[:, RP_G:RP_G + RET_WIDTH] = (zg * jax.nn.sigmoid(zg)).astype(BF16)

    def conv_item():
        n_ext = jnp.concatenate([_rms(xp_ref[...], gain).astype(BF16), n, _rms(xn_ref[...], gain).astype(BF16)],
                                axis=0)
        u = _dot(n_ext, w_ref[:, OFF_CC:OFF_CC + CONV_WIDTH]) * _dot(n_ext, w_ref[:, OFF_CX:OFF_CX + CONV_WIDTH])
        grow = i * tm - HALO_ROWS + lax.broadcasted_iota(jnp.int32, (tm + 2 * HALO_ROWS, 1), 0)
        us_ref[...] = jnp.where((grow >= 0) & (grow < total_rows), u, 0.0)
        conv = (us_ref[pl.ds(HALO_ROWS - 1, tm), :] * cw_ref[0:1, :]
                + us_ref[pl.ds(HALO_ROWS, tm), :] * cw_ref[1:2, :]
                + us_ref[pl.ds(HALO_ROWS + 1, tm), :] * cw_ref[2:3, :])
        yc_ref[...] = (proj(OFF_CB, CONV_WIDTH) * conv).astype(BF16)

    def query_item(c):
        def run():
            zq = proj(OFF_AQ + c * MXU_COLS, MXU_COLS)
            for half in range(MXU_COLS // LANES):
                src = slice(half * LANES, (half + 1) * LANES)
                dst = slice(c * MXU_COLS + half * LANES, c * MXU_COLS + (half + 1) * LANES)
                ap_ref[:, dst] = (_rot64(zq[:, src], ca, sa, low_half) * ATT_Q_SCALE).astype(BF16)
        return run

    def kv_item():
        zkv = proj(OFF_AK, 2 * ATT_KV_WIDTH)
        ap_ref[:, AP_K:AP_K + ATT_KV_WIDTH] = _rot64(zkv[:, :ATT_KV_WIDTH], ca, sa, low_half).astype(BF16)
        ap_ref[:, AP_V:AP_V + ATT_KV_WIDTH] = zkv[:, ATT_KV_WIDTH:].astype(BF16)

    def merge_gate_item(c):
        def run():
            sl = slice(c * 512, (c + 1) * 512)
            gs_ref[:, sl] = jax.nn.sigmoid(proj(OFF_GATE + c * 512, 512)).astype(BF16)
        return run

    projections = ([gate_item, conv_item] + [query_item(c) for c in range(ATT_WIDTH // MXU_COLS)] + [kv_item]
                   + [merge_gate_item(c) for c in range(GATE_WIDTH // 512)])
    sweep = [sweep_item(ch, hd) for ch in reversed(range(len(chunks))) for hd in range(RET_HEADS)]
    _interleave(projections, sweep)


INPROJ_OUT = ("rp", "rkt", "sbs", "yc", "ap", "gs")
INPROJ_TILES = ("rkt", "sbs")


def _inproj(h, layer, dec, gain, w_in, conv_w, tabs, seq_rows):
    rows = h.shape[0]
    tm = _row_tile(rows, INPROJ_ROWS)
    nt = rows // tm
    hb = tm // HALO_ROWS
    tile_of = lambda i: nt - 1 - i
    row = lambda w: pl.BlockSpec((tm, w), lambda i: (tile_of(i), 0))
    tab = pl.BlockSpec((pl.Element(tm), pl.Element(LANES)),
                       lambda i: (pl.multiple_of((tile_of(i) * tm) % seq_rows, CHUNK), 0))
    halo_prev = pl.BlockSpec((HALO_ROWS, D_MODEL), lambda i: (jnp.maximum(tile_of(i) * hb - 1, 0), 0))
    halo_next = pl.BlockSpec((HALO_ROWS, D_MODEL),
                              lambda i: (jnp.minimum((tile_of(i) + 1) * hb, rows // HALO_ROWS - 1), 0))
    widths = dict(rp=RP_WIDTH, yc=CONV_WIDTH, ap=AP_WIDTH, gs=GATE_WIDTH)
    tiles_shape = (rows // CHUNK, RET_HEADS, RET_DIM, RET_DIM)
    tiles_spec = pl.BlockSpec((tm // CHUNK,) + tiles_shape[1:], lambda i: (tile_of(i), 0, 0, 0))
    outs = pl.pallas_call(
        functools.partial(_inproj_kernel, total_rows=rows, seq_rows=seq_rows),
        out_shape=[jax.ShapeDtypeStruct(tiles_shape if name in INPROJ_TILES else (rows, widths[name]), BF16)
                   for name in INPROJ_OUT],
        grid=(nt,),
        in_specs=[pl.BlockSpec(memory_space=pltpu.SMEM), row(D_MODEL), halo_prev, halo_next,
                  _resident((1, D_MODEL), layer), _resident((D_MODEL, IN_WIDTH), layer),
                  _resident((3, CONV_WIDTH), layer), tab, tab, tab, tab],
        out_specs=[tiles_spec if name in INPROJ_TILES else row(widths[name]) for name in INPROJ_OUT],
        scratch_shapes=[pltpu.VMEM((5, CHUNK, RET_WIDTH), F32),
                        pltpu.VMEM((tm + 2 * HALO_ROWS, CONV_WIDTH), F32),
                        pltpu.VMEM((RET_HEADS, RET_DIM, RET_DIM), F32)],
        compiler_params=_params("arbitrary"),
        name="inproj",
    )(dec, h, h, h, gain, w_in, conv_w, *tabs)
    return dict(zip(INPROJ_OUT, outs))


T_CF, T_MASK = range(2)
N_TABS = 2


def _chunk_tables(dec_ref, tab_ref):
    i = lax.broadcasted_iota(jnp.int32, (CHUNK, CHUNK), 0).astype(F32)
    j = lax.broadcasted_iota(jnp.int32, (CHUNK, CHUNK), 1).astype(F32)
    diff = i - j
    for hd in range(RET_HEADS):
        lgf = _log_gamma_tile(dec_ref, 0, hd)
        lgb = _log_gamma_tile(dec_ref, 1, hd)
        base = hd * N_TABS
        tab_ref[base + T_CF] = jnp.exp(lgf * CHUNK)
        tab_ref[base + T_MASK] = jnp.where(diff >= 0, jnp.exp(lgf * jnp.maximum(diff, 0.0)),
                                           jnp.exp(lgb * jnp.maximum(-diff, 0.0)))


def _state_update(s_ref, tab_ref, hd, kt, vd, t_c):
    s_ref[hd] = tab_ref[hd * N_TABS + t_c] * s_ref[hd] + _dot(kt, vd)


def _ret_items(rp_ref, kt_ref, sbs_ref, gain_ref, sf_ref, tab_ref, jj, out):
    rows = slice(jj * CHUNK, (jj + 1) * CHUNK)

    def head_item(hd):
        def run():
            sl = slice(hd * RET_DIM, (hd + 1) * RET_DIM)
            part = lambda off, width=RET_DIM: rp_ref[0, rows, off + hd * width:off + (hd + 1) * width]
            kt = kt_ref[0, jj, hd]
            att = _dot(part(RP_Q), kt) * tab_ref[hd * N_TABS + T_MASK]
            o = _dot(att.astype(BF16), part(RP_V))
            sd = jnp.concatenate([sf_ref[hd].astype(BF16), sbs_ref[0, jj, hd]], axis=0)
            o += _dot(part(RP_QD, 2 * RET_DIM), sd)
            _state_update(sf_ref, tab_ref, hd, kt, part(RP_VF), T_CF)
            mu = jnp.mean(o, axis=-1, keepdims=True)
            oc = o - mu
            var = jnp.mean(oc * oc, axis=-1, keepdims=True)
            on = oc * lax.rsqrt(var + EPS) * gain_ref[:, sl]
            out[(jj, hd)] = (part(RP_G).astype(F32) * on).astype(BF16)
        return run

    return [head_item(hd) for hd in range(RET_HEADS)]


def _attn_window(kvp_ref, ap_ref, kvn_ref, blk, seq_rows):
    tq = SEQ_CHUNKS * CHUNK
    span = tq + 2 * CHUNK
    head0 = lax.broadcasted_iota(jnp.int32, (1, LANES), 1) < ATT_DIM
    wpos = blk * tq - CHUNK + lax.broadcasted_iota(jnp.int32, (span, 1), 0)
    inside = (wpos >= 0) & (wpos < seq_rows)
    kw = ATT_KV_WIDTH
    kall = jnp.concatenate([kvp_ref[0, :, :kw], ap_ref[0, :, AP_K:AP_K + kw], kvn_ref[0, :, :kw]], axis=0)
    vall = jnp.concatenate([kvp_ref[0, :, kw:], ap_ref[0, :, AP_V:AP_V + kw], kvn_ref[0, :, kw:]], axis=0)
    zero = jnp.zeros_like(kall)
    kall = jnp.where(inside, kall, zero)
    vall = jnp.where(inside, vall, zero)
    return (jnp.where(head0, kall, zero), jnp.where(head0, zero, kall),
            jnp.where(head0, vall, zero), jnp.where(head0, zero, vall))


def _attn_items(sink_ref, q_ref, window, blk, jj, seq_rows, out):
    k0, k1, v0, v1 = window
    base = jj * CHUNK
    st = {}

    def scores():
        r = lax.broadcasted_iota(jnp.int32, (CHUNK, 3 * CHUNK), 0)
        j = lax.broadcasted_iota(jnp.int32, (CHUNK, 3 * CHUNK), 1)
        kpos = (blk * SEQ_CHUNKS + jj - 1) * CHUNK + j
        st["ok"] = (jnp.abs(CHUNK + r - j) <= WINDOW) & (kpos >= PAD) & (kpos < seq_rows)
        kb = jnp.concatenate([k0[base:base + 3 * CHUNK], k1[base:base + 3 * CHUNK]], axis=0)
        qs = jnp.concatenate([q_ref[0, base:base + CHUNK, c * LANES:(c + 1) * LANES]
                              for c in range(ATT_GROUPS)], axis=0)
        st["s"] = _dot_nt(qs, kb)

    def softmax_item(c):
        def run():
            head0 = lax.broadcasted_iota(jnp.int32, (1, LANES), 1) < ATT_DIM
            e_cols, inv_cols = [], []
            for kv in range(ATT_KV_HEADS):
                s = st["s"][c * CHUNK:(c + 1) * CHUNK, kv * 3 * CHUNK:(kv + 1) * 3 * CHUNK]
                s = jnp.where(st["ok"], s, NEG_INF)
                sink = sink_ref[kv * ATT_GROUPS + c] * LOG2E
                m = jnp.maximum(jnp.max(s, axis=-1, keepdims=True), sink)
                e = jnp.exp2(s - m)
                denom = jnp.sum(e, axis=-1, keepdims=True) + jnp.exp2(sink - m)
                e_cols.append(e.astype(BF16))
                inv_cols.append(1.0 / denom)
            st[("e", c)] = jnp.concatenate(e_cols, axis=1)
            st[("inv", c)] = jnp.where(head0, inv_cols[0], inv_cols[1])
        return run

    def values():
        vb = jnp.concatenate([v0[base:base + 3 * CHUNK], v1[base:base + 3 * CHUNK]], axis=0)
        o = _dot(jnp.concatenate([st[("e", c)] for c in range(ATT_GROUPS)], axis=0), vb)
        out[jj] = jnp.concatenate([(o[c * CHUNK:(c + 1) * CHUNK] * st[("inv", c)]).astype(BF16)
                                   for c in range(ATT_GROUPS)], axis=1)

    return [scores] + [softmax_item(c) for c in range(ATT_GROUPS)] + [values]


def _merge_items(yr_ref, yc_ref, ya_ref, gs_ref, h_ref, o_ref, wr_ref, wc_ref, wa_ref, wo_ref, row0):
    st = {}
    n_tiles = D_MODEL // MXU_COLS
    rows = h_ref.shape[1]

    def branch_item(nt, b, y_of, w_ref):
        def run():
            cols = slice(nt * MXU_COLS, (nt + 1) * MXU_COLS)
            gate = gs_ref[0, :, b * D_MODEL + nt * MXU_COLS:b * D_MODEL + (nt + 1) * MXU_COLS]
            term = gate.astype(F32) * _dot(y_of(), w_ref[:, cols])
            acc = term if b == 0 else st[nt] + term
            st[nt] = acc.astype(BF16) if b == len(branches) - 1 else acc
        return run

    def out_item(nt):
        def run():
            cols = slice(nt * MXU_COLS, (nt + 1) * MXU_COLS)
            merged = jnp.concatenate([st[k] for k in range(n_tiles)], axis=1)
            mix = _dot(merged, wo_ref[:, cols])
            pos = row0 + lax.broadcasted_iota(jnp.int32, (rows, 1), 0)
            o_ref[0, :, cols] = h_ref[0, :, cols] + jnp.where(pos < PAD, 0.0, mix)
        return run

    branches = ((lambda: yr_ref[...], wr_ref), (lambda: yc_ref[0], wc_ref), (lambda: ya_ref[...], wa_ref))
    items = [branch_item(nt, b, y_of, w_ref) for nt in range(n_tiles) for b, (y_of, w_ref) in enumerate(branches)]
    return items + [out_item(nt) for nt in range(n_tiles)]


def _interleave(major, minor):
    done = 0
    for idx, item in enumerate(major):
        item()
        want = (idx + 1) * len(minor) // len(major)
        while done < want:
            minor[done]()
            done += 1


def _mixer_kernel(dec_ref, sink_ref, rp_ref, rkt_ref, sbs_ref, gain_ref, ap_ref, kvp_ref, kvn_ref,
                  h_ref, yc_ref, gs_ref, wr_ref, wc_ref, wa_ref, wo_ref,
                  o_ref, sf_ref, tab_ref, yr_ref, ya_ref, *, seq_rows, n_blocks):
    j = pl.program_id(1)
    tq = SEQ_CHUNKS * CHUNK

    @pl.when(j == 0)
    def _():
        sf_ref[...] = jnp.zeros_like(sf_ref)
        _chunk_tables(dec_ref, tab_ref)

    def seq_items():
        window = _attn_window(kvp_ref, ap_ref, kvn_ref, j, seq_rows)
        yr_parts, ya_parts = {}, {}
        items = []
        for jj in range(SEQ_CHUNKS):
            ret = _ret_items(rp_ref, rkt_ref, sbs_ref, gain_ref, sf_ref, tab_ref, jj, yr_parts)
            att = _attn_items(sink_ref, ap_ref, window, j, jj, seq_rows, ya_parts)
            items += [att[0], ret[0], att[1], ret[1], att[2], att[3], ret[2], att[4], ret[3], att[5]]

        def park():
            for jj in range(SEQ_CHUNKS):
                rows = slice(jj * CHUNK, (jj + 1) * CHUNK)
                ya_ref[j % 2, rows, :] = ya_parts[jj]
                for hd in range(RET_HEADS):
                    yr_ref[j % 2, rows, hd * RET_DIM:(hd + 1) * RET_DIM] = yr_parts[(jj, hd)]
        return items, park

    def merge_items():
        slot = (j + 1) % 2
        return _merge_items(yr_ref.at[slot], yc_ref, ya_ref.at[slot], gs_ref, h_ref, o_ref,
                            wr_ref, wc_ref, wa_ref, wo_ref, (j - 1) * tq)

    @pl.when(j == 0)
    def _():
        items, park = seq_items()
        for item in items:
            item()
        park()

    @pl.when((j > 0) & (j < n_blocks))
    def _():
        items, park = seq_items()
        _interleave(items, merge_items())
        park()

    @pl.when(j == n_blocks)
    def _():
        for item in merge_items():
            item()


def _mixer(h, layer, dec, sink, gn_gain, wr, wc, wa, wo, proj, batch):
    rows = h.shape[0]
    p = rows // batch
    n = p // CHUNK
    tq = SEQ_CHUNKS * CHUNK
    nblk = pl.cdiv(n, SEQ_CHUNKS)
    v3 = lambda t: t.reshape(batch, p, t.shape[-1])
    tile = (RET_DIM, RET_DIM)
    rkt5 = proj["rkt"].reshape((batch, n, RET_HEADS) + tile)
    sbs5 = proj["sbs"].reshape((batch, n, RET_HEADS) + tile)
    ap3 = v3(proj["ap"])
    tiles_spec = pl.BlockSpec((1, SEQ_CHUNKS, RET_HEADS) + tile, lambda b, j: (b, seq_blk(j), 0, 0, 0))

    smem = pl.BlockSpec(memory_space=pltpu.SMEM)
    seq_blk = lambda j: jnp.minimum(j, nblk - 1)
    seq = lambda w: pl.BlockSpec((1, tq, w), lambda b, j: (b, seq_blk(j), 0))
    lag = lambda w: pl.BlockSpec((1, tq, w), lambda b, j: (b, jnp.maximum(j - 1, 0), 0))
    kv_cols = 2 * ATT_KV_WIDTH
    prev = pl.BlockSpec((1, CHUNK, kv_cols),
                        lambda b, j: (b, jnp.maximum(seq_blk(j) * SEQ_CHUNKS - 1, 0), AP_K // kv_cols))
    nxt = pl.BlockSpec((1, CHUNK, kv_cols),
                       lambda b, j: (b, jnp.minimum((seq_blk(j) + 1) * SEQ_CHUNKS, n - 1), AP_K // kv_cols))
    out = pl.pallas_call(
        functools.partial(_mixer_kernel, seq_rows=p, n_blocks=nblk),
        out_shape=jax.ShapeDtypeStruct((batch, p, D_MODEL), F32),
        grid=(batch, nblk + 1),
        in_specs=[smem, smem, seq(RP_WIDTH), tiles_spec, tiles_spec,
                  _resident((1, RET_WIDTH), layer),
                  seq(AP_WIDTH), prev, nxt,
                  lag(D_MODEL), lag(CONV_WIDTH), lag(GATE_WIDTH),
                  _resident((RET_WIDTH, D_MODEL), layer), _resident((CONV_WIDTH, D_MODEL), layer),
                  _resident((ATT_WIDTH, D_MODEL), layer), _resident((D_MODEL, D_MODEL), layer)],
        out_specs=lag(D_MODEL),
        scratch_shapes=[pltpu.VMEM((RET_HEADS,) + tile, F32),
                        pltpu.VMEM((RET_HEADS * N_TABS, CHUNK, CHUNK), F32),
                        pltpu.VMEM((2, tq, RET_WIDTH), BF16), pltpu.VMEM((2, tq, ATT_WIDTH), BF16)],
        compiler_params=_params("parallel", "arbitrary"),
        name="mixer",
    )(dec, sink, v3(proj["rp"]), rkt5, sbs5, gn_gain, ap3, ap3, ap3,
      v3(h), v3(proj["yc"]), v3(proj["gs"]), wr, wc, wa, wo)
    return out.reshape(rows, D_MODEL)


def _rope_tables(p, tile_rows):
    pos = (jnp.arange(p + tile_rows, dtype=jnp.int32) % p - PAD).astype(F32)[:, None]

    def tab(d):
        inv = ROPE_THETA ** (-jnp.arange(0, d, 2, dtype=F32) / d)
        ang = pos * inv[None, :]
        cos, sin = jnp.cos(ang), jnp.sin(ang)
        reps = LANES // d
        return (jnp.tile(jnp.concatenate([cos, cos], axis=1), (1, reps)),
                jnp.tile(jnp.concatenate([-sin, sin], axis=1), (1, reps)))

    cr, sr = tab(RET_DIM)
    ca, sa = tab(ATT_DIM)
    return cr, sr, ca, sa


def _pair_heads(t, axis):
    shape = t.shape
    t = t.reshape(shape[:axis] + (ATT_KV_HEADS, ATT_GROUPS, ATT_DIM) + shape[axis + 1:])
    t = jnp.swapaxes(t, axis, axis + 1)
    return t.reshape(shape)


def _prep_weights(w_in, w_attn_out):
    aq = _pair_heads(w_in[:, :, OFF_AQ:OFF_AK].astype(BF16), 2)
    w_in = lax.dynamic_update_slice(w_in.astype(BF16), aq, (0, 0, OFF_AQ))
    return w_in, _pair_heads(w_attn_out, 1).astype(BF16)


def _trunk(x, meta_tokens, w, final_norm):
    batch, s, _ = x.shape
    p = s + CHUNK
    rows = batch * p
    meta = jnp.broadcast_to(meta_tokens[None], (batch, N_META, D_MODEL))
    h = jnp.concatenate([jnp.zeros((batch, PAD, D_MODEL), F32), meta, x], axis=1).reshape(rows, D_MODEL)
    tabs = _rope_tables(p, _row_tile(rows, INPROJ_ROWS))
    depth = w["w_in"].shape[0]
    for l in range(depth):
        h = _ffn(h, l, w["norm_ffn1"], w["wg1"], w["wu1"], w["wd1"])
        proj = _inproj(h, l, w["ret_decay"][l], w["norm_mix"], w["w_in"], w["conv_w"], tabs, p)
        h = _mixer(h, l, w["ret_decay"][l], w["attn_sink"][l], w["ret_gn_gain"], w["w_ret_out"],
                   w["w_conv_out"], w["w_attn_out"], w["w_o"], proj, batch)
        if l + 1 < depth:
            h = _ffn(h, l, w["norm_ffn2"], w["wg2"], w["wu2"], w["wd2"])
    return _ffn_final(h.reshape(batch, p, D_MODEL), depth - 1, w["norm_ffn2"], w["wg2"], w["wu2"], w["wd2"],
                      final_norm.reshape(1, D_MODEL))


def kernel(x_prompt, x_sample, meta_tokens, norm_ffn1, w_ffn1_gate, w_ffn1_up, w_ffn1_down, norm_mix, w_in, ret_decay, ret_gn_gain, conv_w, attn_sink, w_ret_out, w_conv_out, w_attn_out, w_o, norm_ffn2, w_ffn2_gate, w_ffn2_up, w_ffn2_down, final_norm):
    depth = w_in.shape[0]
    w_in_b, w_attn_out_b = _prep_weights(w_in, w_attn_out)
    row = lambda t: t.reshape(depth, 1, t.shape[-1])
    w = dict(
        norm_ffn1=row(norm_ffn1), wg1=w_ffn1_gate.astype(BF16), wu1=w_ffn1_up.astype(BF16),
        wd1=w_ffn1_down.astype(BF16), norm_mix=row(norm_mix), w_in=w_in_b, ret_decay=ret_decay,
        ret_gn_gain=row(ret_gn_gain), conv_w=conv_w, attn_sink=attn_sink,
        w_ret_out=w_ret_out.astype(BF16), w_conv_out=w_conv_out.astype(BF16), w_attn_out=w_attn_out_b,
        w_o=w_o.astype(BF16), norm_ffn2=row(norm_ffn2), wg2=w_ffn2_gate.astype(BF16),
        wu2=w_ffn2_up.astype(BF16), wd2=w_ffn2_down.astype(BF16))
    y_prompt = _trunk(x_prompt, meta_tokens, w, final_norm)
    y_sample = _trunk(x_sample, meta_tokens, w, final_norm)
    return (y_prompt, y_sample)
```

```python
import functools
import math

import jax
import jax.numpy as jnp
from jax import lax
from jax.experimental import pallas as pl
from jax.experimental.pallas import tpu as pltpu

F32 = jnp.float32
BF16 = jnp.bfloat16

D_MODEL = 1024
D_FF = 2816
N_META = 16
CHUNK = 128
PAD = CHUNK - N_META
RET_HEADS = 4
RET_DIM = 128
RET_WIDTH = RET_HEADS * RET_DIM
CONV_WIDTH = 512
ATT_Q_HEADS = 8
ATT_KV_HEADS = 2
ATT_DIM = 64
ATT_GROUPS = ATT_Q_HEADS // ATT_KV_HEADS
ATT_WIDTH = ATT_Q_HEADS * ATT_DIM
ATT_KV_WIDTH = ATT_KV_HEADS * ATT_DIM
WINDOW = 128
GATE_WIDTH = 3 * D_MODEL
ROPE_THETA = 10000.0
EPS = 1e-6
NEG_INF = -1e30

LANES = 128
MXU_COLS = 256
HALO_ROWS = 16
FF_CHUNK = MXU_COLS
FFN_ROWS = 1024
INPROJ_ROWS = 512
GATE_CHUNK = 2 * MXU_COLS
SEQ_CHUNKS = 4
VMEM_LIMIT = 56 * 1024 * 1024
LOG2E = math.log2(math.e)
ATT_Q_SCALE = ATT_DIM ** -0.5 * LOG2E

OFF_RQ = 0
OFF_RK = OFF_RQ + RET_WIDTH
OFF_RV = OFF_RK + RET_WIDTH
OFF_RG = OFF_RV + RET_WIDTH
OFF_CB = OFF_RG + RET_WIDTH
OFF_CC = OFF_CB + CONV_WIDTH
OFF_CX = OFF_CC + CONV_WIDTH
OFF_AQ = OFF_CX + CONV_WIDTH
OFF_AK = OFF_AQ + ATT_WIDTH
OFF_AV = OFF_AK + ATT_KV_WIDTH
OFF_GATE = OFF_AV + ATT_KV_WIDTH
IN_WIDTH = OFF_GATE + GATE_WIDTH

RP_Q = 0
RP_QD = RP_Q + RET_WIDTH
RP_V = RP_QD + 2 * RET_WIDTH
RP_VF = RP_V + RET_WIDTH
RP_G = RP_VF + RET_WIDTH
RP_WIDTH = RP_G + RET_WIDTH
AP_K = ATT_WIDTH
AP_WIDTH = AP_K + ATT_KV_WIDTH


def _params(*sem):
    return pltpu.CompilerParams(dimension_semantics=sem, vmem_limit_bytes=VMEM_LIMIT)


def _resident(shape, layer=None):
    nd = len(shape)
    if layer is None:
        return pl.BlockSpec(shape, lambda *_: (0,) * nd, pipeline_mode=pl.Buffered(1))
    return pl.BlockSpec((None,) + tuple(shape), lambda *_: (layer,) + (0,) * nd, pipeline_mode=pl.Buffered(1))


def _row_tile(rows, cap):
    best = CHUNK
    t = CHUNK
    while t <= cap:
        if rows % t == 0:
            best = t
        t += CHUNK
    return best


def _rms(x, gain):
    ms = jnp.mean(x * x, axis=-1, keepdims=True)
    return x * lax.rsqrt(ms + EPS) * gain


def _dot(a, b):
    return jnp.dot(a, b, preferred_element_type=F32)


def _dot_nt(a, b):
    return lax.dot_general(a, b, (((1,), (1,)), ((), ())), preferred_element_type=F32)


def _ffn_body(x, g_ref, wg_ref, wu_ref, wd_ref, acc_ref):
    n = _rms(x, g_ref[...]).astype(BF16)
    for c in range(D_FF // FF_CHUNK):
        sl = slice(c * FF_CHUNK, (c + 1) * FF_CHUNK)
        g = _dot(n, wg_ref[:, sl])
        u = _dot(n, wu_ref[:, sl])
        a = (g * jax.nn.sigmoid(g) * u).astype(BF16)
        d = _dot(a, wd_ref[sl, :])
        if c == 0:
            acc_ref[...] = d
        else:
            acc_ref[...] += d
    return x + 0.5 * acc_ref[...]


def _ffn_kernel(x_ref, g_ref, wg_ref, wu_ref, wd_ref, o_ref, acc_ref):
    o_ref[...] = _ffn_body(x_ref[...], g_ref, wg_ref, wu_ref, wd_ref, acc_ref)


def _ffn_final_kernel(x_ref, g_ref, wg_ref, wu_ref, wd_ref, fg_ref, o_ref, acc_ref):
    o_ref[...] = _rms(_ffn_body(x_ref[0], g_ref, wg_ref, wu_ref, wd_ref, acc_ref), fg_ref[...])


def _ffn_weight_specs(layer):
    return [_resident((1, D_MODEL), layer), _resident((D_MODEL, D_FF), layer), _resident((D_MODEL, D_FF), layer),
            _resident((D_FF, D_MODEL), layer)]


def _ffn(h, layer, gain, wg, wu, wd):
    rows = h.shape[0]
    tm = _row_tile(rows, FFN_ROWS)
    return pl.pallas_call(
        _ffn_kernel,
        out_shape=jax.ShapeDtypeStruct((rows, D_MODEL), F32),
        grid=(rows // tm,),
        in_specs=[pl.BlockSpec((tm, D_MODEL), lambda i: (i, 0))] + _ffn_weight_specs(layer),
        out_specs=pl.BlockSpec((tm, D_MODEL), lambda i: (i, 0)),
        scratch_shapes=[pltpu.VMEM((tm, D_MODEL), F32)],
        compiler_params=_params("parallel"),
        name="ffn",
    )(h, gain, wg, wu, wd)


def _ffn_final(h3, layer, gain, wg, wu, wd, final_gain):
    batch, p, _ = h3.shape
    s = p - CHUNK
    tm = _row_tile(s, FFN_ROWS)
    return pl.pallas_call(
        _ffn_final_kernel,
        out_shape=jax.ShapeDtypeStruct((batch, s, D_MODEL), F32),
        grid=(batch, s // tm),
        in_specs=[pl.BlockSpec((pl.Element(1), pl.Element(tm), pl.Element(D_MODEL)),
                               lambda b, j: (b, pl.multiple_of(CHUNK + j * tm, CHUNK), 0))]
        + _ffn_weight_specs(layer) + [_resident((1, D_MODEL))],
        out_specs=pl.BlockSpec((None, tm, D_MODEL), lambda b, j: (b, j, 0)),
        scratch_shapes=[pltpu.VMEM((tm, D_MODEL), F32)],
        compiler_params=_params("parallel", "parallel"),
        name="ffn_final",
    )(h3, gain, wg, wu, wd, final_gain)


def _rot128(x, cos, sin):
    return x * cos + pltpu.roll(x, 64, axis=1) * sin


def _rot64(x, cos, sin, low_half):
    partner = jnp.where(low_half, pltpu.roll(x, 96, axis=1), pltpu.roll(x, 32, axis=1))
    return x * cos + partner * sin


def _log_gamma_tile(dec_ref, direction, hd):
    return jnp.log1p(-jnp.exp(jnp.full((CHUNK, CHUNK), dec_ref[direction, hd], F32)))


D_QF, D_QB, D_VF, D_VB, D_CB = range(5)


def _row_decay_tables(dec_ref, tab_ref):
    i = lax.broadcasted_iota(jnp.int32, (CHUNK, CHUNK), 0).astype(F32)
    for hd in range(RET_HEADS):
        sl = slice(hd * RET_DIM, (hd + 1) * RET_DIM)
        lgf = _log_gamma_tile(dec_ref, 0, hd)
        lgb = _log_gamma_tile(dec_ref, 1, hd)
        tab_ref[D_QF, :, sl] = jnp.exp(lgf * (i + 1.0))
        tab_ref[D_QB, :, sl] = jnp.exp(lgb * (CHUNK - i))
        tab_ref[D_VF, :, sl] = jnp.exp(lgf * (CHUNK - 1.0 - i))
        tab_ref[D_VB, :, sl] = jnp.exp(lgb * i)
        tab_ref[D_CB, :, sl] = jnp.exp(lgb * CHUNK)


def _inproj_kernel(dec_ref, x_ref, xp_ref, xn_ref, g_ref, w_ref, cw_ref, cr_ref, sr_ref, ca_ref, sa_ref,
                   rp_ref, rkt_ref, sbs_ref, yc_ref, ap_ref, avt_ref, gs_ref, dtab_ref, us_ref, sb_ref,
                   *, total_rows, seq_rows):
    i = pl.num_programs(0) - 1 - pl.program_id(0)
    tm = x_ref.shape[0]

    @pl.when(pl.program_id(0) == 0)
    def _():
        _row_decay_tables(dec_ref, dtab_ref)
        sb_ref[...] = jnp.zeros_like(sb_ref)

    gain = g_ref[...]
    n = _rms(x_ref[...], gain).astype(BF16)

    def proj(off, width):
        return _dot(n, w_ref[:, off:off + width])

    cr, sr = cr_ref[...], sr_ref[...]
    ca, sa = ca_ref[...], sa_ref[...]
    low_half = (lax.broadcasted_iota(jnp.int32, (1, LANES), 1) % ATT_DIM) < (ATT_DIM // 2)
    chunks = [slice(ch * CHUNK, (ch + 1) * CHUNK) for ch in range(tm // CHUNK)]
    kts = {}
    for c in range(RET_WIDTH // MXU_COLS):
        zq = proj(OFF_RQ + c * MXU_COLS, MXU_COLS)
        zk = proj(OFF_RK + c * MXU_COLS, MXU_COLS)
        for half in range(MXU_COLS // RET_DIM):
            hd = c * (MXU_COLS // RET_DIM) + half
            src = slice(half * RET_DIM, (half + 1) * RET_DIM)
            dst = slice(hd * RET_DIM, (hd + 1) * RET_DIM)
            q = _rot128(zq[:, src], cr, sr)
            k = _rot128(zk[:, src], cr, sr) * (RET_DIM ** -0.5)
            rp_ref[:, RP_Q + hd * RET_DIM:RP_Q + (hd + 1) * RET_DIM] = q.astype(BF16)
            for ch, rows in enumerate(chunks):
                lo = RP_QD + 2 * hd * RET_DIM
                rp_ref[rows, lo:lo + RET_DIM] = (q[rows] * dtab_ref[D_QF, :, dst]).astype(BF16)
                rp_ref[rows, lo + RET_DIM:lo + 2 * RET_DIM] = (q[rows] * dtab_ref[D_QB, :, dst]).astype(BF16)
                kts[(ch, hd)] = k[rows].T.astype(BF16)
                rkt_ref[ch, hd] = kts[(ch, hd)]
    zv = proj(OFF_RV, RET_WIDTH)
    rp_ref[:, RP_V:RP_V + RET_WIDTH] = zv.astype(BF16)
    for rows in chunks:
        rp_ref[rows, RP_VF:RP_VF + RET_WIDTH] = (zv[rows] * dtab_ref[D_VF]).astype(BF16)

    seq_chunks = seq_rows // CHUNK

    def sweep_item(ch, hd):
        def run():
            sl = slice(hd * RET_DIM, (hd + 1) * RET_DIM)
            ends_sequence = lax.rem(i * len(chunks) + ch, seq_chunks) == seq_chunks - 1
            vdb = (zv[chunks[ch], sl] * dtab_ref[D_VB, :, sl]).astype(BF16)
            later = jnp.where(ends_sequence, 0.0, sb_ref[hd])
            sbs_ref[ch, hd] = later.astype(BF16)
            sb_ref[hd] = dtab_ref[D_CB, :, sl] * later + _dot(kts[(ch, hd)], vdb)
        return run

    def gate_item():
        zg = proj(OFF_RG, RET_WIDTH)
        rp_ref[:, RP_G:RP_G + RET_WIDTH] = (zg * jax.nn.sigmoid(zg)).astype(BF16)

    def conv_item():
        n_ext = jnp.concatenate([_rms(xp_ref[...], gain).astype(BF16), n, _rms(xn_ref[...], gain).astype(BF16)],
                                axis=0)
        u = _dot(n_ext, w_ref[:, OFF_CC:OFF_CC + CONV_WIDTH]) * _dot(n_ext, w_ref[:, OFF_CX:OFF_CX + CONV_WIDTH])
        grow = i * tm - HALO_ROWS + lax.broadcasted_iota(jnp.int32, (tm + 2 * HALO_ROWS, 1), 0)
        us_ref[...] = jnp.where((grow >= 0) & (grow < total_rows), u, 0.0)
        conv = (us_ref[pl.ds(HALO_ROWS - 1, tm), :] * cw_ref[0:1, :]
                + us_ref[pl.ds(HALO_ROWS, tm), :] * cw_ref[1:2, :]
                + us_ref[pl.ds(HALO_ROWS + 1, tm), :] * cw_ref[2:3, :])
        yc_ref[...] = (proj(OFF_CB, CONV_WIDTH) * conv).astype(BF16)

    def query_item(c):
        def run():
            zq = proj(OFF_AQ + c * MXU_COLS, MXU_COLS)
            for half in range(MXU_COLS // LANES):
                src = slice(half * LANES, (half + 1) * LANES)
                dst = slice(c * MXU_COLS + half * LANES, c * MXU_COLS + (half + 1) * LANES)
                ap_ref[:, dst] = (_rot64(zq[:, src], ca, sa, low_half) * ATT_Q_SCALE).astype(BF16)
        return run

    def kv_item():
        zkv = proj(OFF_AK, 2 * ATT_KV_WIDTH)
        ap_ref[:, AP_K:AP_K + ATT_KV_WIDTH] = _rot64(zkv[:, :ATT_KV_WIDTH], ca, sa, low_half).astype(BF16)
        for ch, rows in enumerate(chunks):
            avt_ref[ch] = zkv[rows, ATT_KV_WIDTH:].T.astype(BF16)

    def merge_gate_item(c):
        def run():
            sl = slice(c * GATE_CHUNK, (c + 1) * GATE_CHUNK)
            gs_ref[:, sl] = jax.nn.sigmoid(proj(OFF_GATE + c * GATE_CHUNK, GATE_CHUNK)).astype(BF16)
        return run

    projections = ([gate_item, conv_item] + [query_item(c) for c in range(ATT_WIDTH // MXU_COLS)] + [kv_item]
                   + [merge_gate_item(c) for c in range(GATE_WIDTH // GATE_CHUNK)])
    sweep = [sweep_item(ch, hd) for ch in reversed(range(len(chunks))) for hd in range(RET_HEADS)]
    _interleave(projections, sweep)


INPROJ_OUT = ("rp", "rkt", "sbs", "yc", "ap", "avt", "gs")
INPROJ_TILES = ("rkt", "sbs")


def _inproj(h, layer, dec, gain, w_in, conv_w, tabs, seq_rows):
    rows = h.shape[0]
    tm = _row_tile(rows, INPROJ_ROWS)
    nt = rows // tm
    hb = tm // HALO_ROWS
    tile_of = lambda i: nt - 1 - i
    row = lambda w: pl.BlockSpec((tm, w), lambda i: (tile_of(i), 0))
    tab = pl.BlockSpec((pl.Element(tm), pl.Element(LANES)),
                       lambda i: (pl.multiple_of((tile_of(i) * tm) % seq_rows, CHUNK), 0))
    halo_prev = pl.BlockSpec((HALO_ROWS, D_MODEL), lambda i: (jnp.maximum(tile_of(i) * hb - 1, 0), 0))
    halo_next = pl.BlockSpec((HALO_ROWS, D_MODEL),
                              lambda i: (jnp.minimum((tile_of(i) + 1) * hb, rows // HALO_ROWS - 1), 0))
    widths = dict(rp=RP_WIDTH, yc=CONV_WIDTH, ap=AP_WIDTH, gs=GATE_WIDTH)
    tiles_shape = (rows // CHUNK, RET_HEADS, RET_DIM, RET_DIM)
    tiles_spec = pl.BlockSpec((tm // CHUNK,) + tiles_shape[1:], lambda i: (tile_of(i), 0, 0, 0))
    vt_shape = (rows // CHUNK, ATT_KV_WIDTH, CHUNK)
    vt_spec = pl.BlockSpec((tm // CHUNK,) + vt_shape[1:], lambda i: (tile_of(i), 0, 0))

    def shape_of(name):
        return tiles_shape if name in INPROJ_TILES else vt_shape if name == "avt" else (rows, widths[name])

    def spec_of(name):
        return tiles_spec if name in INPROJ_TILES else vt_spec if name == "avt" else row(widths[name])

    outs = pl.pallas_call(
        functools.partial(_inproj_kernel, total_rows=rows, seq_rows=seq_rows),
        out_shape=[jax.ShapeDtypeStruct(shape_of(name), BF16) for name in INPROJ_OUT],
        grid=(nt,),
        in_specs=[pl.BlockSpec(memory_space=pltpu.SMEM), row(D_MODEL), halo_prev, halo_next,
                  _resident((1, D_MODEL), layer), _resident((D_MODEL, IN_WIDTH), layer),
                  _resident((3, CONV_WIDTH), layer), tab, tab, tab, tab],
        out_specs=[spec_of(name) for name in INPROJ_OUT],
        scratch_shapes=[pltpu.VMEM((5, CHUNK, RET_WIDTH), F32),
                        pltpu.VMEM((tm + 2 * HALO_ROWS, CONV_WIDTH), F32),
                        pltpu.VMEM((RET_HEADS, RET_DIM, RET_DIM), F32)],
        compiler_params=_params("arbitrary"),
        name="inproj",
    )(dec, h, h, h, gain, w_in, conv_w, *tabs)
    return dict(zip(INPROJ_OUT, outs))


T_CF, T_MASK = range(2)
N_TABS = 2


def _chunk_tables(dec_ref, tab_ref):
    i = lax.broadcasted_iota(jnp.int32, (CHUNK, CHUNK), 0).astype(F32)
    j = lax.broadcasted_iota(jnp.int32, (CHUNK, CHUNK), 1).astype(F32)
    diff = i - j
    for hd in range(RET_HEADS):
        lgf = _log_gamma_tile(dec_ref, 0, hd)
        lgb = _log_gamma_tile(dec_ref, 1, hd)
        base = hd * N_TABS
        tab_ref[base + T_CF] = jnp.exp(lgf * CHUNK)
        tab_ref[base + T_MASK] = jnp.where(diff >= 0, jnp.exp(lgf * jnp.maximum(diff, 0.0)),
                                           jnp.exp(lgb * jnp.maximum(-diff, 0.0)))


def _state_update(s_ref, tab_ref, hd, kt, vd, t_c):
    s_ref[hd] = tab_ref[hd * N_TABS + t_c] * s_ref[hd] + _dot(kt, vd)


def _ret_items(rp_ref, kt_ref, sbs_ref, gain_ref, sf_ref, tab_ref, jj, out):
    rows = slice(jj * CHUNK, (jj + 1) * CHUNK)

    def head_item(hd):
        def run():
            sl = slice(hd * RET_DIM, (hd + 1) * RET_DIM)
            part = lambda off, width=RET_DIM: rp_ref[0, rows, off + hd * width:off + (hd + 1) * width]
            kt = kt_ref[0, jj, hd]
            att = _dot(part(RP_Q), kt) * tab_ref[hd * N_TABS + T_MASK]
            o = _dot(att.astype(BF16), part(RP_V))
            sd = jnp.concatenate([sf_ref[hd].astype(BF16), sbs_ref[0, jj, hd]], axis=0)
            o += _dot(part(RP_QD, 2 * RET_DIM), sd)
            _state_update(sf_ref, tab_ref, hd, kt, part(RP_VF), T_CF)
            mu = jnp.mean(o, axis=-1, keepdims=True)
            oc = o - mu
            var = jnp.mean(oc * oc, axis=-1, keepdims=True)
            on = oc * lax.rsqrt(var + EPS) * gain_ref[:, sl]
            out[(jj, hd)] = (part(RP_G).astype(F32) * on).astype(BF16)
        return run

    return [head_item(hd) for hd in range(RET_HEADS)]


def _attn_window(kp_ref, ap_ref, kn_ref, vtp_ref, vt_ref, vtn_ref, blk, seq_rows):
    tq = SEQ_CHUNKS * CHUNK
    span = tq + 2 * CHUNK
    head0 = lax.broadcasted_iota(jnp.int32, (1, LANES), 1) < ATT_DIM
    wpos = blk * tq - CHUNK + lax.broadcasted_iota(jnp.int32, (span, 1), 0)
    inside = (wpos >= 0) & (wpos < seq_rows)
    kall = jnp.concatenate([kp_ref[0], ap_ref[0, :, AP_K:AP_K + ATT_KV_WIDTH], kn_ref[0]], axis=0)
    zero = jnp.zeros_like(kall)
    kall = jnp.where(inside, kall, zero)
    vts = []
    tiles = [vtp_ref[0, 0]] + [vt_ref[0, w] for w in range(SEQ_CHUNKS)] + [vtn_ref[0, 0]]
    for w, vt in enumerate(tiles):
        cpos = blk * tq + (w - 1) * CHUNK + lax.broadcasted_iota(jnp.int32, (1, CHUNK), 1)
        vts.append(jnp.where((cpos >= 0) & (cpos < seq_rows), vt, jnp.zeros_like(vt)))
    return jnp.where(head0, kall, zero), jnp.where(head0, zero, kall), vts


def _attn_items(sink_ref, q_ref, window, blk, jj, seq_rows, out):
    k0, k1, vts = window
    base = jj * CHUNK
    st = {}

    def scores():
        jk = lax.broadcasted_iota(jnp.int32, (3 * CHUNK, CHUNK), 0)
        rq = lax.broadcasted_iota(jnp.int32, (3 * CHUNK, CHUNK), 1)
        kpos = (blk * SEQ_CHUNKS + jj - 1) * CHUNK + jk
        st["ok"] = (jnp.abs(CHUNK + rq - jk) <= WINDOW) & (kpos >= PAD) & (kpos < seq_rows)
        kb = jnp.concatenate([k0[base:base + 3 * CHUNK], k1[base:base + 3 * CHUNK]], axis=0)
        qs = jnp.concatenate([q_ref[0, base:base + CHUNK, c * LANES:(c + 1) * LANES]
                              for c in range(ATT_GROUPS)], axis=0)
        st["s"] = _dot_nt(kb, qs)

    def softmax_item(c):
        def run():
            for kv in range(ATT_KV_HEADS):
                s = st["s"][kv * 3 * CHUNK:(kv + 1) * 3 * CHUNK, c * CHUNK:(c + 1) * CHUNK]
                s = jnp.where(st["ok"], s, NEG_INF)
                sink = sink_ref[kv * ATT_GROUPS + c] * LOG2E
                m = jnp.maximum(jnp.max(s, axis=0, keepdims=True), sink)
                e = jnp.exp2(s - m)
                denom = jnp.sum(e, axis=0, keepdims=True) + jnp.exp2(sink - m)
                st[("e", kv, c)] = e.astype(BF16)
                st[("inv", kv, c)] = 1.0 / denom
        return run

    def values():
        top = lax.broadcasted_iota(jnp.int32, (LANES, 1), 0) < ATT_DIM
        vt3 = jnp.concatenate(vts[jj:jj + 3], axis=1)
        zero = jnp.zeros_like(vt3)
        vbt = jnp.concatenate([jnp.where(top, vt3, zero), jnp.where(top, zero, vt3)], axis=1)
        et = jnp.concatenate([jnp.concatenate([st[("e", kv, c)] for c in range(ATT_GROUPS)], axis=1)
                              for kv in range(ATT_KV_HEADS)], axis=0)
        ot = _dot(vbt, et)
        inv = [jnp.concatenate([st[("inv", kv, c)] for c in range(ATT_GROUPS)], axis=1)
               for kv in range(ATT_KV_HEADS)]
        ot = ot * jnp.where(top, inv[0], inv[1])
        out[jj] = jnp.concatenate([ot[:, c * CHUNK:(c + 1) * CHUNK].T.astype(BF16)
                                   for c in range(ATT_GROUPS)], axis=1)

    return [scores] + [softmax_item(c) for c in range(ATT_GROUPS)] + [values]


def _merge_items(yr_ref, yc_ref, ya_ref, gs_ref, h_ref, o_ref, wr_ref, wc_ref, wa_ref, wo_ref, row0):
    st = {}
    n_tiles = D_MODEL // MXU_COLS
    rows = h_ref.shape[1]

    def branch_item(nt, b, y_of, w_ref):
        def run():
            cols = slice(nt * MXU_COLS, (nt + 1) * MXU_COLS)
            gate = gs_ref[0, :, b * D_MODEL + nt * MXU_COLS:b * D_MODEL + (nt + 1) * MXU_COLS]
            term = gate.astype(F32) * _dot(y_of(), w_ref[:, cols])
            acc = term if b == 0 else st[nt] + term
            st[nt] = acc.astype(BF16) if b == len(branches) - 1 else acc
        return run

    def out_item(nt):
        def run():
            cols = slice(nt * MXU_COLS, (nt + 1) * MXU_COLS)
            merged = jnp.concatenate([st[k] for k in range(n_tiles)], axis=1)
            mix = _dot(merged, wo_ref[:, cols])
            pos = row0 + lax.broadcasted_iota(jnp.int32, (rows, 1), 0)
            o_ref[0, :, cols] = h_ref[0, :, cols] + jnp.where(pos < PAD, 0.0, mix)
        return run

    branches = ((lambda: yr_ref[...], wr_ref), (lambda: yc_ref[0], wc_ref), (lambda: ya_ref[...], wa_ref))
    items = [branch_item(nt, b, y_of, w_ref) for nt in range(n_tiles) for b, (y_of, w_ref) in enumerate(branches)]
    return items + [out_item(nt) for nt in range(n_tiles)]


def _interleave(major, minor):
    done = 0
    for idx, item in enumerate(major):
        item()
        want = (idx + 1) * len(minor) // len(major)
        while done < want:
            minor[done]()
            done += 1


def _mixer_kernel(dec_ref, sink_ref, rp_ref, rkt_ref, sbs_ref, gain_ref, ap_ref, kp_ref, kn_ref,
                  vt_ref, vtp_ref, vtn_ref,
                  h_ref, yc_ref, gs_ref, wr_ref, wc_ref, wa_ref, wo_ref,
                  o_ref, sf_ref, tab_ref, yr_ref, ya_ref, *, seq_rows, n_blocks):
    j = pl.program_id(1)
    tq = SEQ_CHUNKS * CHUNK

    @pl.when(j == 0)
    def _():
        sf_ref[...] = jnp.zeros_like(sf_ref)
        _chunk_tables(dec_ref, tab_ref)

    def seq_items():
        window = _attn_window(kp_ref, ap_ref, kn_ref, vtp_ref, vt_ref, vtn_ref, j, seq_rows)
        yr_parts, ya_parts = {}, {}
        items = []
        for jj in range(SEQ_CHUNKS):
            ret = _ret_items(rp_ref, rkt_ref, sbs_ref, gain_ref, sf_ref, tab_ref, jj, yr_parts)
            att = _attn_items(sink_ref, ap_ref, window, j, jj, seq_rows, ya_parts)
            items += [att[0], ret[0], att[1], ret[1], att[2], att[3], ret[2], att[4], ret[3], att[5]]

        def park():
            for jj in range(SEQ_CHUNKS):
                rows = slice(jj * CHUNK, (jj + 1) * CHUNK)
                ya_ref[j % 2, rows, :] = ya_parts[jj]
                for hd in range(RET_HEADS):
                    yr_ref[j % 2, rows, hd * RET_DIM:(hd + 1) * RET_DIM] = yr_parts[(jj, hd)]
        return items, park

    def merge_items():
        slot = (j + 1) % 2
        return _merge_items(yr_ref.at[slot], yc_ref, ya_ref.at[slot], gs_ref, h_ref, o_ref,
                            wr_ref, wc_ref, wa_ref, wo_ref, (j - 1) * tq)

    @pl.when(j == 0)
    def _():
        items, park = seq_items()
        for item in items:
            item()
        park()

    @pl.when((j > 0) & (j < n_blocks))
    def _():
        items, park = seq_items()
        _interleave(items, merge_items())
        park()

    @pl.when(j == n_blocks)
    def _():
        for item in merge_items():
            item()


def _mixer(h, layer, dec, sink, gn_gain, wr, wc, wa, wo, proj, batch):
    rows = h.shape[0]
    p = rows // batch
    n = p // CHUNK
    tq = SEQ_CHUNKS * CHUNK
    nblk = pl.cdiv(n, SEQ_CHUNKS)
    v3 = lambda t: t.reshape(batch, p, t.shape[-1])
    tile = (RET_DIM, RET_DIM)
    rkt5 = proj["rkt"].reshape((batch, n, RET_HEADS) + tile)
    sbs5 = proj["sbs"].reshape((batch, n, RET_HEADS) + tile)
    ap3 = v3(proj["ap"])
    tiles_spec = pl.BlockSpec((1, SEQ_CHUNKS, RET_HEADS) + tile, lambda b, j: (b, seq_blk(j), 0, 0, 0))

    smem = pl.BlockSpec(memory_space=pltpu.SMEM)
    seq_blk = lambda j: jnp.minimum(j, nblk - 1)
    seq = lambda w: pl.BlockSpec((1, tq, w), lambda b, j: (b, seq_blk(j), 0))
    lag = lambda w: pl.BlockSpec((1, tq, w), lambda b, j: (b, jnp.maximum(j - 1, 0), 0))
    prev_chunk = lambda j: jnp.maximum(seq_blk(j) * SEQ_CHUNKS - 1, 0)
    next_chunk = lambda j: jnp.minimum((seq_blk(j) + 1) * SEQ_CHUNKS, n - 1)
    k_col = AP_K // ATT_KV_WIDTH
    prev = pl.BlockSpec((1, CHUNK, ATT_KV_WIDTH), lambda b, j: (b, prev_chunk(j), k_col))
    nxt = pl.BlockSpec((1, CHUNK, ATT_KV_WIDTH), lambda b, j: (b, next_chunk(j), k_col))
    vt4 = proj["avt"].reshape(batch, n, ATT_KV_WIDTH, CHUNK)
    vt_main = pl.BlockSpec((1, SEQ_CHUNKS, ATT_KV_WIDTH, CHUNK), lambda b, j: (b, seq_blk(j), 0, 0))
    vt_prev = pl.BlockSpec((1, 1, ATT_KV_WIDTH, CHUNK), lambda b, j: (b, prev_chunk(j), 0, 0))
    vt_next = pl.BlockSpec((1, 1, ATT_KV_WIDTH, CHUNK), lambda b, j: (b, next_chunk(j), 0, 0))
    out = pl.pallas_call(
        functools.partial(_mixer_kernel, seq_rows=p, n_blocks=nblk),
        out_shape=jax.ShapeDtypeStruct((batch, p, D_MODEL), F32),
        grid=(batch, nblk + 1),
        in_specs=[smem, smem, seq(RP_WIDTH), tiles_spec, tiles_spec,
                  _resident((1, RET_WIDTH), layer),
                  seq(AP_WIDTH), prev, nxt, vt_main, vt_prev, vt_next,
                  lag(D_MODEL), lag(CONV_WIDTH), lag(GATE_WIDTH),
                  _resident((RET_WIDTH, D_MODEL), layer), _resident((CONV_WIDTH, D_MODEL), layer),
                  _resident((ATT_WIDTH, D_MODEL), layer), _resident((D_MODEL, D_MODEL), layer)],
        out_specs=lag(D_MODEL),
        scratch_shapes=[pltpu.VMEM((RET_HEADS,) + tile, F32),
                        pltpu.VMEM((RET_HEADS * N_TABS, CHUNK, CHUNK), F32),
                        pltpu.VMEM((2, tq, RET_WIDTH), BF16), pltpu.VMEM((2, tq, ATT_WIDTH), BF16)],
        compiler_params=_params("parallel", "arbitrary"),
        name="mixer",
    )(dec, sink, v3(proj["rp"]), rkt5, sbs5, gn_gain, ap3, ap3, ap3, vt4, vt4, vt4,
      v3(h), v3(proj["yc"]), v3(proj["gs"]), wr, wc, wa, wo)
    return out.reshape(rows, D_MODEL)


def _rope_tables(p, tile_rows):
    pos = (jnp.arange(p + tile_rows, dtype=jnp.int32) % p - PAD).astype(F32)[:, None]

    def tab(d):
        inv = ROPE_THETA ** (-jnp.arange(0, d, 2, dtype=F32) / d)
        ang = pos * inv[None, :]
        cos, sin = jnp.cos(ang), jnp.sin(ang)
        reps = LANES // d
        return (jnp.tile(jnp.concatenate([cos, cos], axis=1), (1, reps)),
                jnp.tile(jnp.concatenate([-sin, sin], axis=1), (1, reps)))

    cr, sr = tab(RET_DIM)
    ca, sa = tab(ATT_DIM)
    return cr, sr, ca, sa


def _pair_heads(t, axis):
    shape = t.shape
    t = t.reshape(shape[:axis] + (ATT_KV_HEADS, ATT_GROUPS, ATT_DIM) + shape[axis + 1:])
    t = jnp.swapaxes(t, axis, axis + 1)
    return t.reshape(shape)


def _prep_weights(w_in, w_attn_out):
    aq = _pair_heads(w_in[:, :, OFF_AQ:OFF_AK].astype(BF16), 2)
    w_in = lax.dynamic_update_slice(w_in.astype(BF16), aq, (0, 0, OFF_AQ))
    return w_in, _pair_heads(w_attn_out, 1).astype(BF16)


def _trunk(x, meta_tokens, w, final_norm):
    batch, s, _ = x.shape
    p = s + CHUNK
    rows = batch * p
    meta = jnp.broadcast_to(meta_tokens[None], (batch, N_META, D_MODEL))
    h = jnp.concatenate([jnp.zeros((batch, PAD, D_MODEL), F32), meta, x], axis=1).reshape(rows, D_MODEL)
    tabs = _rope_tables(p, _row_tile(rows, INPROJ_ROWS))
    depth = w["w_in"].shape[0]
    for l in range(depth):
        h = _ffn(h, l, w["norm_ffn1"], w["wg1"], w["wu1"], w["wd1"])
        proj = _inproj(h, l, w["ret_decay"][l], w["norm_mix"], w["w_in"], w["conv_w"], tabs, p)
        h = _mixer(h, l, w["ret_decay"][l], w["attn_sink"][l], w["ret_gn_gain"], w["w_ret_out"],
                   w["w_conv_out"], w["w_attn_out"], w["w_o"], proj, batch)
        if l + 1 < depth:
            h = _ffn(h, l, w["norm_ffn2"], w["wg2"], w["wu2"], w["wd2"])
    return _ffn_final(h.reshape(batch, p, D_MODEL), depth - 1, w["norm_ffn2"], w["wg2"], w["wu2"], w["wd2"],
                      final_norm.reshape(1, D_MODEL))


def kernel(x_prompt, x_sample, meta_tokens, norm_ffn1, w_ffn1_gate, w_ffn1_up, w_ffn1_down, norm_mix, w_in, ret_decay, ret_gn_gain, conv_w, attn_sink, w_ret_out, w_conv_out, w_attn_out, w_o, norm_ffn2, w_ffn2_gate, w_ffn2_up, w_ffn2_down, final_norm):
    depth = w_in.shape[0]
    w_in_b, w_attn_out_b = _prep_weights(w_in, w_attn_out)
    row = lambda t: t.reshape(depth, 1, t.shape[-1])
    w = dict(
        norm_ffn1=row(norm_ffn1), wg1=w_ffn1_gate.astype(BF16), wu1=w_ffn1_up.astype(BF16),
        wd1=w_ffn1_down.astype(BF16), norm_mix=row(norm_mix), w_in=w_in_b, ret_decay=ret_decay,
        ret_gn_gain=row(ret_gn_gain), conv_w=conv_w, attn_sink=attn_sink,
        w_ret_out=w_ret_out.astype(BF16), w_conv_out=w_conv_out.astype(BF16), w_attn_out=w_attn_out_b,
        w_o=w_o.astype(BF16), norm_ffn2=row(norm_ffn2), wg2=w_ffn2_gate.astype(BF16),
        wu2=w_ffn2_up.astype(BF16), wd2=w_ffn2_down.astype(BF16))
    y_prompt = _trunk(x_prompt, meta_tokens, w, final_norm)
    y_sample = _trunk(x_sample, meta_tokens, w, final_norm)
    return (y_prompt, y_sample)
```

```python
import functools
import math

import jax
import jax.numpy as jnp
from jax import lax
from jax.experimental import pallas as pl
from jax.experimental.pallas import tpu as pltpu

F32 = jnp.float32
BF16 = jnp.bfloat16

D_MODEL = 1024
D_FF = 2816
N_META = 16
CHUNK = 128
PAD = CHUNK - N_META
RET_HEADS = 4
RET_DIM = 128
RET_WIDTH = RET_HEADS * RET_DIM
CONV_WIDTH = 512
ATT_Q_HEADS = 8
ATT_KV_HEADS = 2
ATT_DIM = 64
ATT_GROUPS = ATT_Q_HEADS // ATT_KV_HEADS
ATT_WIDTH = ATT_Q_HEADS * ATT_DIM
ATT_KV_WIDTH = ATT_KV_HEADS * ATT_DIM
WINDOW = 128
GATE_WIDTH = 3 * D_MODEL
ROPE_THETA = 10000.0
EPS = 1e-6
NEG_INF = -1e30

LANES = 128
MXU_COLS = 256
HALO_ROWS = 16
FF_CHUNK = MXU_COLS
FFN_ROWS = 1024
INPROJ_ROWS = 512
GATE_CHUNK = 2 * MXU_COLS
SEQ_CHUNKS = 6
VMEM_LIMIT = 56 * 1024 * 1024
LOG2E = math.log2(math.e)
ATT_Q_SCALE = ATT_DIM ** -0.5 * LOG2E

OFF_RQ = 0
OFF_RK = OFF_RQ + RET_WIDTH
OFF_RV = OFF_RK + RET_WIDTH
OFF_RG = OFF_RV + RET_WIDTH
OFF_CB = OFF_RG + RET_WIDTH
OFF_CC = OFF_CB + CONV_WIDTH
OFF_CX = OFF_CC + CONV_WIDTH
OFF_AQ = OFF_CX + CONV_WIDTH
OFF_AK = OFF_AQ + ATT_WIDTH
OFF_AV = OFF_AK + ATT_KV_WIDTH
OFF_GATE = OFF_AV + ATT_KV_WIDTH
IN_WIDTH = OFF_GATE + GATE_WIDTH

RP_Q = 0
RP_QD = RP_Q + RET_WIDTH
RP_V = RP_QD + 2 * RET_WIDTH
RP_VF = RP_V + RET_WIDTH
RP_G = RP_VF + RET_WIDTH
RP_WIDTH = RP_G + RET_WIDTH
AP_K = ATT_WIDTH
AP_WIDTH = AP_K + ATT_KV_WIDTH


def _params(*sem):
    return pltpu.CompilerParams(dimension_semantics=sem, vmem_limit_bytes=VMEM_LIMIT)


def _resident(shape, layer=None):
    nd = len(shape)
    if layer is None:
        return pl.BlockSpec(shape, lambda *_: (0,) * nd, pipeline_mode=pl.Buffered(1))
    return pl.BlockSpec((None,) + tuple(shape), lambda *_: (layer,) + (0,) * nd, pipeline_mode=pl.Buffered(1))


def _row_tile(rows, cap):
    best = CHUNK
    t = CHUNK
    while t <= cap:
        if rows % t == 0:
            best = t
        t += CHUNK
    return best


def _rms(x, gain):
    ms = jnp.mean(x * x, axis=-1, keepdims=True)
    return x * lax.rsqrt(ms + EPS) * gain


def _dot(a, b):
    return jnp.dot(a, b, preferred_element_type=F32)


def _dot_nt(a, b):
    return lax.dot_general(a, b, (((1,), (1,)), ((), ())), preferred_element_type=F32)


def _ffn_body(x, g_ref, wg_ref, wu_ref, wd_ref, acc_ref):
    n = _rms(x, g_ref[...]).astype(BF16)
    for c in range(D_FF // FF_CHUNK):
        sl = slice(c * FF_CHUNK, (c + 1) * FF_CHUNK)
        g = _dot(n, wg_ref[:, sl])
        u = _dot(n, wu_ref[:, sl])
        a = (g * jax.nn.sigmoid(g) * u).astype(BF16)
        d = _dot(a, wd_ref[sl, :])
        if c == 0:
            acc_ref[...] = d
        else:
            acc_ref[...] += d
    return x + 0.5 * acc_ref[...]


def _ffn_kernel(x_ref, g_ref, wg_ref, wu_ref, wd_ref, o_ref, acc_ref):
    o_ref[...] = _ffn_body(x_ref[...], g_ref, wg_ref, wu_ref, wd_ref, acc_ref)


def _ffn_final_kernel(x_ref, g_ref, wg_ref, wu_ref, wd_ref, fg_ref, o_ref, acc_ref):
    o_ref[...] = _rms(_ffn_body(x_ref[0], g_ref, wg_ref, wu_ref, wd_ref, acc_ref), fg_ref[...])


def _ffn_weight_specs(layer):
    return [_resident((1, D_MODEL), layer), _resident((D_MODEL, D_FF), layer), _resident((D_MODEL, D_FF), layer),
            _resident((D_FF, D_MODEL), layer)]


def _ffn(h, layer, gain, wg, wu, wd):
    rows = h.shape[0]
    tm = _row_tile(rows, FFN_ROWS)
    return pl.pallas_call(
        _ffn_kernel,
        out_shape=jax.ShapeDtypeStruct((rows, D_MODEL), F32),
        grid=(rows // tm,),
        in_specs=[pl.BlockSpec((tm, D_MODEL), lambda i: (i, 0))] + _ffn_weight_specs(layer),
        out_specs=pl.BlockSpec((tm, D_MODEL), lambda i: (i, 0)),
        scratch_shapes=[pltpu.VMEM((tm, D_MODEL), F32)],
        compiler_params=_params("parallel"),
        name="ffn",
    )(h, gain, wg, wu, wd)


def _ffn_final(h3, layer, gain, wg, wu, wd, final_gain):
    batch, p, _ = h3.shape
    s = p - CHUNK
    tm = _row_tile(s, FFN_ROWS)
    return pl.pallas_call(
        _ffn_final_kernel,
        out_shape=jax.ShapeDtypeStruct((batch, s, D_MODEL), F32),
        grid=(batch, s // tm),
        in_specs=[pl.BlockSpec((pl.Element(1), pl.Element(tm), pl.Element(D_MODEL)),
                               lambda b, j: (b, pl.multiple_of(CHUNK + j * tm, CHUNK), 0))]
        + _ffn_weight_specs(layer) + [_resident((1, D_MODEL))],
        out_specs=pl.BlockSpec((None, tm, D_MODEL), lambda b, j: (b, j, 0)),
        scratch_shapes=[pltpu.VMEM((tm, D_MODEL), F32)],
        compiler_params=_params("parallel", "parallel"),
        name="ffn_final",
    )(h3, gain, wg, wu, wd, final_gain)


def _rot128(x, cos, sin):
    return x * cos + pltpu.roll(x, 64, axis=1) * sin


def _rot64(x, cos, sin, low_half):
    partner = jnp.where(low_half, pltpu.roll(x, 96, axis=1), pltpu.roll(x, 32, axis=1))
    return x * cos + partner * sin


def _log_gamma_tile(dec_ref, direction, hd):
    return jnp.log1p(-jnp.exp(jnp.full((CHUNK, CHUNK), dec_ref[direction, hd], F32)))


D_QF, D_QB, D_VF, D_VB, D_CB = range(5)


def _row_decay_tables(dec_ref, tab_ref):
    i = lax.broadcasted_iota(jnp.int32, (CHUNK, CHUNK), 0).astype(F32)
    for hd in range(RET_HEADS):
        sl = slice(hd * RET_DIM, (hd + 1) * RET_DIM)
        lgf = _log_gamma_tile(dec_ref, 0, hd)
        lgb = _log_gamma_tile(dec_ref, 1, hd)
        tab_ref[D_QF, :, sl] = jnp.exp(lgf * (i + 1.0))
        tab_ref[D_QB, :, sl] = jnp.exp(lgb * (CHUNK - i))
        tab_ref[D_VF, :, sl] = jnp.exp(lgf * (CHUNK - 1.0 - i))
        tab_ref[D_VB, :, sl] = jnp.exp(lgb * i)
        tab_ref[D_CB, :, sl] = jnp.exp(lgb * CHUNK)


def _inproj_kernel(dec_ref, x_ref, xp_ref, xn_ref, g_ref, w_ref, cw_ref, cr_ref, sr_ref, ca_ref, sa_ref,
                   rp_ref, rkt_ref, sbs_ref, yc_ref, ap_ref, avt_ref, gs_ref, dtab_ref, us_ref, sb_ref,
                   *, total_rows, seq_rows):
    i = pl.num_programs(0) - 1 - pl.program_id(0)
    tm = x_ref.shape[0]

    @pl.when(pl.program_id(0) == 0)
    def _():
        _row_decay_tables(dec_ref, dtab_ref)
        sb_ref[...] = jnp.zeros_like(sb_ref)

    gain = g_ref[...]
    n = _rms(x_ref[...], gain).astype(BF16)

    def proj(off, width):
        return _dot(n, w_ref[:, off:off + width])

    cr, sr = cr_ref[...], sr_ref[...]
    ca, sa = ca_ref[...], sa_ref[...]
    low_half = (lax.broadcasted_iota(jnp.int32, (1, LANES), 1) % ATT_DIM) < (ATT_DIM // 2)
    chunks = [slice(ch * CHUNK, (ch + 1) * CHUNK) for ch in range(tm // CHUNK)]
    kts = {}
    for c in range(RET_WIDTH // MXU_COLS):
        zq = proj(OFF_RQ + c * MXU_COLS, MXU_COLS)
        zk = proj(OFF_RK + c * MXU_COLS, MXU_COLS)
        for half in range(MXU_COLS // RET_DIM):
            hd = c * (MXU_COLS // RET_DIM) + half
            src = slice(half * RET_DIM, (half + 1) * RET_DIM)
            dst = slice(hd * RET_DIM, (hd + 1) * RET_DIM)
            q = _rot128(zq[:, src], cr, sr)
            k = _rot128(zk[:, src], cr, sr) * (RET_DIM ** -0.5)
            rp_ref[:, RP_Q + hd * RET_DIM:RP_Q + (hd + 1) * RET_DIM] = q.astype(BF16)
            for ch, rows in enumerate(chunks):
                lo = RP_QD + 2 * hd * RET_DIM
                rp_ref[rows, lo:lo + RET_DIM] = (q[rows] * dtab_ref[D_QF, :, dst]).astype(BF16)
                rp_ref[rows, lo + RET_DIM:lo + 2 * RET_DIM] = (q[rows] * dtab_ref[D_QB, :, dst]).astype(BF16)
                kts[(ch, hd)] = k[rows].T.astype(BF16)
                rkt_ref[ch, hd] = kts[(ch, hd)]
    zv = proj(OFF_RV, RET_WIDTH)
    rp_ref[:, RP_V:RP_V + RET_WIDTH] = zv.astype(BF16)
    for rows in chunks:
        rp_ref[rows, RP_VF:RP_VF + RET_WIDTH] = (zv[rows] * dtab_ref[D_VF]).astype(BF16)

    seq_chunks = seq_rows // CHUNK

    def sweep_item(ch, hd):
        def run():
            sl = slice(hd * RET_DIM, (hd + 1) * RET_DIM)
            ends_sequence = lax.rem(i * len(chunks) + ch, seq_chunks) == seq_chunks - 1
            vdb = (zv[chunks[ch], sl] * dtab_ref[D_VB, :, sl]).astype(BF16)
            later = jnp.where(ends_sequence, 0.0, sb_ref[hd])
            sbs_ref[ch, hd] = later.astype(BF16)
            sb_ref[hd] = dtab_ref[D_CB, :, sl] * later + _dot(kts[(ch, hd)], vdb)
        return run

    def gate_item():
        zg = proj(OFF_RG, RET_WIDTH)
        rp_ref[:, RP_G:RP_G + RET_WIDTH] = (zg * jax.nn.sigmoid(zg)).astype(BF16)

    def conv_item():
        n_ext = jnp.concatenate([_rms(xp_ref[...], gain).astype(BF16), n, _rms(xn_ref[...], gain).astype(BF16)],
                                axis=0)
        u = _dot(n_ext, w_ref[:, OFF_CC:OFF_CC + CONV_WIDTH]) * _dot(n_ext, w_ref[:, OFF_CX:OFF_CX + CONV_WIDTH])
        grow = i * tm - HALO_ROWS + lax.broadcasted_iota(jnp.int32, (tm + 2 * HALO_ROWS, 1), 0)
        us_ref[...] = jnp.where((grow >= 0) & (grow < total_rows), u, 0.0)
        conv = (us_ref[pl.ds(HALO_ROWS - 1, tm), :] * cw_ref[0:1, :]
                + us_ref[pl.ds(HALO_ROWS, tm), :] * cw_ref[1:2, :]
                + us_ref[pl.ds(HALO_ROWS + 1, tm), :] * cw_ref[2:3, :])
        yc_ref[...] = (proj(OFF_CB, CONV_WIDTH) * conv).astype(BF16)

    def query_item(c):
        def run():
            zq = proj(OFF_AQ + c * MXU_COLS, MXU_COLS)
            for half in range(MXU_COLS // LANES):
                src = slice(half * LANES, (half + 1) * LANES)
                dst = slice(c * MXU_COLS + half * LANES, c * MXU_COLS + (half + 1) * LANES)
                ap_ref[:, dst] = (_rot64(zq[:, src], ca, sa, low_half) * ATT_Q_SCALE).astype(BF16)
        return run

    def kv_item():
        zkv = proj(OFF_AK, 2 * ATT_KV_WIDTH)
        ap_ref[:, AP_K:AP_K + ATT_KV_WIDTH] = _rot64(zkv[:, :ATT_KV_WIDTH], ca, sa, low_half).astype(BF16)
        for ch, rows in enumerate(chunks):
            avt_ref[ch] = zkv[rows, ATT_KV_WIDTH:].T.astype(BF16)

    def merge_gate_item(c):
        def run():
            sl = slice(c * GATE_CHUNK, (c + 1) * GATE_CHUNK)
            gs_ref[:, sl] = jax.nn.sigmoid(proj(OFF_GATE + c * GATE_CHUNK, GATE_CHUNK)).astype(BF16)
        return run

    projections = ([gate_item, conv_item] + [query_item(c) for c in range(ATT_WIDTH // MXU_COLS)] + [kv_item]
                   + [merge_gate_item(c) for c in range(GATE_WIDTH // GATE_CHUNK)])
    sweep = [sweep_item(ch, hd) for ch in reversed(range(len(chunks))) for hd in range(RET_HEADS)]
    _interleave(projections, sweep)


INPROJ_OUT = ("rp", "rkt", "sbs", "yc", "ap", "avt", "gs")
INPROJ_TILES = ("rkt", "sbs")


def _inproj(h, layer, dec, gain, w_in, conv_w, tabs, seq_rows):
    rows = h.shape[0]
    tm = _row_tile(rows, INPROJ_ROWS)
    nt = rows // tm
    hb = tm // HALO_ROWS
    tile_of = lambda i: nt - 1 - i
    row = lambda w: pl.BlockSpec((tm, w), lambda i: (tile_of(i), 0))
    tab = pl.BlockSpec((pl.Element(tm), pl.Element(LANES)),
                       lambda i: (pl.multiple_of((tile_of(i) * tm) % seq_rows, CHUNK), 0))
    halo_prev = pl.BlockSpec((HALO_ROWS, D_MODEL), lambda i: (jnp.maximum(tile_of(i) * hb - 1, 0), 0))
    halo_next = pl.BlockSpec((HALO_ROWS, D_MODEL),
                              lambda i: (jnp.minimum((tile_of(i) + 1) * hb, rows // HALO_ROWS - 1), 0))
    widths = dict(rp=RP_WIDTH, yc=CONV_WIDTH, ap=AP_WIDTH, gs=GATE_WIDTH)
    tiles_shape = (rows // CHUNK, RET_HEADS, RET_DIM, RET_DIM)
    tiles_spec = pl.BlockSpec((tm // CHUNK,) + tiles_shape[1:], lambda i: (tile_of(i), 0, 0, 0))
    vt_shape = (rows // CHUNK, ATT_KV_WIDTH, CHUNK)
    vt_spec = pl.BlockSpec((tm // CHUNK,) + vt_shape[1:], lambda i: (tile_of(i), 0, 0))

    def shape_of(name):
        return tiles_shape if name in INPROJ_TILES else vt_shape if name == "avt" else (rows, widths[name])

    def spec_of(name):
        return tiles_spec if name in INPROJ_TILES else vt_spec if name == "avt" else row(widths[name])

    outs = pl.pallas_call(
        functools.partial(_inproj_kernel, total_rows=rows, seq_rows=seq_rows),
        out_shape=[jax.ShapeDtypeStruct(shape_of(name), BF16) for name in INPROJ_OUT],
        grid=(nt,),
        in_specs=[pl.BlockSpec(memory_space=pltpu.SMEM), row(D_MODEL), halo_prev, halo_next,
                  _resident((1, D_MODEL), layer), _resident((D_MODEL, IN_WIDTH), layer),
                  _resident((3, CONV_WIDTH), layer), tab, tab, tab, tab],
        out_specs=[spec_of(name) for name in INPROJ_OUT],
        scratch_shapes=[pltpu.VMEM((5, CHUNK, RET_WIDTH), F32),
                        pltpu.VMEM((tm + 2 * HALO_ROWS, CONV_WIDTH), F32),
                        pltpu.VMEM((RET_HEADS, RET_DIM, RET_DIM), F32)],
        compiler_params=_params("arbitrary"),
        name="inproj",
    )(dec, h, h, h, gain, w_in, conv_w, *tabs)
    return dict(zip(INPROJ_OUT, outs))


T_CF, T_MASK = range(2)
N_TABS = 2


def _chunk_tables(dec_ref, tab_ref):
    i = lax.broadcasted_iota(jnp.int32, (CHUNK, CHUNK), 0).astype(F32)
    j = lax.broadcasted_iota(jnp.int32, (CHUNK, CHUNK), 1).astype(F32)
    diff = i - j
    for hd in range(RET_HEADS):
        lgf = _log_gamma_tile(dec_ref, 0, hd)
        lgb = _log_gamma_tile(dec_ref, 1, hd)
        base = hd * N_TABS
        tab_ref[base + T_CF] = jnp.exp(lgf * CHUNK)
        tab_ref[base + T_MASK] = jnp.where(diff >= 0, jnp.exp(lgf * jnp.maximum(diff, 0.0)),
                                           jnp.exp(lgb * jnp.maximum(-diff, 0.0)))


def _state_update(s_ref, tab_ref, hd, kt, vd, t_c):
    s_ref[hd] = tab_ref[hd * N_TABS + t_c] * s_ref[hd] + _dot(kt, vd)


def _ret_items(rp_ref, kt_ref, sbs_ref, gain_ref, sf_ref, tab_ref, jj, out):
    rows = slice(jj * CHUNK, (jj + 1) * CHUNK)

    def head_item(hd):
        def run():
            sl = slice(hd * RET_DIM, (hd + 1) * RET_DIM)
            part = lambda off, width=RET_DIM: rp_ref[0, rows, off + hd * width:off + (hd + 1) * width]
            kt = kt_ref[0, jj, hd]
            att = _dot(part(RP_Q), kt) * tab_ref[hd * N_TABS + T_MASK]
            o = _dot(att.astype(BF16), part(RP_V))
            sd = jnp.concatenate([sf_ref[hd].astype(BF16), sbs_ref[0, jj, hd]], axis=0)
            o += _dot(part(RP_QD, 2 * RET_DIM), sd)
            _state_update(sf_ref, tab_ref, hd, kt, part(RP_VF), T_CF)
            mu = jnp.mean(o, axis=-1, keepdims=True)
            oc = o - mu
            var = jnp.mean(oc * oc, axis=-1, keepdims=True)
            on = oc * lax.rsqrt(var + EPS) * gain_ref[:, sl]
            out[(jj, hd)] = (part(RP_G).astype(F32) * on).astype(BF16)
        return run

    return [head_item(hd) for hd in range(RET_HEADS)]


def _attn_window(kp_ref, ap_ref, kn_ref, vtp_ref, vt_ref, vtn_ref, blk, seq_rows):
    tq = SEQ_CHUNKS * CHUNK
    span = tq + 2 * CHUNK
    head0 = lax.broadcasted_iota(jnp.int32, (1, LANES), 1) < ATT_DIM
    wpos = blk * tq - CHUNK + lax.broadcasted_iota(jnp.int32, (span, 1), 0)
    inside = (wpos >= 0) & (wpos < seq_rows)
    kall = jnp.concatenate([kp_ref[0], ap_ref[0, :, AP_K:AP_K + ATT_KV_WIDTH], kn_ref[0]], axis=0)
    zero = jnp.zeros_like(kall)
    kall = jnp.where(inside, kall, zero)
    vts = []
    tiles = [vtp_ref[0, 0]] + [vt_ref[0, w] for w in range(SEQ_CHUNKS)] + [vtn_ref[0, 0]]
    for w, vt in enumerate(tiles):
        cpos = blk * tq + (w - 1) * CHUNK + lax.broadcasted_iota(jnp.int32, (1, CHUNK), 1)
        vts.append(jnp.where((cpos >= 0) & (cpos < seq_rows), vt, jnp.zeros_like(vt)))
    return jnp.where(head0, kall, zero), jnp.where(head0, zero, kall), vts


def _attn_items(sink_ref, q_ref, window, blk, jj, seq_rows, out):
    k0, k1, vts = window
    base = jj * CHUNK
    st = {}

    def scores():
        jk = lax.broadcasted_iota(jnp.int32, (3 * CHUNK, CHUNK), 0)
        rq = lax.broadcasted_iota(jnp.int32, (3 * CHUNK, CHUNK), 1)
        kpos = (blk * SEQ_CHUNKS + jj - 1) * CHUNK + jk
        st["ok"] = (jnp.abs(CHUNK + rq - jk) <= WINDOW) & (kpos >= PAD) & (kpos < seq_rows)
        kb = jnp.concatenate([k0[base:base + 3 * CHUNK], k1[base:base + 3 * CHUNK]], axis=0)
        qs = jnp.concatenate([q_ref[0, base:base + CHUNK, c * LANES:(c + 1) * LANES]
                              for c in range(ATT_GROUPS)], axis=0)
        st["s"] = _dot_nt(kb, qs)

    def softmax_item(c):
        def run():
            for kv in range(ATT_KV_HEADS):
                s = st["s"][kv * 3 * CHUNK:(kv + 1) * 3 * CHUNK, c * CHUNK:(c + 1) * CHUNK]
                s = jnp.where(st["ok"], s, NEG_INF)
                sink = sink_ref[kv * ATT_GROUPS + c] * LOG2E
                m = jnp.maximum(jnp.max(s, axis=0, keepdims=True), sink)
                e = jnp.exp2(s - m)
                denom = jnp.sum(e, axis=0, keepdims=True) + jnp.exp2(sink - m)
                st[("e", kv, c)] = e.astype(BF16)
                st[("inv", kv, c)] = 1.0 / denom
        return run

    def values():
        top = lax.broadcasted_iota(jnp.int32, (LANES, 1), 0) < ATT_DIM
        vt3 = jnp.concatenate(vts[jj:jj + 3], axis=1)
        zero = jnp.zeros_like(vt3)
        vbt = jnp.concatenate([jnp.where(top, vt3, zero), jnp.where(top, zero, vt3)], axis=1)
        et = jnp.concatenate([jnp.concatenate([st[("e", kv, c)] for c in range(ATT_GROUPS)], axis=1)
                              for kv in range(ATT_KV_HEADS)], axis=0)
        ot = _dot(vbt, et)
        inv = [jnp.concatenate([st[("inv", kv, c)] for c in range(ATT_GROUPS)], axis=1)
               for kv in range(ATT_KV_HEADS)]
        ot = ot * jnp.where(top, inv[0], inv[1])
        out[jj] = jnp.concatenate([ot[:, c * CHUNK:(c + 1) * CHUNK].T.astype(BF16)
                                   for c in range(ATT_GROUPS)], axis=1)

    return [scores] + [softmax_item(c) for c in range(ATT_GROUPS)] + [values]


def _merge_items(yr_ref, yc_ref, ya_ref, gs_ref, h_ref, o_ref, wr_ref, wc_ref, wa_ref, wo_ref, row0):
    st = {}
    n_tiles = D_MODEL // MXU_COLS
    rows = h_ref.shape[1]

    def branch_item(nt, b, y_of, w_ref):
        def run():
            cols = slice(nt * MXU_COLS, (nt + 1) * MXU_COLS)
            gate = gs_ref[0, :, b * D_MODEL + nt * MXU_COLS:b * D_MODEL + (nt + 1) * MXU_COLS]
            term = gate.astype(F32) * _dot(y_of(), w_ref[:, cols])
            acc = term if b == 0 else st[nt] + term
            st[nt] = acc.astype(BF16) if b == len(branches) - 1 else acc
        return run

    def out_item(nt):
        def run():
            cols = slice(nt * MXU_COLS, (nt + 1) * MXU_COLS)
            merged = jnp.concatenate([st[k] for k in range(n_tiles)], axis=1)
            mix = _dot(merged, wo_ref[:, cols])
            pos = row0 + lax.broadcasted_iota(jnp.int32, (rows, 1), 0)
            o_ref[0, :, cols] = h_ref[0, :, cols] + jnp.where(pos < PAD, 0.0, mix)
        return run

    branches = ((lambda: yr_ref[...], wr_ref), (lambda: yc_ref[0], wc_ref), (lambda: ya_ref[...], wa_ref))
    items = [branch_item(nt, b, y_of, w_ref) for nt in range(n_tiles) for b, (y_of, w_ref) in enumerate(branches)]
    return items + [out_item(nt) for nt in range(n_tiles)]


def _interleave(major, minor):
    done = 0
    for idx, item in enumerate(major):
        item()
        want = (idx + 1) * len(minor) // len(major)
        while done < want:
            minor[done]()
            done += 1


def _mixer_kernel(dec_ref, sink_ref, rp_ref, rkt_ref, sbs_ref, gain_ref, ap_ref, kp_ref, kn_ref,
                  vt_ref, vtp_ref, vtn_ref,
                  h_ref, yc_ref, gs_ref, wr_ref, wc_ref, wa_ref, wo_ref,
                  o_ref, sf_ref, tab_ref, yr_ref, ya_ref, *, seq_rows, n_blocks):
    j = pl.program_id(1)
    tq = SEQ_CHUNKS * CHUNK

    @pl.when(j == 0)
    def _():
        sf_ref[...] = jnp.zeros_like(sf_ref)
        _chunk_tables(dec_ref, tab_ref)

    def seq_items():
        window = _attn_window(kp_ref, ap_ref, kn_ref, vtp_ref, vt_ref, vtn_ref, j, seq_rows)
        yr_parts, ya_parts = {}, {}
        rets = [_ret_items(rp_ref, rkt_ref, sbs_ref, gain_ref, sf_ref, tab_ref, jj, yr_parts)
                for jj in range(SEQ_CHUNKS)]
        atts = [_attn_items(sink_ref, ap_ref, window, j, jj, seq_rows, ya_parts) for jj in range(SEQ_CHUNKS)]
        items = [atts[0][0]]
        for jj in range(SEQ_CHUNKS):
            ret, att = rets[jj], atts[jj]
            nxt = [atts[jj + 1][0]] if jj + 1 < SEQ_CHUNKS else []
            items += [ret[0], att[1], ret[1], att[2]] + nxt + [att[3], ret[2], att[4], ret[3], att[5]]

        def park():
            for jj in range(SEQ_CHUNKS):
                rows = slice(jj * CHUNK, (jj + 1) * CHUNK)
                ya_ref[j % 2, rows, :] = ya_parts[jj]
                for hd in range(RET_HEADS):
                    yr_ref[j % 2, rows, hd * RET_DIM:(hd + 1) * RET_DIM] = yr_parts[(jj, hd)]
        return items, park

    def merge_items():
        slot = (j + 1) % 2
        return _merge_items(yr_ref.at[slot], yc_ref, ya_ref.at[slot], gs_ref, h_ref, o_ref,
                            wr_ref, wc_ref, wa_ref, wo_ref, (j - 1) * tq)

    @pl.when(j == 0)
    def _():
        items, park = seq_items()
        for item in items:
            item()
        park()

    @pl.when((j > 0) & (j < n_blocks))
    def _():
        items, park = seq_items()
        _interleave(items, merge_items())
        park()

    @pl.when(j == n_blocks)
    def _():
        for item in merge_items():
            item()


def _mixer(h, layer, dec, sink, gn_gain, wr, wc, wa, wo, proj, batch):
    rows = h.shape[0]
    p = rows // batch
    n = p // CHUNK
    tq = SEQ_CHUNKS * CHUNK
    nblk = pl.cdiv(n, SEQ_CHUNKS)
    v3 = lambda t: t.reshape(batch, p, t.shape[-1])
    tile = (RET_DIM, RET_DIM)
    rkt5 = proj["rkt"].reshape((batch, n, RET_HEADS) + tile)
    sbs5 = proj["sbs"].reshape((batch, n, RET_HEADS) + tile)
    ap3 = v3(proj["ap"])
    tiles_spec = pl.BlockSpec((1, SEQ_CHUNKS, RET_HEADS) + tile, lambda b, j: (b, seq_blk(j), 0, 0, 0))

    smem = pl.BlockSpec(memory_space=pltpu.SMEM)
    seq_blk = lambda j: jnp.minimum(j, nblk - 1)
    seq = lambda w: pl.BlockSpec((1, tq, w), lambda b, j: (b, seq_blk(j), 0))
    lag = lambda w: pl.BlockSpec((1, tq, w), lambda b, j: (b, jnp.maximum(j - 1, 0), 0))
    prev_chunk = lambda j: jnp.maximum(seq_blk(j) * SEQ_CHUNKS - 1, 0)
    next_chunk = lambda j: jnp.minimum((seq_blk(j) + 1) * SEQ_CHUNKS, n - 1)
    k_col = AP_K // ATT_KV_WIDTH
    prev = pl.BlockSpec((1, CHUNK, ATT_KV_WIDTH), lambda b, j: (b, prev_chunk(j), k_col))
    nxt = pl.BlockSpec((1, CHUNK, ATT_KV_WIDTH), lambda b, j: (b, next_chunk(j), k_col))
    vt4 = proj["avt"].reshape(batch, n, ATT_KV_WIDTH, CHUNK)
    vt_main = pl.BlockSpec((1, SEQ_CHUNKS, ATT_KV_WIDTH, CHUNK), lambda b, j: (b, seq_blk(j), 0, 0))
    vt_prev = pl.BlockSpec((1, 1, ATT_KV_WIDTH, CHUNK), lambda b, j: (b, prev_chunk(j), 0, 0))
    vt_next = pl.BlockSpec((1, 1, ATT_KV_WIDTH, CHUNK), lambda b, j: (b, next_chunk(j), 0, 0))
    out = pl.pallas_call(
        functools.partial(_mixer_kernel, seq_rows=p, n_blocks=nblk),
        out_shape=jax.ShapeDtypeStruct((batch, p, D_MODEL), F32),
        grid=(batch, nblk + 1),
        in_specs=[smem, smem, seq(RP_WIDTH), tiles_spec, tiles_spec,
                  _resident((1, RET_WIDTH), layer),
                  seq(AP_WIDTH), prev, nxt, vt_main, vt_prev, vt_next,
                  lag(D_MODEL), lag(CONV_WIDTH), lag(GATE_WIDTH),
                  _resident((RET_WIDTH, D_MODEL), layer), _resident((CONV_WIDTH, D_MODEL), layer),
                  _resident((ATT_WIDTH, D_MODEL), layer), _resident((D_MODEL, D_MODEL), layer)],
        out_specs=lag(D_MODEL),
        scratch_shapes=[pltpu.VMEM((RET_HEADS,) + tile, F32),
                        pltpu.VMEM((RET_HEADS * N_TABS, CHUNK, CHUNK), F32),
                        pltpu.VMEM((2, tq, RET_WIDTH), BF16), pltpu.VMEM((2, tq, ATT_WIDTH), BF16)],
        compiler_params=_params("parallel", "arbitrary"),
        name="mixer",
    )(dec, sink, v3(proj["rp"]), rkt5, sbs5, gn_gain, ap3, ap3, ap3, vt4, vt4, vt4,
      v3(h), v3(proj["yc"]), v3(proj["gs"]), wr, wc, wa, wo)
    return out.reshape(rows, D_MODEL)


def _rope_tables(p, tile_rows):
    pos = (jnp.arange(p + tile_rows, dtype=jnp.int32) % p - PAD).astype(F32)[:, None]

    def tab(d):
        inv = ROPE_THETA ** (-jnp.arange(0, d, 2, dtype=F32) / d)
        ang = pos * inv[None, :]
        cos, sin = jnp.cos(ang), jnp.sin(ang)
        reps = LANES // d
        return (jnp.tile(jnp.concatenate([cos, cos], axis=1), (1, reps)),
                jnp.tile(jnp.concatenate([-sin, sin], axis=1), (1, reps)))

    cr, sr = tab(RET_DIM)
    ca, sa = tab(ATT_DIM)
    return cr, sr, ca, sa


def _pair_heads(t, axis):
    shape = t.shape
    t = t.reshape(shape[:axis] + (ATT_KV_HEADS, ATT_GROUPS, ATT_DIM) + shape[axis + 1:])
    t = jnp.swapaxes(t, axis, axis + 1)
    return t.reshape(shape)


def _prep_weights(w_in, w_attn_out):
    aq = _pair_heads(w_in[:, :, OFF_AQ:OFF_AK].astype(BF16), 2)
    w_in = lax.dynamic_update_slice(w_in.astype(BF16), aq, (0, 0, OFF_AQ))
    return w_in, _pair_heads(w_attn_out, 1).astype(BF16)


def _trunk(x, meta_tokens, w, final_norm):
    batch, s, _ = x.shape
    p = s + CHUNK
    rows = batch * p
    meta = jnp.broadcast_to(meta_tokens[None], (batch, N_META, D_MODEL))
    h = jnp.concatenate([jnp.zeros((batch, PAD, D_MODEL), F32), meta, x], axis=1).reshape(rows, D_MODEL)
    tabs = _rope_tables(p, _row_tile(rows, INPROJ_ROWS))
    depth = w["w_in"].shape[0]
    for l in range(depth):
        h = _ffn(h, l, w["norm_ffn1"], w["wg1"], w["wu1"], w["wd1"])
        proj = _inproj(h, l, w["ret_decay"][l], w["norm_mix"], w["w_in"], w["conv_w"], tabs, p)
        h = _mixer(h, l, w["ret_decay"][l], w["attn_sink"][l], w["ret_gn_gain"], w["w_ret_out"],
                   w["w_conv_out"], w["w_attn_out"], w["w_o"], proj, batch)
        if l + 1 < depth:
            h = _ffn(h, l, w["norm_ffn2"], w["wg2"], w["wu2"], w["wd2"])
    return _ffn_final(h.reshape(batch, p, D_MODEL), depth - 1, w["norm_ffn2"], w["wg2"], w["wu2"], w["wd2"],
                      final_norm.reshape(1, D_MODEL))


def kernel(x_prompt, x_sample, meta_tokens, norm_ffn1, w_ffn1_gate, w_ffn1_up, w_ffn1_down, norm_mix, w_in, ret_decay, ret_gn_gain, conv_w, attn_sink, w_ret_out, w_conv_out, w_attn_out, w_o, norm_ffn2, w_ffn2_gate, w_ffn2_up, w_ffn2_down, final_norm):
    depth = w_in.shape[0]
    w_in_b, w_attn_out_b = _prep_weights(w_in, w_attn_out)
    row = lambda t: t.reshape(depth, 1, t.shape[-1])
    w = dict(
        norm_ffn1=row(norm_ffn1), wg1=w_ffn1_gate.astype(BF16), wu1=w_ffn1_up.astype(BF16),
        wd1=w_ffn1_down.astype(BF16), norm_mix=row(norm_mix), w_in=w_in_b, ret_decay=ret_decay,
        ret_gn_gain=row(ret_gn_gain), conv_w=conv_w, attn_sink=attn_sink,
        w_ret_out=w_ret_out.astype(BF16), w_conv_out=w_conv_out.astype(BF16), w_attn_out=w_attn_out_b,
        w_o=w_o.astype(BF16), norm_ffn2=row(norm_ffn2), wg2=w_ffn2_gate.astype(BF16),
        wu2=w_ffn2_up.astype(BF16), wd2=w_ffn2_down.astype(BF16))
    y_prompt = _trunk(x_prompt, meta_tokens, w, final_norm)
    y_sample = _trunk(x_sample, meta_tokens, w, final_norm)
    return (y_prompt, y_sample)
```

```python
import functools
import math

import jax
import jax.numpy as jnp
from jax import lax
from jax.experimental import pallas as pl
from jax.experimental.pallas import tpu as pltpu

F32 = jnp.float32
BF16 = jnp.bfloat16

D_MODEL = 1024
D_FF = 2816
N_META = 16
CHUNK = 128
PAD = CHUNK - N_META
RET_HEADS = 4
RET_DIM = 128
RET_WIDTH = RET_HEADS * RET_DIM
CONV_WIDTH = 512
ATT_Q_HEADS = 8
ATT_KV_HEADS = 2
ATT_DIM = 64
ATT_GROUPS = ATT_Q_HEADS // ATT_KV_HEADS
ATT_WIDTH = ATT_Q_HEADS * ATT_DIM
ATT_KV_WIDTH = ATT_KV_HEADS * ATT_DIM
WINDOW = 128
GATE_WIDTH = 3 * D_MODEL
ROPE_THETA = 10000.0
EPS = 1e-6
NEG_INF = -1e30

LANES = 128
MXU_COLS = 256
HALO_ROWS = 16
FF_CHUNK = MXU_COLS
FFN_ROWS = 1024
INPROJ_ROWS = 768
GATE_CHUNK = 2 * MXU_COLS
SEQ_CHUNKS = 6
VMEM_LIMIT = 58 * 1024 * 1024
LOG2E = math.log2(math.e)
ATT_Q_SCALE = ATT_DIM ** -0.5 * LOG2E

OFF_RQ = 0
OFF_RK = OFF_RQ + RET_WIDTH
OFF_RV = OFF_RK + RET_WIDTH
OFF_RG = OFF_RV + RET_WIDTH
OFF_CB = OFF_RG + RET_WIDTH
OFF_CC = OFF_CB + CONV_WIDTH
OFF_CX = OFF_CC + CONV_WIDTH
OFF_AQ = OFF_CX + CONV_WIDTH
OFF_AK = OFF_AQ + ATT_WIDTH
OFF_AV = OFF_AK + ATT_KV_WIDTH
OFF_GATE = OFF_AV + ATT_KV_WIDTH
IN_WIDTH = OFF_GATE + GATE_WIDTH

RP_Q = 0
RP_QD = RP_Q + RET_WIDTH
RP_V = RP_QD + 2 * RET_WIDTH
RP_VF = RP_V + RET_WIDTH
RP_G = RP_VF + RET_WIDTH
RP_WIDTH = RP_G + RET_WIDTH
AP_K = ATT_WIDTH
AP_WIDTH = AP_K + ATT_KV_WIDTH


def _params(*sem):
    return pltpu.CompilerParams(dimension_semantics=sem, vmem_limit_bytes=VMEM_LIMIT)


def _resident(shape, layer=None):
    nd = len(shape)
    if layer is None:
        return pl.BlockSpec(shape, lambda *_: (0,) * nd, pipeline_mode=pl.Buffered(1))
    return pl.BlockSpec((None,) + tuple(shape), lambda *_: (layer,) + (0,) * nd, pipeline_mode=pl.Buffered(1))


def _row_tile(rows, cap):
    best = CHUNK
    t = CHUNK
    while t <= cap:
        if rows % t == 0:
            best = t
        t += CHUNK
    return best


def _rms(x, gain):
    ms = jnp.mean(x * x, axis=-1, keepdims=True)
    return x * lax.rsqrt(ms + EPS) * gain


def _dot(a, b):
    return jnp.dot(a, b, preferred_element_type=F32)


def _dot_nt(a, b):
    return lax.dot_general(a, b, (((1,), (1,)), ((), ())), preferred_element_type=F32)


def _ffn_body(x, g_ref, wg_ref, wu_ref, wd_ref, acc_ref):
    n = _rms(x, g_ref[...]).astype(BF16)
    for c in range(D_FF // FF_CHUNK):
        sl = slice(c * FF_CHUNK, (c + 1) * FF_CHUNK)
        g = _dot(n, wg_ref[:, sl])
        u = _dot(n, wu_ref[:, sl])
        a = (g * jax.nn.sigmoid(g) * u).astype(BF16)
        d = _dot(a, wd_ref[sl, :])
        if c == 0:
            acc_ref[...] = d
        else:
            acc_ref[...] += d
    return x + 0.5 * acc_ref[...]


def _ffn_kernel(x_ref, g_ref, wg_ref, wu_ref, wd_ref, o_ref, acc_ref):
    o_ref[...] = _ffn_body(x_ref[...], g_ref, wg_ref, wu_ref, wd_ref, acc_ref)


def _ffn_final_kernel(x_ref, g_ref, wg_ref, wu_ref, wd_ref, fg_ref, o_ref, acc_ref):
    o_ref[...] = _rms(_ffn_body(x_ref[0], g_ref, wg_ref, wu_ref, wd_ref, acc_ref), fg_ref[...])


def _ffn_weight_specs(layer):
    return [_resident((1, D_MODEL), layer), _resident((D_MODEL, D_FF), layer), _resident((D_MODEL, D_FF), layer),
            _resident((D_FF, D_MODEL), layer)]


def _ffn(h, layer, gain, wg, wu, wd):
    rows = h.shape[0]
    tm = _row_tile(rows, FFN_ROWS)
    return pl.pallas_call(
        _ffn_kernel,
        out_shape=jax.ShapeDtypeStruct((rows, D_MODEL), F32),
        grid=(rows // tm,),
        in_specs=[pl.BlockSpec((tm, D_MODEL), lambda i: (i, 0))] + _ffn_weight_specs(layer),
        out_specs=pl.BlockSpec((tm, D_MODEL), lambda i: (i, 0)),
        scratch_shapes=[pltpu.VMEM((tm, D_MODEL), F32)],
        compiler_params=_params("parallel"),
        name="ffn",
    )(h, gain, wg, wu, wd)


def _ffn_final(h3, layer, gain, wg, wu, wd, final_gain):
    batch, p, _ = h3.shape
    s = p - CHUNK
    tm = _row_tile(s, FFN_ROWS)
    return pl.pallas_call(
        _ffn_final_kernel,
        out_shape=jax.ShapeDtypeStruct((batch, s, D_MODEL), F32),
        grid=(batch, s // tm),
        in_specs=[pl.BlockSpec((pl.Element(1), pl.Element(tm), pl.Element(D_MODEL)),
                               lambda b, j: (b, pl.multiple_of(CHUNK + j * tm, CHUNK), 0))]
        + _ffn_weight_specs(layer) + [_resident((1, D_MODEL))],
        out_specs=pl.BlockSpec((None, tm, D_MODEL), lambda b, j: (b, j, 0)),
        scratch_shapes=[pltpu.VMEM((tm, D_MODEL), F32)],
        compiler_params=_params("parallel", "parallel"),
        name="ffn_final",
    )(h3, gain, wg, wu, wd, final_gain)


def _rot128(x, cos, sin):
    return x * cos + pltpu.roll(x, 64, axis=1) * sin


def _rot64(x, cos, sin, low_half):
    partner = jnp.where(low_half, pltpu.roll(x, 96, axis=1), pltpu.roll(x, 32, axis=1))
    return x * cos + partner * sin


def _log_gamma_tile(dec_ref, direction, hd):
    return jnp.log1p(-jnp.exp(jnp.full((CHUNK, CHUNK), dec_ref[direction, hd], F32)))


D_QF, D_QB, D_VF, D_VB, D_CB = range(5)


def _row_decay_tables(dec_ref, tab_ref):
    i = lax.broadcasted_iota(jnp.int32, (CHUNK, CHUNK), 0).astype(F32)
    for hd in range(RET_HEADS):
        sl = slice(hd * RET_DIM, (hd + 1) * RET_DIM)
        lgf = _log_gamma_tile(dec_ref, 0, hd)
        lgb = _log_gamma_tile(dec_ref, 1, hd)
        tab_ref[D_QF, :, sl] = jnp.exp(lgf * (i + 1.0))
        tab_ref[D_QB, :, sl] = jnp.exp(lgb * (CHUNK - i))
        tab_ref[D_VF, :, sl] = jnp.exp(lgf * (CHUNK - 1.0 - i))
        tab_ref[D_VB, :, sl] = jnp.exp(lgb * i)
        tab_ref[D_CB, :, sl] = jnp.exp(lgb * CHUNK)


def _inproj_kernel(dec_ref, x_ref, xp_ref, xn_ref, g_ref, w_ref, cw_ref, cr_ref, sr_ref, ca_ref, sa_ref,
                   rp_ref, rkt_ref, sbs_ref, yc_ref, ap_ref, avt_ref, gs_ref, dtab_ref, us_ref, sb_ref,
                   *, total_rows, seq_rows):
    i = pl.num_programs(0) - 1 - pl.program_id(0)
    tm = x_ref.shape[0]

    @pl.when(pl.program_id(0) == 0)
    def _():
        _row_decay_tables(dec_ref, dtab_ref)
        sb_ref[...] = jnp.zeros_like(sb_ref)

    gain = g_ref[...]
    n = _rms(x_ref[...], gain).astype(BF16)

    def proj(off, width):
        return _dot(n, w_ref[:, off:off + width])

    cr, sr = cr_ref[...], sr_ref[...]
    ca, sa = ca_ref[...], sa_ref[...]
    low_half = (lax.broadcasted_iota(jnp.int32, (1, LANES), 1) % ATT_DIM) < (ATT_DIM // 2)
    chunks = [slice(ch * CHUNK, (ch + 1) * CHUNK) for ch in range(tm // CHUNK)]
    kts = {}
    for c in range(RET_WIDTH // MXU_COLS):
        zq = proj(OFF_RQ + c * MXU_COLS, MXU_COLS)
        zk = proj(OFF_RK + c * MXU_COLS, MXU_COLS)
        for half in range(MXU_COLS // RET_DIM):
            hd = c * (MXU_COLS // RET_DIM) + half
            src = slice(half * RET_DIM, (half + 1) * RET_DIM)
            dst = slice(hd * RET_DIM, (hd + 1) * RET_DIM)
            q = _rot128(zq[:, src], cr, sr)
            k = _rot128(zk[:, src], cr, sr) * (RET_DIM ** -0.5)
            rp_ref[:, RP_Q + hd * RET_DIM:RP_Q + (hd + 1) * RET_DIM] = q.astype(BF16)
            for ch, rows in enumerate(chunks):
                lo = RP_QD + 2 * hd * RET_DIM
                rp_ref[rows, lo:lo + RET_DIM] = (q[rows] * dtab_ref[D_QF, :, dst]).astype(BF16)
                rp_ref[rows, lo + RET_DIM:lo + 2 * RET_DIM] = (q[rows] * dtab_ref[D_QB, :, dst]).astype(BF16)
                kts[(ch, hd)] = k[rows].T.astype(BF16)
                rkt_ref[ch, hd] = kts[(ch, hd)]
    zv = proj(OFF_RV, RET_WIDTH)
    rp_ref[:, RP_V:RP_V + RET_WIDTH] = zv.astype(BF16)
    for rows in chunks:
        rp_ref[rows, RP_VF:RP_VF + RET_WIDTH] = (zv[rows] * dtab_ref[D_VF]).astype(BF16)

    seq_chunks = seq_rows // CHUNK

    def sweep_item(ch, hd):
        def run():
            sl = slice(hd * RET_DIM, (hd + 1) * RET_DIM)
            ends_sequence = lax.rem(i * len(chunks) + ch, seq_chunks) == seq_chunks - 1
            vdb = (zv[chunks[ch], sl] * dtab_ref[D_VB, :, sl]).astype(BF16)
            later = jnp.where(ends_sequence, 0.0, sb_ref[hd])
            sbs_ref[ch, hd] = later.astype(BF16)
            sb_ref[hd] = dtab_ref[D_CB, :, sl] * later + _dot(kts[(ch, hd)], vdb)
        return run

    def gate_item():
        zg = proj(OFF_RG, RET_WIDTH)
        rp_ref[:, RP_G:RP_G + RET_WIDTH] = (zg * jax.nn.sigmoid(zg)).astype(BF16)

    def conv_item():
        n_ext = jnp.concatenate([_rms(xp_ref[...], gain).astype(BF16), n, _rms(xn_ref[...], gain).astype(BF16)],
                                axis=0)
        u = _dot(n_ext, w_ref[:, OFF_CC:OFF_CC + CONV_WIDTH]) * _dot(n_ext, w_ref[:, OFF_CX:OFF_CX + CONV_WIDTH])
        grow = i * tm - HALO_ROWS + lax.broadcasted_iota(jnp.int32, (tm + 2 * HALO_ROWS, 1), 0)
        us_ref[...] = jnp.where((grow >= 0) & (grow < total_rows), u, 0.0)
        conv = (us_ref[pl.ds(HALO_ROWS - 1, tm), :] * cw_ref[0:1, :]
                + us_ref[pl.ds(HALO_ROWS, tm), :] * cw_ref[1:2, :]
                + us_ref[pl.ds(HALO_ROWS + 1, tm), :] * cw_ref[2:3, :])
        yc_ref[...] = (proj(OFF_CB, CONV_WIDTH) * conv).astype(BF16)

    def query_item(c):
        def run():
            zq = proj(OFF_AQ + c * MXU_COLS, MXU_COLS)
            for half in range(MXU_COLS // LANES):
                src = slice(half * LANES, (half + 1) * LANES)
                dst = slice(c * MXU_COLS + half * LANES, c * MXU_COLS + (half + 1) * LANES)
                ap_ref[:, dst] = (_rot64(zq[:, src], ca, sa, low_half) * ATT_Q_SCALE).astype(BF16)
        return run

    def kv_item():
        zkv = proj(OFF_AK, 2 * ATT_KV_WIDTH)
        ap_ref[:, AP_K:AP_K + ATT_KV_WIDTH] = _rot64(zkv[:, :ATT_KV_WIDTH], ca, sa, low_half).astype(BF16)
        for ch, rows in enumerate(chunks):
            avt_ref[ch] = zkv[rows, ATT_KV_WIDTH:].T.astype(BF16)

    def merge_gate_item(c):
        def run():
            sl = slice(c * GATE_CHUNK, (c + 1) * GATE_CHUNK)
            gs_ref[:, sl] = jax.nn.sigmoid(proj(OFF_GATE + c * GATE_CHUNK, GATE_CHUNK)).astype(BF16)
        return run

    projections = ([gate_item, conv_item] + [query_item(c) for c in range(ATT_WIDTH // MXU_COLS)] + [kv_item]
                   + [merge_gate_item(c) for c in range(GATE_WIDTH // GATE_CHUNK)])
    sweep = [sweep_item(ch, hd) for ch in reversed(range(len(chunks))) for hd in range(RET_HEADS)]
    _interleave(projections, sweep)


INPROJ_OUT = ("rp", "rkt", "sbs", "yc", "ap", "avt", "gs")
INPROJ_TILES = ("rkt", "sbs")


def _inproj(h, layer, dec, gain, w_in, conv_w, tabs, seq_rows):
    rows = h.shape[0]
    tm = _row_tile(rows, INPROJ_ROWS)
    nt = rows // tm
    hb = tm // HALO_ROWS
    tile_of = lambda i: nt - 1 - i
    row = lambda w: pl.BlockSpec((tm, w), lambda i: (tile_of(i), 0))
    tab = pl.BlockSpec((pl.Element(tm), pl.Element(LANES)),
                       lambda i: (pl.multiple_of((tile_of(i) * tm) % seq_rows, CHUNK), 0))
    halo_prev = pl.BlockSpec((HALO_ROWS, D_MODEL), lambda i: (jnp.maximum(tile_of(i) * hb - 1, 0), 0))
    halo_next = pl.BlockSpec((HALO_ROWS, D_MODEL),
                              lambda i: (jnp.minimum((tile_of(i) + 1) * hb, rows // HALO_ROWS - 1), 0))
    widths = dict(rp=RP_WIDTH, yc=CONV_WIDTH, ap=AP_WIDTH, gs=GATE_WIDTH)
    tiles_shape = (rows // CHUNK, RET_HEADS, RET_DIM, RET_DIM)
    tiles_spec = pl.BlockSpec((tm // CHUNK,) + tiles_shape[1:], lambda i: (tile_of(i), 0, 0, 0))
    vt_shape = (rows // CHUNK, ATT_KV_WIDTH, CHUNK)
    vt_spec = pl.BlockSpec((tm // CHUNK,) + vt_shape[1:], lambda i: (tile_of(i), 0, 0))

    def shape_of(name):
        return tiles_shape if name in INPROJ_TILES else vt_shape if name == "avt" else (rows, widths[name])

    def spec_of(name):
        return tiles_spec if name in INPROJ_TILES else vt_spec if name == "avt" else row(widths[name])

    outs = pl.pallas_call(
        functools.partial(_inproj_kernel, total_rows=rows, seq_rows=seq_rows),
        out_shape=[jax.ShapeDtypeStruct(shape_of(name), BF16) for name in INPROJ_OUT],
        grid=(nt,),
        in_specs=[pl.BlockSpec(memory_space=pltpu.SMEM), row(D_MODEL), halo_prev, halo_next,
                  _resident((1, D_MODEL), layer), _resident((D_MODEL, IN_WIDTH), layer),
                  _resident((3, CONV_WIDTH), layer), tab, tab, tab, tab],
        out_specs=[spec_of(name) for name in INPROJ_OUT],
        scratch_shapes=[pltpu.VMEM((5, CHUNK, RET_WIDTH), F32),
                        pltpu.VMEM((tm + 2 * HALO_ROWS, CONV_WIDTH), F32),
                        pltpu.VMEM((RET_HEADS, RET_DIM, RET_DIM), F32)],
        compiler_params=_params("arbitrary"),
        name="inproj",
    )(dec, h, h, h, gain, w_in, conv_w, *tabs)
    return dict(zip(INPROJ_OUT, outs))


T_CF, T_MASK = range(2)
N_TABS = 2


def _chunk_tables(dec_ref, tab_ref):
    i = lax.broadcasted_iota(jnp.int32, (CHUNK, CHUNK), 0).astype(F32)
    j = lax.broadcasted_iota(jnp.int32, (CHUNK, CHUNK), 1).astype(F32)
    diff = i - j
    for hd in range(RET_HEADS):
        lgf = _log_gamma_tile(dec_ref, 0, hd)
        lgb = _log_gamma_tile(dec_ref, 1, hd)
        base = hd * N_TABS
        tab_ref[base + T_CF] = jnp.exp(lgf * CHUNK)
        tab_ref[base + T_MASK] = jnp.where(diff >= 0, jnp.exp(lgf * jnp.maximum(diff, 0.0)),
                                           jnp.exp(lgb * jnp.maximum(-diff, 0.0)))


def _state_update(s_ref, tab_ref, hd, kt, vd, t_c):
    s_ref[hd] = tab_ref[hd * N_TABS + t_c] * s_ref[hd] + _dot(kt, vd)


def _ret_items(rp_ref, kt_ref, sbs_ref, gain_ref, sf_ref, tab_ref, jj, out):
    rows = slice(jj * CHUNK, (jj + 1) * CHUNK)

    def head_item(hd):
        def run():
            sl = slice(hd * RET_DIM, (hd + 1) * RET_DIM)
            part = lambda off, width=RET_DIM: rp_ref[0, rows, off + hd * width:off + (hd + 1) * width]
            kt = kt_ref[0, jj, hd]
            att = _dot(part(RP_Q), kt) * tab_ref[hd * N_TABS + T_MASK]
            o = _dot(att.astype(BF16), part(RP_V))
            sd = jnp.concatenate([sf_ref[hd].astype(BF16), sbs_ref[0, jj, hd]], axis=0)
            o += _dot(part(RP_QD, 2 * RET_DIM), sd)
            _state_update(sf_ref, tab_ref, hd, kt, part(RP_VF), T_CF)
            mu = jnp.mean(o, axis=-1, keepdims=True)
            oc = o - mu
            var = jnp.mean(oc * oc, axis=-1, keepdims=True)
            on = oc * lax.rsqrt(var + EPS) * gain_ref[:, sl]
            out[(jj, hd)] = (part(RP_G).astype(F32) * on).astype(BF16)
        return run

    return [head_item(hd) for hd in range(RET_HEADS)]


def _attn_window(kp_ref, ap_ref, kn_ref, vtp_ref, vt_ref, vtn_ref, blk, seq_rows):
    tq = SEQ_CHUNKS * CHUNK
    span = tq + 2 * CHUNK
    head0 = lax.broadcasted_iota(jnp.int32, (1, LANES), 1) < ATT_DIM
    wpos = blk * tq - CHUNK + lax.broadcasted_iota(jnp.int32, (span, 1), 0)
    inside = (wpos >= 0) & (wpos < seq_rows)
    kall = jnp.concatenate([kp_ref[0], ap_ref[0, :, AP_K:AP_K + ATT_KV_WIDTH], kn_ref[0]], axis=0)
    zero = jnp.zeros_like(kall)
    kall = jnp.where(inside, kall, zero)
    vts = []
    tiles = [vtp_ref[0, 0]] + [vt_ref[0, w] for w in range(SEQ_CHUNKS)] + [vtn_ref[0, 0]]
    for w, vt in enumerate(tiles):
        cpos = blk * tq + (w - 1) * CHUNK + lax.broadcasted_iota(jnp.int32, (1, CHUNK), 1)
        vts.append(jnp.where((cpos >= 0) & (cpos < seq_rows), vt, jnp.zeros_like(vt)))
    return jnp.where(head0, kall, zero), jnp.where(head0, zero, kall), vts


def _attn_items(sink_ref, q_ref, window, blk, jj, seq_rows, out):
    k0, k1, vts = window
    base = jj * CHUNK
    st = {}

    def scores():
        jk = lax.broadcasted_iota(jnp.int32, (3 * CHUNK, CHUNK), 0)
        rq = lax.broadcasted_iota(jnp.int32, (3 * CHUNK, CHUNK), 1)
        kpos = (blk * SEQ_CHUNKS + jj - 1) * CHUNK + jk
        st["ok"] = (jnp.abs(CHUNK + rq - jk) <= WINDOW) & (kpos >= PAD) & (kpos < seq_rows)
        kb = jnp.concatenate([k0[base:base + 3 * CHUNK], k1[base:base + 3 * CHUNK]], axis=0)
        qs = jnp.concatenate([q_ref[0, base:base + CHUNK, c * LANES:(c + 1) * LANES]
                              for c in range(ATT_GROUPS)], axis=0)
        st["s"] = _dot_nt(kb, qs)

    def softmax_item(c):
        def run():
            for kv in range(ATT_KV_HEADS):
                s = st["s"][kv * 3 * CHUNK:(kv + 1) * 3 * CHUNK, c * CHUNK:(c + 1) * CHUNK]
                s = jnp.where(st["ok"], s, NEG_INF)
                sink = sink_ref[kv * ATT_GROUPS + c] * LOG2E
                m = jnp.maximum(jnp.max(s, axis=0, keepdims=True), sink)
                e = jnp.exp2(s - m)
                denom = jnp.sum(e, axis=0, keepdims=True) + jnp.exp2(sink - m)
                st[("e", kv, c)] = e.astype(BF16)
                st[("inv", kv, c)] = 1.0 / denom
        return run

    def values():
        top = lax.broadcasted_iota(jnp.int32, (LANES, 1), 0) < ATT_DIM
        vt3 = jnp.concatenate(vts[jj:jj + 3], axis=1)
        zero = jnp.zeros_like(vt3)
        vbt = jnp.concatenate([jnp.where(top, vt3, zero), jnp.where(top, zero, vt3)], axis=1)
        et = jnp.concatenate([jnp.concatenate([st[("e", kv, c)] for c in range(ATT_GROUPS)], axis=1)
                              for kv in range(ATT_KV_HEADS)], axis=0)
        ot = _dot(vbt, et)
        inv = [jnp.concatenate([st[("inv", kv, c)] for c in range(ATT_GROUPS)], axis=1)
               for kv in range(ATT_KV_HEADS)]
        ot = ot * jnp.where(top, inv[0], inv[1])
        out[jj] = jnp.concatenate([ot[:, c * CHUNK:(c + 1) * CHUNK].T.astype(BF16)
                                   for c in range(ATT_GROUPS)], axis=1)

    return [scores] + [softmax_item(c) for c in range(ATT_GROUPS)] + [values]


def _merge_items(yr_ref, yc_ref, ya_ref, gs_ref, h_ref, o_ref, wr_ref, wc_ref, wa_ref, wo_ref, row0):
    st = {}
    n_tiles = D_MODEL // MXU_COLS
    rows = h_ref.shape[1]

    def branch_item(nt, b, y_of, w_ref):
        def run():
            cols = slice(nt * MXU_COLS, (nt + 1) * MXU_COLS)
            gate = gs_ref[0, :, b * D_MODEL + nt * MXU_COLS:b * D_MODEL + (nt + 1) * MXU_COLS]
            term = gate.astype(F32) * _dot(y_of(), w_ref[:, cols])
            acc = term if b == 0 else st[nt] + term
            st[nt] = acc.astype(BF16) if b == len(branches) - 1 else acc
        return run

    def out_item(nt, half):
        def run():
            cols = slice(nt * MXU_COLS, (nt + 1) * MXU_COLS)
            rs = pl.ds(half * (rows // 2), rows // 2)
            merged = jnp.concatenate([st[k][half * (rows // 2):(half + 1) * (rows // 2)] for k in range(n_tiles)],
                                     axis=1)
            mix = _dot(merged, wo_ref[:, cols])
            pos = row0 + half * (rows // 2) + lax.broadcasted_iota(jnp.int32, (rows // 2, 1), 0)
            o_ref[0, rs, cols] = h_ref[0, rs, cols] + jnp.where(pos < PAD, 0.0, mix)
        return run

    branches = ((lambda: yr_ref[...], wr_ref), (lambda: yc_ref[0], wc_ref), (lambda: ya_ref[...], wa_ref))
    items = [branch_item(nt, b, y_of, w_ref) for nt in range(n_tiles) for b, (y_of, w_ref) in enumerate(branches)]
    return items + [out_item(nt, half) for nt in range(n_tiles) for half in range(2)]


def _interleave(major, minor):
    done = 0
    for idx, item in enumerate(major):
        item()
        want = (idx + 1) * len(minor) // len(major)
        while done < want:
            minor[done]()
            done += 1


def _mixer_kernel(dec_ref, sink_ref, rp_ref, rkt_ref, sbs_ref, gain_ref, ap_ref, kp_ref, kn_ref,
                  vt_ref, vtp_ref, vtn_ref,
                  h_ref, yc_ref, gs_ref, wr_ref, wc_ref, wa_ref, wo_ref,
                  o_ref, sf_ref, tab_ref, yr_ref, ya_ref, *, seq_rows, n_blocks):
    j = pl.program_id(1)
    tq = SEQ_CHUNKS * CHUNK

    @pl.when(j == 0)
    def _():
        sf_ref[...] = jnp.zeros_like(sf_ref)
        _chunk_tables(dec_ref, tab_ref)

    def seq_items():
        window = _attn_window(kp_ref, ap_ref, kn_ref, vtp_ref, vt_ref, vtn_ref, j, seq_rows)
        yr_parts, ya_parts = {}, {}
        rets = [_ret_items(rp_ref, rkt_ref, sbs_ref, gain_ref, sf_ref, tab_ref, jj, yr_parts)
                for jj in range(SEQ_CHUNKS)]
        atts = [_attn_items(sink_ref, ap_ref, window, j, jj, seq_rows, ya_parts) for jj in range(SEQ_CHUNKS)]
        items = [atts[0][0]]
        for jj in range(SEQ_CHUNKS):
            ret, att = rets[jj], atts[jj]
            nxt = [atts[jj + 1][0]] if jj + 1 < SEQ_CHUNKS else []
            items += [ret[0], att[1], ret[1], att[2]] + nxt + [att[3], ret[2], att[4], ret[3], att[5]]

        def park():
            for jj in range(SEQ_CHUNKS):
                rows = slice(jj * CHUNK, (jj + 1) * CHUNK)
                ya_ref[j % 2, rows, :] = ya_parts[jj]
                for hd in range(RET_HEADS):
                    yr_ref[j % 2, rows, hd * RET_DIM:(hd + 1) * RET_DIM] = yr_parts[(jj, hd)]
        return items, park

    def merge_items():
        slot = (j + 1) % 2
        return _merge_items(yr_ref.at[slot], yc_ref, ya_ref.at[slot], gs_ref, h_ref, o_ref,
                            wr_ref, wc_ref, wa_ref, wo_ref, (j - 1) * tq)

    @pl.when(j == 0)
    def _():
        items, park = seq_items()
        for item in items:
            item()
        park()

    @pl.when((j > 0) & (j < n_blocks))
    def _():
        items, park = seq_items()
        _interleave(items, merge_items())
        park()

    @pl.when(j == n_blocks)
    def _():
        for item in merge_items():
            item()


def _mixer(h, layer, dec, sink, gn_gain, wr, wc, wa, wo, proj, batch):
    rows = h.shape[0]
    p = rows // batch
    n = p // CHUNK
    tq = SEQ_CHUNKS * CHUNK
    nblk = pl.cdiv(n, SEQ_CHUNKS)
    v3 = lambda t: t.reshape(batch, p, t.shape[-1])
    tile = (RET_DIM, RET_DIM)
    rkt5 = proj["rkt"].reshape((batch, n, RET_HEADS) + tile)
    sbs5 = proj["sbs"].reshape((batch, n, RET_HEADS) + tile)
    ap3 = v3(proj["ap"])
    tiles_spec = pl.BlockSpec((1, SEQ_CHUNKS, RET_HEADS) + tile, lambda b, j: (b, seq_blk(j), 0, 0, 0))

    smem = pl.BlockSpec(memory_space=pltpu.SMEM)
    seq_blk = lambda j: jnp.minimum(j, nblk - 1)
    seq = lambda w: pl.BlockSpec((1, tq, w), lambda b, j: (b, seq_blk(j), 0))
    lag = lambda w: pl.BlockSpec((1, tq, w), lambda b, j: (b, jnp.maximum(j - 1, 0), 0))
    prev_chunk = lambda j: jnp.maximum(seq_blk(j) * SEQ_CHUNKS - 1, 0)
    next_chunk = lambda j: jnp.minimum((seq_blk(j) + 1) * SEQ_CHUNKS, n - 1)
    k_col = AP_K // ATT_KV_WIDTH
    prev = pl.BlockSpec((1, CHUNK, ATT_KV_WIDTH), lambda b, j: (b, prev_chunk(j), k_col))
    nxt = pl.BlockSpec((1, CHUNK, ATT_KV_WIDTH), lambda b, j: (b, next_chunk(j), k_col))
    vt4 = proj["avt"].reshape(batch, n, ATT_KV_WIDTH, CHUNK)
    vt_main = pl.BlockSpec((1, SEQ_CHUNKS, ATT_KV_WIDTH, CHUNK), lambda b, j: (b, seq_blk(j), 0, 0))
    vt_prev = pl.BlockSpec((1, 1, ATT_KV_WIDTH, CHUNK), lambda b, j: (b, prev_chunk(j), 0, 0))
    vt_next = pl.BlockSpec((1, 1, ATT_KV_WIDTH, CHUNK), lambda b, j: (b, next_chunk(j), 0, 0))
    out = pl.pallas_call(
        functools.partial(_mixer_kernel, seq_rows=p, n_blocks=nblk),
        out_shape=jax.ShapeDtypeStruct((batch, p, D_MODEL), F32),
        grid=(batch, nblk + 1),
        in_specs=[smem, smem, seq(RP_WIDTH), tiles_spec, tiles_spec,
                  _resident((1, RET_WIDTH), layer),
                  seq(AP_WIDTH), prev, nxt, vt_main, vt_prev, vt_next,
                  lag(D_MODEL), lag(CONV_WIDTH), lag(GATE_WIDTH),
                  _resident((RET_WIDTH, D_MODEL), layer), _resident((CONV_WIDTH, D_MODEL), layer),
                  _resident((ATT_WIDTH, D_MODEL), layer), _resident((D_MODEL, D_MODEL), layer)],
        out_specs=lag(D_MODEL),
        scratch_shapes=[pltpu.VMEM((RET_HEADS,) + tile, F32),
                        pltpu.VMEM((RET_HEADS * N_TABS, CHUNK, CHUNK), F32),
                        pltpu.VMEM((2, tq, RET_WIDTH), BF16), pltpu.VMEM((2, tq, ATT_WIDTH), BF16)],
        compiler_params=_params("parallel", "arbitrary"),
        name="mixer",
    )(dec, sink, v3(proj["rp"]), rkt5, sbs5, gn_gain, ap3, ap3, ap3, vt4, vt4, vt4,
      v3(h), v3(proj["yc"]), v3(proj["gs"]), wr, wc, wa, wo)
    return out.reshape(rows, D_MODEL)


def _rope_tables(p, tile_rows):
    pos = (jnp.arange(p + tile_rows, dtype=jnp.int32) % p - PAD).astype(F32)[:, None]

    def tab(d):
        inv = ROPE_THETA ** (-jnp.arange(0, d, 2, dtype=F32) / d)
        ang = pos * inv[None, :]
        cos, sin = jnp.cos(ang), jnp.sin(ang)
        reps = LANES // d
        return (jnp.tile(jnp.concatenate([cos, cos], axis=1), (1, reps)),
                jnp.tile(jnp.concatenate([-sin, sin], axis=1), (1, reps)))

    cr, sr = tab(RET_DIM)
    ca, sa = tab(ATT_DIM)
    return cr, sr, ca, sa


def _pair_heads(t, axis):
    shape = t.shape
    t = t.reshape(shape[:axis] + (ATT_KV_HEADS, ATT_GROUPS, ATT_DIM) + shape[axis + 1:])
    t = jnp.swapaxes(t, axis, axis + 1)
    return t.reshape(shape)


def _prep_weights(w_in, w_attn_out):
    aq = _pair_heads(w_in[:, :, OFF_AQ:OFF_AK].astype(BF16), 2)
    w_in = lax.dynamic_update_slice(w_in.astype(BF16), aq, (0, 0, OFF_AQ))
    return w_in, _pair_heads(w_attn_out, 1).astype(BF16)


def _trunk(x, meta_tokens, w, final_norm):
    batch, s, _ = x.shape
    p = s + CHUNK
    rows = batch * p
    meta = jnp.broadcast_to(meta_tokens[None], (batch, N_META, D_MODEL))
    h = jnp.concatenate([jnp.zeros((batch, PAD, D_MODEL), F32), meta, x], axis=1).reshape(rows, D_MODEL)
    tabs = _rope_tables(p, _row_tile(rows, INPROJ_ROWS))
    depth = w["w_in"].shape[0]
    for l in range(depth):
        h = _ffn(h, l, w["norm_ffn1"], w["wg1"], w["wu1"], w["wd1"])
        proj = _inproj(h, l, w["ret_decay"][l], w["norm_mix"], w["w_in"], w["conv_w"], tabs, p)
        h = _mixer(h, l, w["ret_decay"][l], w["attn_sink"][l], w["ret_gn_gain"], w["w_ret_out"],
                   w["w_conv_out"], w["w_attn_out"], w["w_o"], proj, batch)
        if l + 1 < depth:
            h = _ffn(h, l, w["norm_ffn2"], w["wg2"], w["wu2"], w["wd2"])
    return _ffn_final(h.reshape(batch, p, D_MODEL), depth - 1, w["norm_ffn2"], w["wg2"], w["wu2"], w["wd2"],
                      final_norm.reshape(1, D_MODEL))


def kernel(x_prompt, x_sample, meta_tokens, norm_ffn1, w_ffn1_gate, w_ffn1_up, w_ffn1_down, norm_mix, w_in, ret_decay, ret_gn_gain, conv_w, attn_sink, w_ret_out, w_conv_out, w_attn_out, w_o, norm_ffn2, w_ffn2_gate, w_ffn2_up, w_ffn2_down, final_norm):
    depth = w_in.shape[0]
    w_in_b, w_attn_out_b = _prep_weights(w_in, w_attn_out)
    row = lambda t: t.reshape(depth, 1, t.shape[-1])
    w = dict(
        norm_ffn1=row(norm_ffn1), wg1=w_ffn1_gate.astype(BF16), wu1=w_ffn1_up.astype(BF16),
        wd1=w_ffn1_down.astype(BF16), norm_mix=row(norm_mix), w_in=w_in_b, ret_decay=ret_decay,
        ret_gn_gain=row(ret_gn_gain), conv_w=conv_w, attn_sink=attn_sink,
        w_ret_out=w_ret_out.astype(BF16), w_conv_out=w_conv_out.astype(BF16), w_attn_out=w_attn_out_b,
        w_o=w_o.astype(BF16), norm_ffn2=row(norm_ffn2), wg2=w_ffn2_gate.astype(BF16),
        wu2=w_ffn2_up.astype(BF16), wd2=w_ffn2_down.astype(BF16))
    y_prompt = _trunk(x_prompt, meta_tokens, w, final_norm)
    y_sample = _trunk(x_sample, meta_tokens, w, final_norm)
    return (y_prompt, y_sample)
```

```python
import functools
import math

import jax
import jax.numpy as jnp
from jax import lax
from jax.experimental import pallas as pl
from jax.experimental.pallas import tpu as pltpu

F32 = jnp.float32
BF16 = jnp.bfloat16

D_MODEL = 1024
D_FF = 2816
N_META = 16
CHUNK = 128
PAD = CHUNK - N_META
RET_HEADS = 4
RET_DIM = 128
RET_WIDTH = RET_HEADS * RET_DIM
CONV_WIDTH = 512
ATT_Q_HEADS = 8
ATT_KV_HEADS = 2
ATT_DIM = 64
ATT_GROUPS = ATT_Q_HEADS // ATT_KV_HEADS
ATT_WIDTH = ATT_Q_HEADS * ATT_DIM
ATT_KV_WIDTH = ATT_KV_HEADS * ATT_DIM
WINDOW = 128
GATE_WIDTH = 3 * D_MODEL
ROPE_THETA = 10000.0
EPS = 1e-6
NEG_INF = -1e30

LANES = 128
MXU_COLS = 256
HALO_ROWS = 16
FF_CHUNK = MXU_COLS
FFN_ROWS = 1024
INPROJ_ROWS = 768
GATE_CHUNK = 2 * MXU_COLS
SEQ_CHUNKS = 6
OUT_SPLIT = 3
VMEM_LIMIT = 58 * 1024 * 1024
LOG2E = math.log2(math.e)
ATT_Q_SCALE = ATT_DIM ** -0.5 * LOG2E

OFF_RQ = 0
OFF_RK = OFF_RQ + RET_WIDTH
OFF_RV = OFF_RK + RET_WIDTH
OFF_RG = OFF_RV + RET_WIDTH
OFF_CB = OFF_RG + RET_WIDTH
OFF_CC = OFF_CB + CONV_WIDTH
OFF_CX = OFF_CC + CONV_WIDTH
OFF_AQ = OFF_CX + CONV_WIDTH
OFF_AK = OFF_AQ + ATT_WIDTH
OFF_AV = OFF_AK + ATT_KV_WIDTH
OFF_GATE = OFF_AV + ATT_KV_WIDTH
IN_WIDTH = OFF_GATE + GATE_WIDTH

RP_Q = 0
RP_QD = RP_Q + RET_WIDTH
RP_V = RP_QD + 2 * RET_WIDTH
RP_VF = RP_V + RET_WIDTH
RP_G = RP_VF + RET_WIDTH
RP_WIDTH = RP_G + RET_WIDTH
AP_K = ATT_WIDTH
AP_WIDTH = AP_K + ATT_KV_WIDTH


def _params(*sem):
    return pltpu.CompilerParams(dimension_semantics=sem, vmem_limit_bytes=VMEM_LIMIT)


def _resident(shape, layer=None):
    nd = len(shape)
    if layer is None:
        return pl.BlockSpec(shape, lambda *_: (0,) * nd, pipeline_mode=pl.Buffered(1))
    return pl.BlockSpec((None,) + tuple(shape), lambda *_: (layer,) + (0,) * nd, pipeline_mode=pl.Buffered(1))


def _row_tile(rows, cap):
    best = CHUNK
    t = CHUNK
    while t <= cap:
        if rows % t == 0:
            best = t
        t += CHUNK
    return best


def _rms(x, gain):
    ms = jnp.mean(x * x, axis=-1, keepdims=True)
    return x * lax.rsqrt(ms + EPS) * gain


def _dot(a, b):
    return jnp.dot(a, b, preferred_element_type=F32)


def _dot_nt(a, b):
    return lax.dot_general(a, b, (((1,), (1,)), ((), ())), preferred_element_type=F32)


def _ffn_body(x, g_ref, wg_ref, wu_ref, wd_ref, acc_ref):
    n = _rms(x, g_ref[...]).astype(BF16)
    for c in range(D_FF // FF_CHUNK):
        sl = slice(c * FF_CHUNK, (c + 1) * FF_CHUNK)
        g = _dot(n, wg_ref[:, sl])
        u = _dot(n, wu_ref[:, sl])
        a = (g * jax.nn.sigmoid(g) * u).astype(BF16)
        d = _dot(a, wd_ref[sl, :])
        if c == 0:
            acc_ref[...] = d
        else:
            acc_ref[...] += d
    return x + 0.5 * acc_ref[...]


def _ffn_kernel(x_ref, g_ref, wg_ref, wu_ref, wd_ref, o_ref, acc_ref):
    o_ref[...] = _ffn_body(x_ref[...], g_ref, wg_ref, wu_ref, wd_ref, acc_ref)


def _ffn_final_kernel(x_ref, g_ref, wg_ref, wu_ref, wd_ref, fg_ref, o_ref, acc_ref):
    o_ref[...] = _rms(_ffn_body(x_ref[0], g_ref, wg_ref, wu_ref, wd_ref, acc_ref), fg_ref[...])


def _ffn_weight_specs(layer):
    return [_resident((1, D_MODEL), layer), _resident((D_MODEL, D_FF), layer), _resident((D_MODEL, D_FF), layer),
            _resident((D_FF, D_MODEL), layer)]


def _ffn(h, layer, gain, wg, wu, wd):
    rows = h.shape[0]
    tm = _row_tile(rows, FFN_ROWS)
    return pl.pallas_call(
        _ffn_kernel,
        out_shape=jax.ShapeDtypeStruct((rows, D_MODEL), F32),
        grid=(rows // tm,),
        in_specs=[pl.BlockSpec((tm, D_MODEL), lambda i: (i, 0))] + _ffn_weight_specs(layer),
        out_specs=pl.BlockSpec((tm, D_MODEL), lambda i: (i, 0)),
        scratch_shapes=[pltpu.VMEM((tm, D_MODEL), F32)],
        compiler_params=_params("parallel"),
        name="ffn",
    )(h, gain, wg, wu, wd)


def _ffn_final(h3, layer, gain, wg, wu, wd, final_gain):
    batch, p, _ = h3.shape
    s = p - CHUNK
    tm = _row_tile(s, FFN_ROWS)
    return pl.pallas_call(
        _ffn_final_kernel,
        out_shape=jax.ShapeDtypeStruct((batch, s, D_MODEL), F32),
        grid=(batch, s // tm),
        in_specs=[pl.BlockSpec((pl.Element(1), pl.Element(tm), pl.Element(D_MODEL)),
                               lambda b, j: (b, pl.multiple_of(CHUNK + j * tm, CHUNK), 0))]
        + _ffn_weight_specs(layer) + [_resident((1, D_MODEL))],
        out_specs=pl.BlockSpec((None, tm, D_MODEL), lambda b, j: (b, j, 0)),
        scratch_shapes=[pltpu.VMEM((tm, D_MODEL), F32)],
        compiler_params=_params("parallel", "parallel"),
        name="ffn_final",
    )(h3, gain, wg, wu, wd, final_gain)


def _rot128(x, cos, sin):
    return x * cos + pltpu.roll(x, 64, axis=1) * sin


def _rot64(x, cos, sin, low_half):
    partner = jnp.where(low_half, pltpu.roll(x, 96, axis=1), pltpu.roll(x, 32, axis=1))
    return x * cos + partner * sin


def _log_gamma_tile(dec_ref, direction, hd):
    return jnp.log1p(-jnp.exp(jnp.full((CHUNK, CHUNK), dec_ref[direction, hd], F32)))


D_QF, D_QB, D_VF, D_VB, D_CB = range(5)


def _row_decay_tables(dec_ref, tab_ref):
    i = lax.broadcasted_iota(jnp.int32, (CHUNK, CHUNK), 0).astype(F32)
    for hd in range(RET_HEADS):
        sl = slice(hd * RET_DIM, (hd + 1) * RET_DIM)
        lgf = _log_gamma_tile(dec_ref, 0, hd)
        lgb = _log_gamma_tile(dec_ref, 1, hd)
        tab_ref[D_QF, :, sl] = jnp.exp(lgf * (i + 1.0))
        tab_ref[D_QB, :, sl] = jnp.exp(lgb * (CHUNK - i))
        tab_ref[D_VF, :, sl] = jnp.exp(lgf * (CHUNK - 1.0 - i))
        tab_ref[D_VB, :, sl] = jnp.exp(lgb * i)
        tab_ref[D_CB, :, sl] = jnp.exp(lgb * CHUNK)


def _inproj_kernel(dec_ref, x_ref, xp_ref, xn_ref, g_ref, w_ref, cw_ref, cr_ref, sr_ref, ca_ref, sa_ref,
                   rp_ref, rkt_ref, sbs_ref, yc_ref, ap_ref, avt_ref, gs_ref, dtab_ref, us_ref, sb_ref,
                   *, total_rows, seq_rows):
    i = pl.num_programs(0) - 1 - pl.program_id(0)
    tm = x_ref.shape[0]

    @pl.when(pl.program_id(0) == 0)
    def _():
        _row_decay_tables(dec_ref, dtab_ref)
        sb_ref[...] = jnp.zeros_like(sb_ref)

    gain = g_ref[...]
    n = _rms(x_ref[...], gain).astype(BF16)

    def proj(off, width):
        return _dot(n, w_ref[:, off:off + width])

    cr, sr = cr_ref[...], sr_ref[...]
    ca, sa = ca_ref[...], sa_ref[...]
    low_half = (lax.broadcasted_iota(jnp.int32, (1, LANES), 1) % ATT_DIM) < (ATT_DIM // 2)
    chunks = [slice(ch * CHUNK, (ch + 1) * CHUNK) for ch in range(tm // CHUNK)]
    kts = {}
    for c in range(RET_WIDTH // MXU_COLS):
        zq = proj(OFF_RQ + c * MXU_COLS, MXU_COLS)
        zk = proj(OFF_RK + c * MXU_COLS, MXU_COLS)
        for half in range(MXU_COLS // RET_DIM):
            hd = c * (MXU_COLS // RET_DIM) + half
            src = slice(half * RET_DIM, (half + 1) * RET_DIM)
            dst = slice(hd * RET_DIM, (hd + 1) * RET_DIM)
            q = _rot128(zq[:, src], cr, sr)
            k = _rot128(zk[:, src], cr, sr) * (RET_DIM ** -0.5)
            rp_ref[:, RP_Q + hd * RET_DIM:RP_Q + (hd + 1) * RET_DIM] = q.astype(BF16)
            for ch, rows in enumerate(chunks):
                lo = RP_QD + 2 * hd * RET_DIM
                rp_ref[rows, lo:lo + RET_DIM] = (q[rows] * dtab_ref[D_QF, :, dst]).astype(BF16)
                rp_ref[rows, lo + RET_DIM:lo + 2 * RET_DIM] = (q[rows] * dtab_ref[D_QB, :, dst]).astype(BF16)
                kts[(ch, hd)] = k[rows].T.astype(BF16)
                rkt_ref[ch, hd] = kts[(ch, hd)]
    zv = proj(OFF_RV, RET_WIDTH)
    rp_ref[:, RP_V:RP_V + RET_WIDTH] = zv.astype(BF16)
    for rows in chunks:
        rp_ref[rows, RP_VF:RP_VF + RET_WIDTH] = (zv[rows] * dtab_ref[D_VF]).astype(BF16)

    seq_chunks = seq_rows // CHUNK

    def sweep_item(ch, hd):
        def run():
            sl = slice(hd * RET_DIM, (hd + 1) * RET_DIM)
            ends_sequence = lax.rem(i * len(chunks) + ch, seq_chunks) == seq_chunks - 1
            vdb = (zv[chunks[ch], sl] * dtab_ref[D_VB, :, sl]).astype(BF16)
            later = jnp.where(ends_sequence, 0.0, sb_ref[hd])
            sbs_ref[ch, hd] = later.astype(BF16)
            sb_ref[hd] = dtab_ref[D_CB, :, sl] * later + _dot(kts[(ch, hd)], vdb)
        return run

    def gate_item():
        zg = proj(OFF_RG, RET_WIDTH)
        rp_ref[:, RP_G:RP_G + RET_WIDTH] = (zg * jax.nn.sigmoid(zg)).astype(BF16)

    def conv_item():
        n_ext = jnp.concatenate([_rms(xp_ref[...], gain).astype(BF16), n, _rms(xn_ref[...], gain).astype(BF16)],
                                axis=0)
        u = _dot(n_ext, w_ref[:, OFF_CC:OFF_CC + CONV_WIDTH]) * _dot(n_ext, w_ref[:, OFF_CX:OFF_CX + CONV_WIDTH])
        grow = i * tm - HALO_ROWS + lax.broadcasted_iota(jnp.int32, (tm + 2 * HALO_ROWS, 1), 0)
        us_ref[...] = jnp.where((grow >= 0) & (grow < total_rows), u, 0.0)
        conv = (us_ref[pl.ds(HALO_ROWS - 1, tm), :] * cw_ref[0:1, :]
                + us_ref[pl.ds(HALO_ROWS, tm), :] * cw_ref[1:2, :]
                + us_ref[pl.ds(HALO_ROWS + 1, tm), :] * cw_ref[2:3, :])
        yc_ref[...] = (proj(OFF_CB, CONV_WIDTH) * conv).astype(BF16)

    def query_item(c):
        def run():
            zq = proj(OFF_AQ + c * MXU_COLS, MXU_COLS)
            for half in range(MXU_COLS // LANES):
                src = slice(half * LANES, (half + 1) * LANES)
                dst = slice(c * MXU_COLS + half * LANES, c * MXU_COLS + (half + 1) * LANES)
                ap_ref[:, dst] = (_rot64(zq[:, src], ca, sa, low_half) * ATT_Q_SCALE).astype(BF16)
        return run

    def kv_item():
        zkv = proj(OFF_AK, 2 * ATT_KV_WIDTH)
        ap_ref[:, AP_K:AP_K + ATT_KV_WIDTH] = _rot64(zkv[:, :ATT_KV_WIDTH], ca, sa, low_half).astype(BF16)
        for ch, rows in enumerate(chunks):
            avt_ref[ch] = zkv[rows, ATT_KV_WIDTH:].T.astype(BF16)

    def merge_gate_item(c):
        def run():
            sl = slice(c * GATE_CHUNK, (c + 1) * GATE_CHUNK)
            gs_ref[:, sl] = jax.nn.sigmoid(proj(OFF_GATE + c * GATE_CHUNK, GATE_CHUNK)).astype(BF16)
        return run

    projections = ([gate_item, conv_item] + [query_item(c) for c in range(ATT_WIDTH // MXU_COLS)] + [kv_item]
                   + [merge_gate_item(c) for c in range(GATE_WIDTH // GATE_CHUNK)])
    sweep = [sweep_item(ch, hd) for ch in reversed(range(len(chunks))) for hd in range(RET_HEADS)]
    _interleave(projections, sweep)


INPROJ_OUT = ("rp", "rkt", "sbs", "yc", "ap", "avt", "gs")
INPROJ_TILES = ("rkt", "sbs")


def _inproj(h, layer, dec, gain, w_in, conv_w, tabs, seq_rows):
    rows = h.shape[0]
    tm = _row_tile(rows, INPROJ_ROWS)
    nt = rows // tm
    hb = tm // HALO_ROWS
    tile_of = lambda i: nt - 1 - i
    row = lambda w: pl.BlockSpec((tm, w), lambda i: (tile_of(i), 0))
    tab = pl.BlockSpec((pl.Element(tm), pl.Element(LANES)),
                       lambda i: (pl.multiple_of((tile_of(i) * tm) % seq_rows, CHUNK), 0))
    halo_prev = pl.BlockSpec((HALO_ROWS, D_MODEL), lambda i: (jnp.maximum(tile_of(i) * hb - 1, 0), 0))
    halo_next = pl.BlockSpec((HALO_ROWS, D_MODEL),
                              lambda i: (jnp.minimum((tile_of(i) + 1) * hb, rows // HALO_ROWS - 1), 0))
    widths = dict(rp=RP_WIDTH, yc=CONV_WIDTH, ap=AP_WIDTH, gs=GATE_WIDTH)
    tiles_shape = (rows // CHUNK, RET_HEADS, RET_DIM, RET_DIM)
    tiles_spec = pl.BlockSpec((tm // CHUNK,) + tiles_shape[1:], lambda i: (tile_of(i), 0, 0, 0))
    vt_shape = (rows // CHUNK, ATT_KV_WIDTH, CHUNK)
    vt_spec = pl.BlockSpec((tm // CHUNK,) + vt_shape[1:], lambda i: (tile_of(i), 0, 0))

    def shape_of(name):
        return tiles_shape if name in INPROJ_TILES else vt_shape if name == "avt" else (rows, widths[name])

    def spec_of(name):
        return tiles_spec if name in INPROJ_TILES else vt_spec if name == "avt" else row(widths[name])

    outs = pl.pallas_call(
        functools.partial(_inproj_kernel, total_rows=rows, seq_rows=seq_rows),
        out_shape=[jax.ShapeDtypeStruct(shape_of(name), BF16) for name in INPROJ_OUT],
        grid=(nt,),
        in_specs=[pl.BlockSpec(memory_space=pltpu.SMEM), row(D_MODEL), halo_prev, halo_next,
                  _resident((1, D_MODEL), layer), _resident((D_MODEL, IN_WIDTH), layer),
                  _resident((3, CONV_WIDTH), layer), tab, tab, tab, tab],
        out_specs=[spec_of(name) for name in INPROJ_OUT],
        scratch_shapes=[pltpu.VMEM((5, CHUNK, RET_WIDTH), F32),
                        pltpu.VMEM((tm + 2 * HALO_ROWS, CONV_WIDTH), F32),
                        pltpu.VMEM((RET_HEADS, RET_DIM, RET_DIM), F32)],
        compiler_params=_params("arbitrary"),
        name="inproj",
    )(dec, h, h, h, gain, w_in, conv_w, *tabs)
    return dict(zip(INPROJ_OUT, outs))


T_CF, T_MASK = range(2)
N_TABS = 2


def _chunk_tables(dec_ref, tab_ref):
    i = lax.broadcasted_iota(jnp.int32, (CHUNK, CHUNK), 0).astype(F32)
    j = lax.broadcasted_iota(jnp.int32, (CHUNK, CHUNK), 1).astype(F32)
    diff = i - j
    for hd in range(RET_HEADS):
        lgf = _log_gamma_tile(dec_ref, 0, hd)
        lgb = _log_gamma_tile(dec_ref, 1, hd)
        base = hd * N_TABS
        tab_ref[base + T_CF] = jnp.exp(lgf * CHUNK)
        tab_ref[base + T_MASK] = jnp.where(diff >= 0, jnp.exp(lgf * jnp.maximum(diff, 0.0)),
                                           jnp.exp(lgb * jnp.maximum(-diff, 0.0)))


def _state_update(s_ref, tab_ref, hd, kt, vd, t_c):
    s_ref[hd] = tab_ref[hd * N_TABS + t_c] * s_ref[hd] + _dot(kt, vd)


def _ret_items(rp_ref, kt_ref, sbs_ref, gain_ref, sf_ref, tab_ref, jj, out):
    rows = slice(jj * CHUNK, (jj + 1) * CHUNK)

    def head_item(hd):
        def run():
            sl = slice(hd * RET_DIM, (hd + 1) * RET_DIM)
            part = lambda off, width=RET_DIM: rp_ref[0, rows, off + hd * width:off + (hd + 1) * width]
            kt = kt_ref[0, jj, hd]
            att = _dot(part(RP_Q), kt) * tab_ref[hd * N_TABS + T_MASK]
            o = _dot(att.astype(BF16), part(RP_V))
            sd = jnp.concatenate([sf_ref[hd].astype(BF16), sbs_ref[0, jj, hd]], axis=0)
            o += _dot(part(RP_QD, 2 * RET_DIM), sd)
            _state_update(sf_ref, tab_ref, hd, kt, part(RP_VF), T_CF)
            mu = jnp.mean(o, axis=-1, keepdims=True)
            oc = o - mu
            var = jnp.mean(oc * oc, axis=-1, keepdims=True)
            on = oc * lax.rsqrt(var + EPS) * gain_ref[:, sl]
            out[(jj, hd)] = (part(RP_G).astype(F32) * on).astype(BF16)
        return run

    return [head_item(hd) for hd in range(RET_HEADS)]


def _attn_window(kp_ref, ap_ref, kn_ref, vtp_ref, vt_ref, vtn_ref, blk, seq_rows):
    tq = SEQ_CHUNKS * CHUNK
    span = tq + 2 * CHUNK
    head0 = lax.broadcasted_iota(jnp.int32, (1, LANES), 1) < ATT_DIM
    wpos = blk * tq - CHUNK + lax.broadcasted_iota(jnp.int32, (span, 1), 0)
    inside = (wpos >= 0) & (wpos < seq_rows)
    kall = jnp.concatenate([kp_ref[0], ap_ref[0, :, AP_K:AP_K + ATT_KV_WIDTH], kn_ref[0]], axis=0)
    zero = jnp.zeros_like(kall)
    kall = jnp.where(inside, kall, zero)
    vts = []
    tiles = [vtp_ref[0, 0]] + [vt_ref[0, w] for w in range(SEQ_CHUNKS)] + [vtn_ref[0, 0]]
    for w, vt in enumerate(tiles):
        cpos = blk * tq + (w - 1) * CHUNK + lax.broadcasted_iota(jnp.int32, (1, CHUNK), 1)
        vts.append(jnp.where((cpos >= 0) & (cpos < seq_rows), vt, jnp.zeros_like(vt)))
    return jnp.where(head0, kall, zero), jnp.where(head0, zero, kall), vts


def _attn_items(sink_ref, q_ref, window, blk, jj, seq_rows, out):
    k0, k1, vts = window
    base = jj * CHUNK
    st = {}

    def scores():
        jk = lax.broadcasted_iota(jnp.int32, (3 * CHUNK, CHUNK), 0)
        rq = lax.broadcasted_iota(jnp.int32, (3 * CHUNK, CHUNK), 1)
        kpos = (blk * SEQ_CHUNKS + jj - 1) * CHUNK + jk
        st["ok"] = (jnp.abs(CHUNK + rq - jk) <= WINDOW) & (kpos >= PAD) & (kpos < seq_rows)
        kb = jnp.concatenate([k0[base:base + 3 * CHUNK], k1[base:base + 3 * CHUNK]], axis=0)
        qs = jnp.concatenate([q_ref[0, base:base + CHUNK, c * LANES:(c + 1) * LANES]
                              for c in range(ATT_GROUPS)], axis=0)
        st["s"] = _dot_nt(kb, qs)

    def softmax_item(c):
        def run():
            for kv in range(ATT_KV_HEADS):
                s = st["s"][kv * 3 * CHUNK:(kv + 1) * 3 * CHUNK, c * CHUNK:(c + 1) * CHUNK]
                s = jnp.where(st["ok"], s, NEG_INF)
                sink = sink_ref[kv * ATT_GROUPS + c] * LOG2E
                m = jnp.maximum(jnp.max(s, axis=0, keepdims=True), sink)
                e = jnp.exp2(s - m)
                denom = jnp.sum(e, axis=0, keepdims=True) + jnp.exp2(sink - m)
                st[("e", kv, c)] = e.astype(BF16)
                st[("inv", kv, c)] = 1.0 / denom
        return run

    def values():
        top = lax.broadcasted_iota(jnp.int32, (LANES, 1), 0) < ATT_DIM
        vt3 = jnp.concatenate(vts[jj:jj + 3], axis=1)
        zero = jnp.zeros_like(vt3)
        vbt = jnp.concatenate([jnp.where(top, vt3, zero), jnp.where(top, zero, vt3)], axis=1)
        et = jnp.concatenate([jnp.concatenate([st[("e", kv, c)] for c in range(ATT_GROUPS)], axis=1)
                              for kv in range(ATT_KV_HEADS)], axis=0)
        ot = _dot(vbt, et)
        inv = [jnp.concatenate([st[("inv", kv, c)] for c in range(ATT_GROUPS)], axis=1)
               for kv in range(ATT_KV_HEADS)]
        ot = ot * jnp.where(top, inv[0], inv[1])
        out[jj] = jnp.concatenate([ot[:, c * CHUNK:(c + 1) * CHUNK].T.astype(BF16)
                                   for c in range(ATT_GROUPS)], axis=1)

    return [scores] + [softmax_item(c) for c in range(ATT_GROUPS)] + [values]


def _merge_items(yr_ref, yc_ref, ya_ref, gs_ref, h_ref, o_ref, wr_ref, wc_ref, wa_ref, wo_ref, row0):
    st = {}
    n_tiles = D_MODEL // MXU_COLS
    rows = h_ref.shape[1]

    def branch_item(nt, b, y_of, w_ref):
        def run():
            cols = slice(nt * MXU_COLS, (nt + 1) * MXU_COLS)
            gate = gs_ref[0, :, b * D_MODEL + nt * MXU_COLS:b * D_MODEL + (nt + 1) * MXU_COLS]
            term = gate.astype(F32) * _dot(y_of(), w_ref[:, cols])
            acc = term if b == 0 else st[nt] + term
            st[nt] = acc.astype(BF16) if b == len(branches) - 1 else acc
        return run

    part_rows = rows // OUT_SPLIT

    def out_item(nt, part):
        def run():
            cols = slice(nt * MXU_COLS, (nt + 1) * MXU_COLS)
            rs = pl.ds(part * part_rows, part_rows)
            merged = jnp.concatenate([st[k][part * part_rows:(part + 1) * part_rows] for k in range(n_tiles)],
                                     axis=1)
            mix = _dot(merged, wo_ref[:, cols])
            pos = row0 + part * part_rows + lax.broadcasted_iota(jnp.int32, (part_rows, 1), 0)
            o_ref[0, rs, cols] = h_ref[0, rs, cols] + jnp.where(pos < PAD, 0.0, mix)
        return run

    branches = ((lambda: yr_ref[...], wr_ref), (lambda: yc_ref[0], wc_ref), (lambda: ya_ref[...], wa_ref))
    items = [branch_item(nt, b, y_of, w_ref) for nt in range(n_tiles) for b, (y_of, w_ref) in enumerate(branches)]
    return items + [out_item(nt, part) for nt in range(n_tiles) for part in range(OUT_SPLIT)]


def _interleave(major, minor):
    done = 0
    for idx, item in enumerate(major):
        item()
        want = (idx + 1) * len(minor) // len(major)
        while done < want:
            minor[done]()
            done += 1


def _mixer_kernel(dec_ref, sink_ref, rp_ref, rkt_ref, sbs_ref, gain_ref, ap_ref, kp_ref, kn_ref,
                  vt_ref, vtp_ref, vtn_ref,
                  h_ref, yc_ref, gs_ref, wr_ref, wc_ref, wa_ref, wo_ref,
                  o_ref, sf_ref, tab_ref, yr_ref, ya_ref, *, seq_rows, n_blocks):
    j = pl.program_id(1)
    tq = SEQ_CHUNKS * CHUNK

    @pl.when(j == 0)
    def _():
        sf_ref[...] = jnp.zeros_like(sf_ref)
        _chunk_tables(dec_ref, tab_ref)

    def seq_items():
        window = _attn_window(kp_ref, ap_ref, kn_ref, vtp_ref, vt_ref, vtn_ref, j, seq_rows)
        yr_parts, ya_parts = {}, {}
        rets = [_ret_items(rp_ref, rkt_ref, sbs_ref, gain_ref, sf_ref, tab_ref, jj, yr_parts)
                for jj in range(SEQ_CHUNKS)]
        atts = [_attn_items(sink_ref, ap_ref, window, j, jj, seq_rows, ya_parts) for jj in range(SEQ_CHUNKS)]
        items = [atts[0][0]]
        for jj in range(SEQ_CHUNKS):
            ret, att = rets[jj], atts[jj]
            nxt = [atts[jj + 1][0]] if jj + 1 < SEQ_CHUNKS else []
            items += [ret[0], att[1], ret[1], att[2]] + nxt + [att[3], ret[2], att[4], ret[3], att[5]]

        def park():
            for jj in range(SEQ_CHUNKS):
                rows = slice(jj * CHUNK, (jj + 1) * CHUNK)
                ya_ref[j % 2, rows, :] = ya_parts[jj]
                for hd in range(RET_HEADS):
                    yr_ref[j % 2, rows, hd * RET_DIM:(hd + 1) * RET_DIM] = yr_parts[(jj, hd)]
        return items, park

    def merge_items():
        slot = (j + 1) % 2
        return _merge_items(yr_ref.at[slot], yc_ref, ya_ref.at[slot], gs_ref, h_ref, o_ref,
                            wr_ref, wc_ref, wa_ref, wo_ref, (j - 1) * tq)

    @pl.when(j == 0)
    def _():
        items, park = seq_items()
        for item in items:
            item()
        park()

    @pl.when((j > 0) & (j < n_blocks))
    def _():
        items, park = seq_items()
        _interleave(items, merge_items())
        park()

    @pl.when(j == n_blocks)
    def _():
        for item in merge_items():
            item()


def _mixer(h, layer, dec, sink, gn_gain, wr, wc, wa, wo, proj, batch):
    rows = h.shape[0]
    p = rows // batch
    n = p // CHUNK
    tq = SEQ_CHUNKS * CHUNK
    nblk = pl.cdiv(n, SEQ_CHUNKS)
    v3 = lambda t: t.reshape(batch, p, t.shape[-1])
    tile = (RET_DIM, RET_DIM)
    rkt5 = proj["rkt"].reshape((batch, n, RET_HEADS) + tile)
    sbs5 = proj["sbs"].reshape((batch, n, RET_HEADS) + tile)
    ap3 = v3(proj["ap"])
    tiles_spec = pl.BlockSpec((1, SEQ_CHUNKS, RET_HEADS) + tile, lambda b, j: (b, seq_blk(j), 0, 0, 0))

    smem = pl.BlockSpec(memory_space=pltpu.SMEM)
    seq_blk = lambda j: jnp.minimum(j, nblk - 1)
    seq = lambda w: pl.BlockSpec((1, tq, w), lambda b, j: (b, seq_blk(j), 0))
    lag = lambda w: pl.BlockSpec((1, tq, w), lambda b, j: (b, jnp.maximum(j - 1, 0), 0))
    prev_chunk = lambda j: jnp.maximum(seq_blk(j) * SEQ_CHUNKS - 1, 0)
    next_chunk = lambda j: jnp.minimum((seq_blk(j) + 1) * SEQ_CHUNKS, n - 1)
    k_col = AP_K // ATT_KV_WIDTH
    prev = pl.BlockSpec((1, CHUNK, ATT_KV_WIDTH), lambda b, j: (b, prev_chunk(j), k_col))
    nxt = pl.BlockSpec((1, CHUNK, ATT_KV_WIDTH), lambda b, j: (b, next_chunk(j), k_col))
    vt4 = proj["avt"].reshape(batch, n, ATT_KV_WIDTH, CHUNK)
    vt_main = pl.BlockSpec((1, SEQ_CHUNKS, ATT_KV_WIDTH, CHUNK), lambda b, j: (b, seq_blk(j), 0, 0))
    vt_prev = pl.BlockSpec((1, 1, ATT_KV_WIDTH, CHUNK), lambda b, j: (b, prev_chunk(j), 0, 0))
    vt_next = pl.BlockSpec((1, 1, ATT_KV_WIDTH, CHUNK), lambda b, j: (b, next_chunk(j), 0, 0))
    out = pl.pallas_call(
        functools.partial(_mixer_kernel, seq_rows=p, n_blocks=nblk),
        out_shape=jax.ShapeDtypeStruct((batch, p, D_MODEL), F32),
        grid=(batch, nblk + 1),
        in_specs=[smem, smem, seq(RP_WIDTH), tiles_spec, tiles_spec,
                  _resident((1, RET_WIDTH), layer),
                  seq(AP_WIDTH), prev, nxt, vt_main, vt_prev, vt_next,
                  lag(D_MODEL), lag(CONV_WIDTH), lag(GATE_WIDTH),
                  _resident((RET_WIDTH, D_MODEL), layer), _resident((CONV_WIDTH, D_MODEL), layer),
                  _resident((ATT_WIDTH, D_MODEL), layer), _resident((D_MODEL, D_MODEL), layer)],
        out_specs=lag(D_MODEL),
        scratch_shapes=[pltpu.VMEM((RET_HEADS,) + tile, F32),
                        pltpu.VMEM((RET_HEADS * N_TABS, CHUNK, CHUNK), F32),
                        pltpu.VMEM((2, tq, RET_WIDTH), BF16), pltpu.VMEM((2, tq, ATT_WIDTH), BF16)],
        compiler_params=_params("parallel", "arbitrary"),
        name="mixer",
    )(dec, sink, v3(proj["rp"]), rkt5, sbs5, gn_gain, ap3, ap3, ap3, vt4, vt4, vt4,
      v3(h), v3(proj["yc"]), v3(proj["gs"]), wr, wc, wa, wo)
    return out.reshape(rows, D_MODEL)


def _rope_tables(p, tile_rows):
    pos = (jnp.arange(p + tile_rows, dtype=jnp.int32) % p - PAD).astype(F32)[:, None]

    def tab(d):
        inv = ROPE_THETA ** (-jnp.arange(0, d, 2, dtype=F32) / d)
        ang = pos * inv[None, :]
        cos, sin = jnp.cos(ang), jnp.sin(ang)
        reps = LANES // d
        return (jnp.tile(jnp.concatenate([cos, cos], axis=1), (1, reps)),
                jnp.tile(jnp.concatenate([-sin, sin], axis=1), (1, reps)))

    cr, sr = tab(RET_DIM)
    ca, sa = tab(ATT_DIM)
    return cr, sr, ca, sa


def _pair_heads(t, axis):
    shape = t.shape
    t = t.reshape(shape[:axis] + (ATT_KV_HEADS, ATT_GROUPS, ATT_DIM) + shape[axis + 1:])
    t = jnp.swapaxes(t, axis, axis + 1)
    return t.reshape(shape)


def _prep_weights(w_in, w_attn_out):
    aq = _pair_heads(w_in[:, :, OFF_AQ:OFF_AK].astype(BF16), 2)
    w_in = lax.dynamic_update_slice(w_in.astype(BF16), aq, (0, 0, OFF_AQ))
    return w_in, _pair_heads(w_attn_out, 1).astype(BF16)


def _trunk(x, meta_tokens, w, final_norm):
    batch, s, _ = x.shape
    p = s + CHUNK
    rows = batch * p
    meta = jnp.broadcast_to(meta_tokens[None], (batch, N_META, D_MODEL))
    h = jnp.concatenate([jnp.zeros((batch, PAD, D_MODEL), F32), meta, x], axis=1).reshape(rows, D_MODEL)
    tabs = _rope_tables(p, _row_tile(rows, INPROJ_ROWS))
    depth = w["w_in"].shape[0]
    for l in range(depth):
        h = _ffn(h, l, w["norm_ffn1"], w["wg1"], w["wu1"], w["wd1"])
        proj = _inproj(h, l, w["ret_decay"][l], w["norm_mix"], w["w_in"], w["conv_w"], tabs, p)
        h = _mixer(h, l, w["ret_decay"][l], w["attn_sink"][l], w["ret_gn_gain"], w["w_ret_out"],
                   w["w_conv_out"], w["w_attn_out"], w["w_o"], proj, batch)
        if l + 1 < depth:
            h = _ffn(h, l, w["norm_ffn2"], w["wg2"], w["wu2"], w["wd2"])
    return _ffn_final(h.reshape(batch, p, D_MODEL), depth - 1, w["norm_ffn2"], w["wg2"], w["wu2"], w["wd2"],
                      final_norm.reshape(1, D_MODEL))


def kernel(x_prompt, x_sample, meta_tokens, norm_ffn1, w_ffn1_gate, w_ffn1_up, w_ffn1_down, norm_mix, w_in, ret_decay, ret_gn_gain, conv_w, attn_sink, w_ret_out, w_conv_out, w_attn_out, w_o, norm_ffn2, w_ffn2_gate, w_ffn2_up, w_ffn2_down, final_norm):
    depth = w_in.shape[0]
    w_in_b, w_attn_out_b = _prep_weights(w_in, w_attn_out)
    row = lambda t: t.reshape(depth, 1, t.shape[-1])
    w = dict(
        norm_ffn1=row(norm_ffn1), wg1=w_ffn1_gate.astype(BF16), wu1=w_ffn1_up.astype(BF16),
        wd1=w_ffn1_down.astype(BF16), norm_mix=row(norm_mix), w_in=w_in_b, ret_decay=ret_decay,
        ret_gn_gain=row(ret_gn_gain), conv_w=conv_w, attn_sink=attn_sink,
        w_ret_out=w_ret_out.astype(BF16), w_conv_out=w_conv_out.astype(BF16), w_attn_out=w_attn_out_b,
        w_o=w_o.astype(BF16), norm_ffn2=row(norm_ffn2), wg2=w_ffn2_gate.astype(BF16),
        wu2=w_ffn2_up.astype(BF16), wd2=w_ffn2_down.astype(BF16))
    y_prompt = _trunk(x_prompt, meta_tokens, w, final_norm)
    y_sample = _trunk(x_sample, meta_tokens, w, final_norm)
    return (y_prompt, y_sample)
```
